```python
import jax, jax.numpy as jnp
from jax import lax
import numpy as np

D_MODEL = 1024
BATCH = 4
SEQ = 4096
DEPTH = 1

N_META = 16
ATT_HEADS = 16
HEAD_DIM = 64
ATT_WIDTH = ATT_HEADS * HEAD_DIM
POOL_WINDOWS = (2, 4, 8, 16)
POOL_GROUPS = len(POOL_WINDOWS)
POOL_WIDTH = D_MODEL
POOL_GROUP_WIDTH = POOL_WIDTH // POOL_GROUPS
MIX_WIDTH = ATT_WIDTH + POOL_WIDTH
Q_BLOCK = 128
LN_EPS = 1e-5
DEEPNORM_ALPHA = (2.0 * DEPTH) ** 0.25
DEEPNORM_BETA = (8.0 * DEPTH) ** -0.25
SPLITS = (ATT_WIDTH, 2 * ATT_WIDTH, 3 * ATT_WIDTH, 3 * ATT_WIDTH + ATT_HEADS,
          4 * ATT_WIDTH + ATT_HEADS, 4 * ATT_WIDTH + ATT_HEADS + POOL_WIDTH)
IN_COLS = 4 * ATT_WIDTH + ATT_HEADS + 2 * POOL_WIDTH

kernel_name = 'hybrid_fox_pool_deepnorm_layer'


def layer_norm(x, g, b):
    x32 = x.astype(jnp.float32)
    mu = jnp.mean(x32, axis=-1, keepdims=True)
    var = jnp.mean(jnp.square(x32 - mu), axis=-1, keepdims=True)
    y = (x32 - mu) * lax.rsqrt(var + LN_EPS) * g.astype(jnp.float32) + b.astype(jnp.float32)
    return y.astype(x.dtype)


def forgetting_attention(q, k, v, f_logit, b_forget):
    B, L, _ = q.shape
    q = q.reshape(B, L, ATT_HEADS, HEAD_DIM)
    k = k.reshape(B, L, ATT_HEADS, HEAD_DIM)
    v = v.reshape(B, L, ATT_HEADS, HEAD_DIM)
    log_f = jax.nn.log_sigmoid(f_logit.astype(jnp.float32) + b_forget.astype(jnp.float32))
    c = jnp.cumsum(log_f, axis=1)
    c_k = jnp.transpose(c, (0, 2, 1))
    key_pos = jnp.arange(L)
    scale = HEAD_DIM ** -0.5

    def attend(q_blk, c_blk, q_pos):
        s = jnp.einsum('bqhd,bkhd->bhqk', q_blk, k).astype(jnp.float32) * scale
        s = s + jnp.transpose(c_blk, (0, 2, 1))[..., :, None] - c_k[:, :, None, :]
        mask = key_pos[None, :] <= q_pos[:, None]
        s = jnp.where(mask[None, None], s, -jnp.inf)
        p = jax.nn.softmax(s, axis=-1).astype(v.dtype)
        return jnp.einsum('bhqk,bkhd->bqhd', p, v)

    o_meta = attend(q[:, :N_META], c[:, :N_META], jnp.arange(N_META))
    n_blk = (L - N_META) // Q_BLOCK
    q_blocks = jnp.moveaxis(q[:, N_META:].reshape(B, n_blk, Q_BLOCK, ATT_HEADS, HEAD_DIM), 1, 0)
    c_blocks = jnp.moveaxis(c[:, N_META:].reshape(B, n_blk, Q_BLOCK, ATT_HEADS), 1, 0)
    pos_blocks = (N_META + jnp.arange(n_blk * Q_BLOCK)).reshape(n_blk, Q_BLOCK)
    o_real = lax.map(lambda args: attend(*args), (q_blocks, c_blocks, pos_blocks))
    o_real = jnp.moveaxis(o_real, 0, 1).reshape(B, n_blk * Q_BLOCK, ATT_HEADS, HEAD_DIM)
    o = jnp.concatenate([o_meta, o_real], axis=1)
    return o.reshape(B, L, ATT_WIDTH)


def multiscale_pool(u, w_pool, pool_scale):
    B, L, _ = u.shape
    u32 = u.astype(jnp.float32)
    cs = jnp.pad(jnp.cumsum(u32, axis=1), ((0, 0), (1, 0), (0, 0)))
    t = jnp.arange(L)
    outs = []
    for gi, w in enumerate(POOL_WINDOWS):
        sl = slice(gi * POOL_GROUP_WIDTH, (gi + 1) * POOL_GROUP_WIDTH)
        csg = cs[:, :, sl]
        hi = csg[:, 1:]
        lo = jnp.pad(csg[:, :L + 1 - w], ((0, 0), (w - 1, 0), (0, 0)))
        cnt = jnp.minimum(t + 1, w).astype(jnp.float32)[None, :, None]
        outs.append((hi - lo) / cnt - u32[:, :, sl])
    d = jnp.stack(outs, axis=2)
    y = jnp.einsum('blgc,gce->blge', d, w_pool.astype(jnp.float32))
    y = y.reshape(B, L, POOL_WIDTH) * pool_scale.astype(jnp.float32)
    return y.astype(u.dtype)


def hybrid_layer(x, w_in, b_forget, w_pool, pool_scale, w_out, ln_g, ln_b):
    h = jnp.einsum('bld,dc->blc', x, w_in)
    q, k, v, f_logit, g_att, u, g_pool = jnp.split(h, SPLITS, axis=-1)
    a = forgetting_attention(q, k, v, f_logit, b_forget) * jax.nn.silu(g_att)
    p = multiscale_pool(u, w_pool, pool_scale) * jax.nn.silu(g_pool)
    y = jnp.einsum('blc,cd->bld', jnp.concatenate([a, p], axis=-1), w_out)
    return layer_norm(DEEPNORM_ALPHA * x + y, ln_g, ln_b)


def setup_inputs(seed: int = 0) -> dict:
    key = jax.random.key(seed)
    ks = jax.random.split(key, 13)
    f32 = jnp.float32
    x = jax.random.normal(ks[0], (BATCH, SEQ, D_MODEL), f32)
    meta_tokens = jax.random.normal(ks[1], (N_META, D_MODEL), f32)
    ln_in_g = 1.0 + 0.02 * jax.random.normal(ks[2], (D_MODEL,), f32)
    ln_in_b = 0.02 * jax.random.normal(ks[3], (D_MODEL,), f32)
    w_in = jax.random.normal(ks[4], (DEPTH, D_MODEL, IN_COLS), f32) * D_MODEL ** -0.5
    b_forget = (jnp.linspace(1.0, 6.0, ATT_HEADS, dtype=f32)[None, :]
                + 0.1 * jax.random.normal(ks[5], (DEPTH, ATT_HEADS), f32))
    w_pool = jax.random.normal(ks[6], (DEPTH, POOL_GROUPS, POOL_GROUP_WIDTH, POOL_GROUP_WIDTH), f32) * POOL_GROUP_WIDTH ** -0.5
    pool_scale = 1.0 + 0.1 * jax.random.normal(ks[7], (DEPTH, POOL_WIDTH), f32)
    w_out = jax.random.normal(ks[8], (DEPTH, MIX_WIDTH, D_MODEL), f32) * (MIX_WIDTH ** -0.5 * DEEPNORM_BETA)
    ln_g = 1.0 + 0.02 * jax.random.normal(ks[9], (DEPTH, D_MODEL), f32)
    ln_b = 0.02 * jax.random.normal(ks[10], (DEPTH, D_MODEL), f32)
    return {'x': x, 'meta_tokens': meta_tokens, 'ln_in_g': ln_in_g, 'ln_in_b': ln_in_b,
            'w_in': w_in, 'b_forget': b_forget, 'w_pool': w_pool, 'pool_scale': pool_scale,
            'w_out': w_out, 'ln_g': ln_g, 'ln_b': ln_b}


def reference(x, meta_tokens, ln_in_g, ln_in_b, w_in, b_forget, w_pool, pool_scale, w_out, ln_g, ln_b):
    B = x.shape[0]
    meta = jnp.broadcast_to(meta_tokens[None].astype(x.dtype), (B, N_META, D_MODEL))
    h = jnp.concatenate([meta, x], axis=1)
    h = layer_norm(h, ln_in_g, ln_in_b)
    for layer in range(DEPTH):
        h = hybrid_layer(h, w_in[layer], b_forget[layer], w_pool[layer], pool_scale[layer],
                         w_out[layer], ln_g[layer], ln_b[layer])
    return h[:, N_META:]
```

```python
import functools
import math

import numpy as np
import jax
import jax.numpy as jnp
from jax import lax
from jax.experimental import pallas as pl
from jax.experimental.pallas import tpu as pltpu

D_MODEL = 1024
N_META = 16
HEADS = 16
HEAD_DIM = 64
ATT_WIDTH = HEADS * HEAD_DIM
POOL_WINDOWS = (2, 4, 8, 16)
POOL_GROUP_WIDTH = D_MODEL // len(POOL_WINDOWS)
LN_EPS = 1e-5
DEEPNORM_ALPHA = 2.0 ** 0.25
LOG2E = math.log2(math.e)
Q_SCALE = HEAD_DIM ** -0.5 * LOG2E

LANES = 128
KEY_WIDTH = 128
META_PAD = 128
MASKED_BIAS = 1e30
TOKEN_TILE = 512
ATT_TILE = 256
ONES_ROWS = 16
VMEM_LIMIT = 56 * 1024 * 1024

_NT = (((1,), (1,)), ((), ()))


def _layer_norm(x, g, b):
    mu = jnp.mean(x, axis=-1, keepdims=True)
    xc = x - mu
    var = jnp.mean(xc * xc, axis=-1, keepdims=True)
    return xc * lax.rsqrt(var + LN_EPS) * g + b


def _log_sigmoid(z):
    return jnp.minimum(z, 0.0) - jnp.log(1.0 + jnp.exp(-jnp.abs(z)))


def _silu(z):
    return z / (1.0 + jnp.exp(-z))


def _lane_cumsum(x):
    n = x.shape[-1]
    lane = lax.broadcasted_iota(jnp.int32, x.shape, x.ndim - 1)
    d = 1
    while d < n:
        x = x + jnp.where(lane >= d, pltpu.roll(x, d, x.ndim - 1), 0.0)
        d *= 2
    return x


def _bias_columns(c_rows):
    hi = c_rows.astype(jnp.bfloat16).astype(jnp.float32)
    r1 = c_rows - hi
    mid = r1.astype(jnp.bfloat16).astype(jnp.float32)
    lo = (r1 - mid).astype(jnp.bfloat16).astype(jnp.float32)
    packed = hi + pltpu.roll(mid, HEADS, 1) + pltpu.roll(lo, 2 * HEADS, 1)
    return packed.astype(jnp.bfloat16)


def _pool_mix(u_ext, u, wpool_ref, pscale, gate):
    outs = []
    for gi, w in enumerate(POOL_WINDOWS):
        sl = slice(gi * POOL_GROUP_WIDTH, (gi + 1) * POOL_GROUP_WIDTH)
        r = u_ext[:, sl]
        s = 1
        while s < w:
            r = r + pltpu.roll(r, s, 0)
            s *= 2
        d = r[N_META:, :] * (1.0 / w) - u[:, sl]
        outs.append(jnp.dot(d.astype(jnp.bfloat16), wpool_ref[gi],
                            preferred_element_type=jnp.float32))
    y = jnp.concatenate(outs, axis=-1) * pscale
    return y * gate


def _meta_kernel(mt_ref, g_ref, b_ref, wk_ref, wvT_ref, wfT_ref, bf_ref, wu_ref, e_ref,
                 kmeta_ref, vTmeta_ref, umeta_ref):
    hn = _layer_norm(mt_ref[...], g_ref[...], b_ref[...])
    hb = hn.astype(jnp.bfloat16)
    flT = lax.dot_general(wfT_ref[...], hb, _NT, preferred_element_type=jnp.float32)
    row = lax.broadcasted_iota(jnp.int32, flT.shape, 0)
    col = lax.broadcasted_iota(jnp.int32, flT.shape, 1)
    valid = (row < HEADS) & (col < N_META)
    logf = jnp.where(valid, _log_sigmoid(flT + bf_ref[...]), 0.0)
    cT = _lane_cumsum(logf)
    total = cT[:, META_PAD - 1:META_PAD]
    c_rows = ((cT - total) * LOG2E).T
    trow = lax.broadcasted_iota(jnp.int32, c_rows.shape, 0)
    tcol = lax.broadcasted_iota(jnp.int32, c_rows.shape, 1)
    c_rows = jnp.where((trow >= N_META) & (tcol < HEADS), MASKED_BIAS, c_rows)
    aug = jnp.dot(_bias_columns(c_rows), e_ref[...], preferred_element_type=jnp.float32)
    kk = jnp.dot(hb, wk_ref[...], preferred_element_type=jnp.float32)
    kmeta_ref[...] = (kk + aug).astype(jnp.bfloat16)
    vT = lax.dot_general(wvT_ref[...], hb, _NT, preferred_element_type=jnp.float32)
    vTmeta_ref[...] = vT.astype(jnp.bfloat16)
    u = jnp.dot(hb, wu_ref[...], preferred_element_type=jnp.float32)
    umeta_ref[...] = u[:N_META, :]


def _in_proj_kernel(x_ref, g_ref, b_ref, wqT_ref, wk_ref, wvT_ref, wfT_ref, bf_ref, wgT_ref,
                    wu_ref, wgp_ref, wpool_ref, pscale_ref, e_ref, umeta_ref,
                    qT_ref, k_ref, vT_ref, sgT_ref, p_ref, carry_c, carry_u):
    t = pl.program_id(1)

    @pl.when(t == 0)
    def _():
        carry_c[...] = jnp.zeros_like(carry_c)
        carry_u[...] = umeta_ref[...]

    hn = _layer_norm(x_ref[0], g_ref[...], b_ref[...])
    hb = hn.astype(jnp.bfloat16)
    n_sub = TOKEN_TILE // ATT_TILE

    qT = lax.dot_general(wqT_ref[...], hb, _NT, preferred_element_type=jnp.float32) * Q_SCALE
    qTb = qT.astype(jnp.bfloat16)
    for j in range(n_sub):
        qT_ref[0, j] = qTb[:, j * ATT_TILE:(j + 1) * ATT_TILE]

    vT = lax.dot_general(wvT_ref[...], hb, _NT, preferred_element_type=jnp.float32)
    vTb = vT.astype(jnp.bfloat16)
    for j in range(n_sub):
        vT_ref[0, j] = vTb[:, j * ATT_TILE:(j + 1) * ATT_TILE]

    gT = lax.dot_general(wgT_ref[...], hb, _NT, preferred_element_type=jnp.float32)
    sgTb = _silu(gT).astype(jnp.bfloat16)
    for j in range(n_sub):
        sgT_ref[0, j] = sgTb[:, j * ATT_TILE:(j + 1) * ATT_TILE]

    flT = lax.dot_general(wfT_ref[...], hb, _NT, preferred_element_type=jnp.float32)
    logf = _log_sigmoid(flT[:HEADS, :] + bf_ref[:HEADS, :])
    cT = _lane_cumsum(logf) + carry_c[:, 0:1]
    carry_c[...] = jnp.broadcast_to(cT[:, TOKEN_TILE - 1:TOKEN_TILE], carry_c.shape)
    cT_pad = jnp.concatenate(
        [cT * LOG2E, jnp.zeros((LANES - HEADS, TOKEN_TILE), jnp.float32)], axis=0)
    c_rows = cT_pad.T
    aug = jnp.dot(_bias_columns(c_rows), e_ref[...], preferred_element_type=jnp.float32)
    kk = jnp.dot(hb, wk_ref[...], preferred_element_type=jnp.float32)
    k_ref[0] = (kk + aug).astype(jnp.bfloat16)

    u = jnp.dot(hb, wu_ref[...], preferred_element_type=jnp.float32)
    gp = jnp.dot(hb, wgp_ref[...], preferred_element_type=jnp.float32)
    u_ext = jnp.concatenate([carry_u[...], u], axis=0)
    carry_u[...] = u[TOKEN_TILE - N_META:, :]
    p = _pool_mix(u_ext, u, wpool_ref, pscale_ref[...], _silu(gp))
    p_ref[0] = p.astype(jnp.bfloat16)


def _attn_kernel(qT_ref, k_ref, vT_ref, sgT_ref, kmeta_ref, vTmeta_ref, o_ref):
    n_qb = qT_ref.shape[1]
    row3 = lax.broadcasted_iota(jnp.int32, (KEY_WIDTH - HEAD_DIM, ATT_TILE), 0)
    q_aug = jnp.where(row3 < 3, -1.0, 0.0).astype(jnp.bfloat16)
    ones = jnp.ones((ONES_ROWS, ATT_TILE), jnp.bfloat16)
    krow = lax.broadcasted_iota(jnp.int32, (ATT_TILE, ATT_TILE), 0)
    qcol = lax.broadcasted_iota(jnp.int32, (ATT_TILE, ATT_TILE), 1)
    causal = krow <= qcol
    vmeta = jnp.concatenate([vTmeta_ref[...], jnp.ones((ONES_ROWS, META_PAD), jnp.bfloat16)], axis=0)

    def q_body(qi, _):
        qa = jnp.concatenate([qT_ref[0, qi], q_aug], axis=0)
        s = jnp.dot(kmeta_ref[...], qa, preferred_element_type=jnp.float32)
        m = jnp.max(s, axis=0, keepdims=True)
        p = jnp.exp2(s - m).astype(jnp.bfloat16)
        acc = jnp.dot(vmeta, p, preferred_element_type=jnp.float32)

        def kv_step(kj, m, acc, masked):
            start = pl.multiple_of(kj * ATT_TILE, ATT_TILE)
            kblk = k_ref[0, pl.ds(start, ATT_TILE), :]
            s = jnp.dot(kblk, qa, preferred_element_type=jnp.float32)
            if masked:
                s = jnp.where(causal, s, -jnp.inf)
            m_new = jnp.maximum(m, jnp.max(s, axis=0, keepdims=True))
            alpha = jnp.exp2(m - m_new)
            p = jnp.exp2(s - m_new).astype(jnp.bfloat16)
            vext = jnp.concatenate([vT_ref[0, kj], ones], axis=0)
            acc = alpha * acc + jnp.dot(vext, p, preferred_element_type=jnp.float32)
            return m_new, acc

        m, acc = lax.fori_loop(0, qi, lambda kj, c: kv_step(kj, c[0], c[1], False), (m, acc))
        m, acc = kv_step(qi, m, acc, True)
        o = acc[:HEAD_DIM, :] / acc[HEAD_DIM:HEAD_DIM + 1, :]
        o = o * sgT_ref[0, qi].astype(jnp.float32)
        o_ref[0, qi] = o.astype(jnp.bfloat16)
        return 0

    lax.fori_loop(0, n_qb, q_body, 0)


def _out_proj_kernel(x_ref, gin_ref, bin_ref, aT_ref, p_ref, waT_ref, wp_ref, g_ref, b_ref, o_ref):
    n_sub = TOKEN_TILE // ATT_TILE
    for j in range(n_sub):
        rows = slice(j * ATT_TILE, (j + 1) * ATT_TILE)
        hn = _layer_norm(x_ref[0, rows, :], gin_ref[...], bin_ref[...])
        yT = jnp.dot(waT_ref[...], aT_ref[0, j], preferred_element_type=jnp.float32)
        y = yT.T + jnp.dot(p_ref[0, rows, :], wp_ref[...], preferred_element_type=jnp.float32)
        o_ref[0, rows, :] = _layer_norm(DEEPNORM_ALPHA * hn + y, g_ref[...], b_ref[...])


def _bias_placement_matrix():
    e = np.zeros((LANES, HEADS * KEY_WIDTH), np.float32)
    for j in range(3):
        for h in range(HEADS):
            e[HEADS * j + h, KEY_WIDTH * h + HEAD_DIM + j] = 1.0
    return e


def _const_spec(shape):
    nd = len(shape)
    return pl.BlockSpec(shape, lambda *_: (0,) * nd, pipeline_mode=pl.Buffered(1))


def kernel(x, meta_tokens, ln_in_g, ln_in_b, w_in, b_forget, w_pool, pool_scale, w_out, ln_g, ln_b):
    B, L, D = x.shape
    assert D == D_MODEL and L % TOKEN_TILE == 0 and w_in.shape[0] == 1
    bf16, f32 = jnp.bfloat16, jnp.float32
    w = w_in[0]
    o_k, o_v, o_f = ATT_WIDTH, 2 * ATT_WIDTH, 3 * ATT_WIDTH
    o_g = o_f + HEADS
    o_u = o_g + ATT_WIDTH
    o_gp = o_u + D_MODEL

    wqT = w[:, :o_k].T.astype(bf16)
    wk = jnp.pad(w[:, o_k:o_v].reshape(D, HEADS, HEAD_DIM),
                 ((0, 0), (0, 0), (0, KEY_WIDTH - HEAD_DIM))).reshape(D, HEADS * KEY_WIDTH).astype(bf16)
    wvT = w[:, o_v:o_f].T.astype(bf16)
    wfT = jnp.pad(w[:, o_f:o_g].T, ((0, LANES - HEADS), (0, 0))).astype(bf16)
    bf_col = jnp.pad(b_forget[0].astype(f32), (0, LANES - HEADS)).reshape(LANES, 1)
    wgT = w[:, o_g:o_u].T.astype(bf16)
    wu = w[:, o_u:o_gp].astype(bf16)
    wgp = w[:, o_gp:].astype(bf16)
    wpool = w_pool[0].astype(bf16)
    pscale = pool_scale[0].reshape(1, D).astype(f32)
    waT = w_out[0, :ATT_WIDTH].T.astype(bf16)
    wp = w_out[0, ATT_WIDTH:].astype(bf16)
    e_mat = jnp.asarray(_bias_placement_matrix(), dtype=bf16)
    g_in = ln_in_g.reshape(1, D).astype(f32)
    b_in = ln_in_b.reshape(1, D).astype(f32)
    g_out = ln_g[0].reshape(1, D).astype(f32)
    b_out = ln_b[0].reshape(1, D).astype(f32)
    mt_pad = jnp.pad(meta_tokens.astype(f32), ((0, META_PAD - N_META), (0, 0)))

    kw = HEADS * KEY_WIDTH
    kmeta, vTmeta, umeta = pl.pallas_call(
        _meta_kernel,
        out_shape=(jax.ShapeDtypeStruct((META_PAD, kw), bf16),
                   jax.ShapeDtypeStruct((ATT_WIDTH, META_PAD), bf16),
                   jax.ShapeDtypeStruct((N_META, D), f32)),
        compiler_params=pltpu.CompilerParams(vmem_limit_bytes=VMEM_LIMIT),
        name="meta_proj",
    )(mt_pad, g_in, b_in, wk, wvT, wfT, bf_col, wu, e_mat)

    n_t = L // TOKEN_TILE
    n_sub = TOKEN_TILE // ATT_TILE
    n_ab = L // ATT_TILE
    tiled = jax.ShapeDtypeStruct((B, n_ab, ATT_WIDTH, ATT_TILE), bf16)
    tiled_spec = pl.BlockSpec((1, n_sub, ATT_WIDTH, ATT_TILE), lambda b, t: (b, t, 0, 0))
    qT, kaug, vT, sgT, pmix = pl.pallas_call(
        _in_proj_kernel,
        grid=(B, n_t),
        in_specs=[
            pl.BlockSpec((1, TOKEN_TILE, D), lambda b, t: (b, t, 0)),
            _const_spec((1, D)), _const_spec((1, D)),
            _const_spec((ATT_WIDTH, D)), _const_spec((D, kw)), _const_spec((ATT_WIDTH, D)),
            _const_spec((LANES, D)), _const_spec((LANES, 1)), _const_spec((ATT_WIDTH, D)),
            _const_spec((D, D)), _const_spec((D, D)),
            _const_spec((len(POOL_WINDOWS), POOL_GROUP_WIDTH, POOL_GROUP_WIDTH)),
            _const_spec((1, D)), _const_spec((LANES, kw)), _const_spec((N_META, D)),
        ],
        out_specs=(tiled_spec,
                   pl.BlockSpec((1, TOKEN_TILE, kw), lambda b, t: (b, t, 0)),
                   tiled_spec, tiled_spec,
                   pl.BlockSpec((1, TOKEN_TILE, D), lambda b, t: (b, t, 0))),
        out_shape=(tiled, jax.ShapeDtypeStruct((B, L, kw), bf16), tiled, tiled,
                   jax.ShapeDtypeStruct((B, L, D), bf16)),
        scratch_shapes=[pltpu.VMEM((HEADS, LANES), f32), pltpu.VMEM((N_META, D), f32)],
        compiler_params=pltpu.CompilerParams(
            dimension_semantics=("arbitrary", "arbitrary"), vmem_limit_bytes=VMEM_LIMIT),
        name="in_proj",
    )(x, g_in, b_in, wqT, wk, wvT, wfT, bf_col, wgT, wu, wgp, wpool, pscale, e_mat, umeta)

    head_tiled = pl.BlockSpec((1, n_ab, HEAD_DIM, ATT_TILE), lambda b, h: (b, 0, h, 0))
    aT = pl.pallas_call(
        _attn_kernel,
        grid=(B, HEADS),
        in_specs=[
            head_tiled,
            pl.BlockSpec((1, L, KEY_WIDTH), lambda b, h: (b, 0, h)),
            head_tiled, head_tiled,
            pl.BlockSpec((META_PAD, KEY_WIDTH), lambda b, h: (0, h)),
            pl.BlockSpec((HEAD_DIM, META_PAD), lambda b, h: (h, 0)),
        ],
        out_specs=head_tiled,
        out_shape=tiled,
        compiler_params=pltpu.CompilerParams(
            dimension_semantics=("arbitrary", "arbitrary"), vmem_limit_bytes=VMEM_LIMIT),
        name="fox_attn",
    )(qT, kaug, vT, sgT, kmeta, vTmeta)

    out = pl.pallas_call(
        _out_proj_kernel,
        grid=(B, n_t),
        in_specs=[
            pl.BlockSpec((1, TOKEN_TILE, D), lambda b, t: (b, t, 0)),
            _const_spec((1, D)), _const_spec((1, D)),
            tiled_spec,
            pl.BlockSpec((1, TOKEN_TILE, D), lambda b, t: (b, t, 0)),
            _const_spec((D, ATT_WIDTH)), _const_spec((D, D)),
            _const_spec((1, D)), _const_spec((1, D)),
        ],
        out_specs=pl.BlockSpec((1, TOKEN_TILE, D), lambda b, t: (b, t, 0)),
        out_shape=jax.ShapeDtypeStruct((B, L, D), x.dtype),
        compiler_params=pltpu.CompilerParams(
            dimension_semantics=("arbitrary", "arbitrary"), vmem_limit_bytes=VMEM_LIMIT),
        name="out_proj",
    )(x, g_in, b_in, aT, pmix, waT, wp, g_out, b_out)
    return out
```

```python
import math

import numpy as np
import jax
import jax.numpy as jnp
from jax import lax
from jax.experimental import pallas as pl
from jax.experimental.pallas import tpu as pltpu

D_MODEL = 1024
N_META = 16
HEADS = 16
HEAD_DIM = 64
ATT_WIDTH = HEADS * HEAD_DIM
POOL_WINDOWS = (2, 4, 8, 16)
POOL_GROUP_WIDTH = D_MODEL // len(POOL_WINDOWS)
LN_EPS = 1e-5
DEEPNORM_ALPHA = 2.0 ** 0.25
LOG2E = math.log2(math.e)
Q_SCALE = HEAD_DIM ** -0.5 * LOG2E

LANES = 128
KEY_WIDTH = 128
META_PAD = 128
MASKED_BIAS = 1e30
TOKEN_TILE = 512
ATT_TILE = 256
ONES_ROWS = 16
HEAD_GROUP = 4
VMEM_LIMIT = 56 * 1024 * 1024

_NT = (((1,), (1,)), ((), ()))


def _layer_norm(x, g, b):
    mu = jnp.mean(x, axis=-1, keepdims=True)
    xc = x - mu
    var = jnp.mean(xc * xc, axis=-1, keepdims=True)
    return xc * lax.rsqrt(var + LN_EPS) * g + b


def _log_sigmoid(z):
    return jnp.minimum(z, 0.0) - jnp.log(1.0 + jnp.exp(-jnp.abs(z)))


def _silu(z):
    return z / (1.0 + jnp.exp(-z))


def _lane_cumsum(x):
    n = x.shape[-1]
    lane = lax.broadcasted_iota(jnp.int32, x.shape, x.ndim - 1)
    d = 1
    while d < n:
        x = x + jnp.where(lane >= d, pltpu.roll(x, d, x.ndim - 1), 0.0)
        d *= 2
    return x


def _bias_columns(c_rows):
    hi = c_rows.astype(jnp.bfloat16).astype(jnp.float32)
    r1 = c_rows - hi
    mid = r1.astype(jnp.bfloat16).astype(jnp.float32)
    lo = (r1 - mid).astype(jnp.bfloat16).astype(jnp.float32)
    packed = hi + pltpu.roll(mid, HEADS, 1) + pltpu.roll(lo, 2 * HEADS, 1)
    return packed.astype(jnp.bfloat16)


def _pool_mix(u_ext, u, wpool_ref, pscale, gate):
    outs = []
    for gi, w in enumerate(POOL_WINDOWS):
        sl = slice(gi * POOL_GROUP_WIDTH, (gi + 1) * POOL_GROUP_WIDTH)
        r = u_ext[:, sl]
        s = 1
        while s < w:
            r = r + pltpu.roll(r, s, 0)
            s *= 2
        d = r[N_META:, :] * (1.0 / w) - u[:, sl]
        outs.append(jnp.dot(d.astype(jnp.bfloat16), wpool_ref[gi],
                            preferred_element_type=jnp.float32))
    y = jnp.concatenate(outs, axis=-1) * pscale
    return y * gate


def _meta_kernel(mt_ref, g_ref, b_ref, wk_ref, wvT_ref, wfT_ref, bf_ref, wu_ref, e_ref,
                 kmeta_ref, vTmeta_ref, umeta_ref):
    hn = _layer_norm(mt_ref[...], g_ref[...], b_ref[...])
    hb = hn.astype(jnp.bfloat16)
    flT = lax.dot_general(wfT_ref[...], hb, _NT, preferred_element_type=jnp.float32)
    row = lax.broadcasted_iota(jnp.int32, flT.shape, 0)
    col = lax.broadcasted_iota(jnp.int32, flT.shape, 1)
    valid = (row < HEADS) & (col < N_META)
    logf = jnp.where(valid, _log_sigmoid(flT + bf_ref[...]), 0.0)
    cT = _lane_cumsum(logf)
    total = cT[:, META_PAD - 1:META_PAD]
    c_rows = ((cT - total) * LOG2E).T
    trow = lax.broadcasted_iota(jnp.int32, c_rows.shape, 0)
    tcol = lax.broadcasted_iota(jnp.int32, c_rows.shape, 1)
    c_rows = jnp.where((trow >= N_META) & (tcol < HEADS), MASKED_BIAS, c_rows)
    aug = jnp.dot(_bias_columns(c_rows), e_ref[...], preferred_element_type=jnp.float32)
    kk = jnp.dot(hb, wk_ref[...], preferred_element_type=jnp.float32)
    kmeta_ref[...] = (kk + aug).astype(jnp.bfloat16)
    vT = lax.dot_general(wvT_ref[...], hb, _NT, preferred_element_type=jnp.float32)
    vTmeta_ref[...] = vT.astype(jnp.bfloat16)
    u = jnp.dot(hb, wu_ref[...], preferred_element_type=jnp.float32)
    umeta_ref[...] = u[:N_META, :]


def _in_proj_kernel(x_ref, g_ref, b_ref, wqT_ref, wk_ref, wvT_ref, wfT_ref, bf_ref, wgT_ref,
                    wu_ref, wgp_ref, wpool_ref, pscale_ref, e_ref, umeta_ref,
                    qT_ref, k_ref, vT_ref, sgT_ref, p_ref, carry_c, carry_u):
    t = pl.program_id(1)

    @pl.when(t == 0)
    def _():
        carry_c[...] = jnp.zeros_like(carry_c)
        carry_u[...] = umeta_ref[...]

    hn = _layer_norm(x_ref[0], g_ref[...], b_ref[...])
    hb = hn.astype(jnp.bfloat16)
    n_sub = TOKEN_TILE // ATT_TILE

    qT = lax.dot_general(wqT_ref[...], hb, _NT, preferred_element_type=jnp.float32) * Q_SCALE
    qTb = qT.astype(jnp.bfloat16)
    for j in range(n_sub):
        qT_ref[0, j] = qTb[:, j * ATT_TILE:(j + 1) * ATT_TILE]

    vT = lax.dot_general(wvT_ref[...], hb, _NT, preferred_element_type=jnp.float32)
    vTb = vT.astype(jnp.bfloat16)
    for j in range(n_sub):
        vT_ref[0, j] = vTb[:, j * ATT_TILE:(j + 1) * ATT_TILE]

    gT = lax.dot_general(wgT_ref[...], hb, _NT, preferred_element_type=jnp.float32)
    sgTb = _silu(gT).astype(jnp.bfloat16)
    for j in range(n_sub):
        sgT_ref[0, j] = sgTb[:, j * ATT_TILE:(j + 1) * ATT_TILE]

    flT = lax.dot_general(wfT_ref[...], hb, _NT, preferred_element_type=jnp.float32)
    logf = _log_sigmoid(flT[:HEADS, :] + bf_ref[:HEADS, :])
    cT = _lane_cumsum(logf) + carry_c[:, 0:1]
    carry_c[...] = jnp.broadcast_to(cT[:, TOKEN_TILE - 1:TOKEN_TILE], carry_c.shape)
    cT_pad = jnp.concatenate(
        [cT * LOG2E, jnp.zeros((LANES - HEADS, TOKEN_TILE), jnp.float32)], axis=0)
    c_rows = cT_pad.T
    aug = jnp.dot(_bias_columns(c_rows), e_ref[...], preferred_element_type=jnp.float32)
    kk = jnp.dot(hb, wk_ref[...], preferred_element_type=jnp.float32)
    k_ref[0] = (kk + aug).astype(jnp.bfloat16)

    u = jnp.dot(hb, wu_ref[...], preferred_element_type=jnp.float32)
    gp = jnp.dot(hb, wgp_ref[...], preferred_element_type=jnp.float32)
    u_ext = jnp.concatenate([carry_u[...], u], axis=0)
    carry_u[...] = u[TOKEN_TILE - N_META:, :]
    p = _pool_mix(u_ext, u, wpool_ref, pscale_ref[...], _silu(gp))
    p_ref[0] = p.astype(jnp.bfloat16)


def _attn_kernel(qT_ref, k_ref, vT_ref, sgT_ref, kmeta_ref, vTmeta_ref, o_ref,
                 qa_ref, smeta_ref, acc_ref):
    n_qb = qT_ref.shape[1]
    heads = range(HEAD_GROUP)
    row3 = lax.broadcasted_iota(jnp.int32, (KEY_WIDTH - HEAD_DIM, ATT_TILE), 0)
    q_aug = jnp.where(row3 < 3, -1.0, 0.0).astype(jnp.bfloat16)
    for g in heads:
        qa_ref[g, HEAD_DIM:, :] = q_aug
    ones = jnp.ones((ONES_ROWS, ATT_TILE), jnp.bfloat16)
    ones_meta = jnp.ones((ONES_ROWS, META_PAD), jnp.bfloat16)
    krow = lax.broadcasted_iota(jnp.int32, (ATT_TILE, ATT_TILE), 0)
    qcol = lax.broadcasted_iota(jnp.int32, (ATT_TILE, ATT_TILE), 1)
    causal = krow <= qcol

    def hrows(g, width):
        return slice(g * width, (g + 1) * width)

    def colmax(s):
        return jnp.max(s, axis=0, keepdims=True)

    def scores(kj):
        start = pl.multiple_of(kj * ATT_TILE, ATT_TILE)
        tiles, out = [], []
        for g in heads:
            kblk = k_ref[0, pl.ds(start, ATT_TILE), hrows(g, KEY_WIDTH)]
            s = jnp.dot(kblk, qa_ref[g], preferred_element_type=jnp.float32)
            tiles.append(s)
            out.append(colmax(s))
        return tuple(tiles), tuple(out)

    def q_body(qi, _):
        for g in heads:
            qa_ref[g, :HEAD_DIM, :] = qT_ref[0, qi, hrows(g, HEAD_DIM), :]
            acc_ref[g] = jnp.zeros(acc_ref.shape[1:], jnp.float32)
        for g in heads:
            smeta_ref[g] = jnp.dot(kmeta_ref[:, hrows(g, KEY_WIDTH)], qa_ref[g],
                                   preferred_element_type=jnp.float32)
        tiles0, smax0 = scores(0)
        m0 = tuple(jnp.full((1, ATT_TILE), -jnp.inf, jnp.float32) for _ in heads)

        def trip(kj, carry):
            ms, smax, tiles = carry
            tiles_next, smax_next = scores(kj + 1)
            ms_new = []
            for g in heads:
                m_new = jnp.maximum(ms[g], smax[g])
                alpha = jnp.exp2(ms[g] - m_new)
                p = jnp.exp2(tiles[g] - m_new).astype(jnp.bfloat16)
                vext = jnp.concatenate([vT_ref[0, kj, hrows(g, HEAD_DIM), :], ones], axis=0)
                acc_ref[g] = alpha * acc_ref[g] + jnp.dot(vext, p, preferred_element_type=jnp.float32)
                ms_new.append(m_new)
            return tuple(ms_new), smax_next, tiles_next

        ms, _, tiles = lax.fori_loop(0, qi, trip, (m0, smax0, tiles0))

        for g in heads:
            s = jnp.where(causal, tiles[g], -jnp.inf)
            sm = smeta_ref[g]
            m_new = jnp.maximum(ms[g], jnp.maximum(colmax(s), colmax(sm)))
            alpha = jnp.exp2(ms[g] - m_new)
            p = jnp.exp2(s - m_new).astype(jnp.bfloat16)
            pm = jnp.exp2(sm - m_new).astype(jnp.bfloat16)
            vext = jnp.concatenate([vT_ref[0, qi, hrows(g, HEAD_DIM), :], ones], axis=0)
            vmeta = jnp.concatenate([vTmeta_ref[hrows(g, HEAD_DIM), :], ones_meta], axis=0)
            acc = (alpha * acc_ref[g]
                   + jnp.dot(vext, p, preferred_element_type=jnp.float32)
                   + jnp.dot(vmeta, pm, preferred_element_type=jnp.float32))
            o = acc[:HEAD_DIM, :] / acc[HEAD_DIM:HEAD_DIM + 1, :]
            o = o * sgT_ref[0, qi, hrows(g, HEAD_DIM), :].astype(jnp.float32)
            o_ref[0, qi, hrows(g, HEAD_DIM), :] = o.astype(jnp.bfloat16)
        return 0

    lax.fori_loop(0, n_qb, q_body, 0)


def _out_proj_kernel(x_ref, gin_ref, bin_ref, aT_ref, p_ref, waT_ref, wp_ref, g_ref, b_ref, o_ref):
    n_sub = TOKEN_TILE // ATT_TILE
    for j in range(n_sub):
        rows = slice(j * ATT_TILE, (j + 1) * ATT_TILE)
        hn = _layer_norm(x_ref[0, rows, :], gin_ref[...], bin_ref[...])
        yT = jnp.dot(waT_ref[...], aT_ref[0, j], preferred_element_type=jnp.float32)
        y = yT.T + jnp.dot(p_ref[0, rows, :], wp_ref[...], preferred_element_type=jnp.float32)
        o_ref[0, rows, :] = _layer_norm(DEEPNORM_ALPHA * hn + y, g_ref[...], b_ref[...])


def _bias_placement_matrix():
    e = np.zeros((LANES, HEADS * KEY_WIDTH), np.float32)
    for j in range(3):
        for h in range(HEADS):
            e[HEADS * j + h, KEY_WIDTH * h + HEAD_DIM + j] = 1.0
    return e


def _const_spec(shape):
    nd = len(shape)
    return pl.BlockSpec(shape, lambda *_: (0,) * nd, pipeline_mode=pl.Buffered(1))


def kernel(x, meta_tokens, ln_in_g, ln_in_b, w_in, b_forget, w_pool, pool_scale, w_out, ln_g, ln_b):
    B, L, D = x.shape
    assert D == D_MODEL and L % TOKEN_TILE == 0 and w_in.shape[0] == 1
    bf16, f32 = jnp.bfloat16, jnp.float32
    w = w_in[0]
    o_k, o_v, o_f = ATT_WIDTH, 2 * ATT_WIDTH, 3 * ATT_WIDTH
    o_g = o_f + HEADS
    o_u = o_g + ATT_WIDTH
    o_gp = o_u + D_MODEL

    wqT = w[:, :o_k].T.astype(bf16)
    wk = jnp.pad(w[:, o_k:o_v].reshape(D, HEADS, HEAD_DIM),
                 ((0, 0), (0, 0), (0, KEY_WIDTH - HEAD_DIM))).reshape(D, HEADS * KEY_WIDTH).astype(bf16)
    wvT = w[:, o_v:o_f].T.astype(bf16)
    wfT = jnp.pad(w[:, o_f:o_g].T, ((0, LANES - HEADS), (0, 0))).astype(bf16)
    bf_col = jnp.pad(b_forget[0].astype(f32), (0, LANES - HEADS)).reshape(LANES, 1)
    wgT = w[:, o_g:o_u].T.astype(bf16)
    wu = w[:, o_u:o_gp].astype(bf16)
    wgp = w[:, o_gp:].astype(bf16)
    wpool = w_pool[0].astype(bf16)
    pscale = pool_scale[0].reshape(1, D).astype(f32)
    waT = w_out[0, :ATT_WIDTH].T.astype(bf16)
    wp = w_out[0, ATT_WIDTH:].astype(bf16)
    e_mat = jnp.asarray(_bias_placement_matrix(), dtype=bf16)
    g_in = ln_in_g.reshape(1, D).astype(f32)
    b_in = ln_in_b.reshape(1, D).astype(f32)
    g_out = ln_g[0].reshape(1, D).astype(f32)
    b_out = ln_b[0].reshape(1, D).astype(f32)
    mt_pad = jnp.pad(meta_tokens.astype(f32), ((0, META_PAD - N_META), (0, 0)))

    kw = HEADS * KEY_WIDTH
    kmeta, vTmeta, umeta = pl.pallas_call(
        _meta_kernel,
        out_shape=(jax.ShapeDtypeStruct((META_PAD, kw), bf16),
                   jax.ShapeDtypeStruct((ATT_WIDTH, META_PAD), bf16),
                   jax.ShapeDtypeStruct((N_META, D), f32)),
        compiler_params=pltpu.CompilerParams(vmem_limit_bytes=VMEM_LIMIT),
        name="meta_proj",
    )(mt_pad, g_in, b_in, wk, wvT, wfT, bf_col, wu, e_mat)

    n_t = L // TOKEN_TILE
    n_sub = TOKEN_TILE // ATT_TILE
    n_ab = L // ATT_TILE
    tiled = jax.ShapeDtypeStruct((B, n_ab, ATT_WIDTH, ATT_TILE), bf16)
    tiled_spec = pl.BlockSpec((1, n_sub, ATT_WIDTH, ATT_TILE), lambda b, t: (b, t, 0, 0))
    qT, kaug, vT, sgT, pmix = pl.pallas_call(
        _in_proj_kernel,
        grid=(B, n_t),
        in_specs=[
            pl.BlockSpec((1, TOKEN_TILE, D), lambda b, t: (b, t, 0)),
            _const_spec((1, D)), _const_spec((1, D)),
            _const_spec((ATT_WIDTH, D)), _const_spec((D, kw)), _const_spec((ATT_WIDTH, D)),
            _const_spec((LANES, D)), _const_spec((LANES, 1)), _const_spec((ATT_WIDTH, D)),
            _const_spec((D, D)), _const_spec((D, D)),
            _const_spec((len(POOL_WINDOWS), POOL_GROUP_WIDTH, POOL_GROUP_WIDTH)),
            _const_spec((1, D)), _const_spec((LANES, kw)), _const_spec((N_META, D)),
        ],
        out_specs=(tiled_spec,
                   pl.BlockSpec((1, TOKEN_TILE, kw), lambda b, t: (b, t, 0)),
                   tiled_spec, tiled_spec,
                   pl.BlockSpec((1, TOKEN_TILE, D), lambda b, t: (b, t, 0))),
        out_shape=(tiled, jax.ShapeDtypeStruct((B, L, kw), bf16), tiled, tiled,
                   jax.ShapeDtypeStruct((B, L, D), bf16)),
        scratch_shapes=[pltpu.VMEM((HEADS, LANES), f32), pltpu.VMEM((N_META, D), f32)],
        compiler_params=pltpu.CompilerParams(
            dimension_semantics=("arbitrary", "arbitrary"), vmem_limit_bytes=VMEM_LIMIT),
        name="in_proj",
    )(x, g_in, b_in, wqT, wk, wvT, wfT, bf_col, wgT, wu, wgp, wpool, pscale, e_mat, umeta)

    gd, gk = HEAD_GROUP * HEAD_DIM, HEAD_GROUP * KEY_WIDTH
    head_tiled = pl.BlockSpec((1, n_ab, gd, ATT_TILE), lambda b, h: (b, 0, h, 0))
    aT = pl.pallas_call(
        _attn_kernel,
        grid=(B, HEADS // HEAD_GROUP),
        in_specs=[
            head_tiled,
            pl.BlockSpec((1, L, gk), lambda b, h: (b, 0, h)),
            head_tiled, head_tiled,
            pl.BlockSpec((META_PAD, gk), lambda b, h: (0, h)),
            pl.BlockSpec((gd, META_PAD), lambda b, h: (h, 0)),
        ],
        out_specs=head_tiled,
        out_shape=tiled,
        scratch_shapes=[pltpu.VMEM((HEAD_GROUP, KEY_WIDTH, ATT_TILE), bf16),
                        pltpu.VMEM((HEAD_GROUP, META_PAD, ATT_TILE), f32),
                        pltpu.VMEM((HEAD_GROUP, HEAD_DIM + ONES_ROWS, ATT_TILE), f32)],
        compiler_params=pltpu.CompilerParams(
            dimension_semantics=("arbitrary", "arbitrary"), vmem_limit_bytes=VMEM_LIMIT),
        name="fox_attn",
    )(qT, kaug, vT, sgT, kmeta, vTmeta)

    out = pl.pallas_call(
        _out_proj_kernel,
        grid=(B, n_t),
        in_specs=[
            pl.BlockSpec((1, TOKEN_TILE, D), lambda b, t: (b, t, 0)),
            _const_spec((1, D)), _const_spec((1, D)),
            tiled_spec,
            pl.BlockSpec((1, TOKEN_TILE, D), lambda b, t: (b, t, 0)),
            _const_spec((D, ATT_WIDTH)), _const_spec((D, D)),
            _const_spec((1, D)), _const_spec((1, D)),
        ],
        out_specs=pl.BlockSpec((1, TOKEN_TILE, D), lambda b, t: (b, t, 0)),
        out_shape=jax.ShapeDtypeStruct((B, L, D), x.dtype),
        compiler_params=pltpu.CompilerParams(
            dimension_semantics=("arbitrary", "arbitrary"), vmem_limit_bytes=VMEM_LIMIT),
        name="out_proj",
    )(x, g_in, b_in, aT, pmix, waT, wp, g_out, b_out)
    return out
```

```python
import math

import numpy as np
import jax
import jax.numpy as jnp
from jax import lax
from jax.experimental import pallas as pl
from jax.experimental.pallas import tpu as pltpu

D_MODEL = 1024
N_META = 16
HEADS = 16
HEAD_DIM = 64
ATT_WIDTH = HEADS * HEAD_DIM
POOL_WINDOWS = (2, 4, 8, 16)
POOL_GROUP_WIDTH = D_MODEL // len(POOL_WINDOWS)
LN_EPS = 1e-5
DEEPNORM_ALPHA = 2.0 ** 0.25
LOG2E = math.log2(math.e)
Q_SCALE = HEAD_DIM ** -0.5 * LOG2E

LANES = 128
KEY_WIDTH = 128
META_PAD = 128
MASKED_BIAS = 1e30
TOKEN_TILE = 512
ATT_TILE = 256
ONES_ROWS = 16
HEAD_GROUP = 4
VMEM_LIMIT = 56 * 1024 * 1024

_NT = (((1,), (1,)), ((), ()))


def _layer_norm(x, g, b):
    mu = jnp.mean(x, axis=-1, keepdims=True)
    xc = x - mu
    var = jnp.mean(xc * xc, axis=-1, keepdims=True)
    return xc * lax.rsqrt(var + LN_EPS) * g + b


def _log_sigmoid(z):
    return jnp.minimum(z, 0.0) - jnp.log(1.0 + jnp.exp(-jnp.abs(z)))


def _silu(z):
    return z / (1.0 + jnp.exp(-z))


def _lane_cumsum(x):
    n = x.shape[-1]
    lane = lax.broadcasted_iota(jnp.int32, x.shape, x.ndim - 1)
    d = 1
    while d < n:
        x = x + jnp.where(lane >= d, pltpu.roll(x, d, x.ndim - 1), 0.0)
        d *= 2
    return x


def _bias_columns(c_rows):
    hi = c_rows.astype(jnp.bfloat16).astype(jnp.float32)
    r1 = c_rows - hi
    mid = r1.astype(jnp.bfloat16).astype(jnp.float32)
    lo = (r1 - mid).astype(jnp.bfloat16).astype(jnp.float32)
    packed = hi + pltpu.roll(mid, HEADS, 1) + pltpu.roll(lo, 2 * HEADS, 1)
    return packed.astype(jnp.bfloat16)


def _pool_mix(u_ext, u, wpool_ref, pscale, gate):
    outs = []
    for gi, w in enumerate(POOL_WINDOWS):
        sl = slice(gi * POOL_GROUP_WIDTH, (gi + 1) * POOL_GROUP_WIDTH)
        r = u_ext[:, sl]
        s = 1
        while s < w:
            r = r + pltpu.roll(r, s, 0)
            s *= 2
        d = r[N_META:, :] * (1.0 / w) - u[:, sl]
        outs.append(jnp.dot(d.astype(jnp.bfloat16), wpool_ref[gi],
                            preferred_element_type=jnp.float32))
    y = jnp.concatenate(outs, axis=-1) * pscale
    return y * gate


def _meta_kernel(mt_ref, g_ref, b_ref, wk_ref, wvT_ref, wfT_ref, bf_ref, wu_ref, e_ref,
                 kmeta_ref, vTmeta_ref, umeta_ref):
    hn = _layer_norm(mt_ref[...], g_ref[...], b_ref[...])
    hb = hn.astype(jnp.bfloat16)
    flT = lax.dot_general(wfT_ref[...], hb, _NT, preferred_element_type=jnp.float32)
    row = lax.broadcasted_iota(jnp.int32, flT.shape, 0)
    col = lax.broadcasted_iota(jnp.int32, flT.shape, 1)
    valid = (row < HEADS) & (col < N_META)
    logf = jnp.where(valid, _log_sigmoid(flT + bf_ref[...]), 0.0)
    cT = _lane_cumsum(logf)
    total = cT[:, META_PAD - 1:META_PAD]
    c_rows = ((cT - total) * LOG2E).T
    trow = lax.broadcasted_iota(jnp.int32, c_rows.shape, 0)
    tcol = lax.broadcasted_iota(jnp.int32, c_rows.shape, 1)
    c_rows = jnp.where((trow >= N_META) & (tcol < HEADS), MASKED_BIAS, c_rows)
    aug = jnp.dot(_bias_columns(c_rows), e_ref[...], preferred_element_type=jnp.float32)
    kk = jnp.dot(hb, wk_ref[...], preferred_element_type=jnp.float32)
    kmeta_ref[...] = (kk + aug).astype(jnp.bfloat16)
    vT = lax.dot_general(wvT_ref[...], hb, _NT, preferred_element_type=jnp.float32)
    vTmeta_ref[...] = vT.astype(jnp.bfloat16)
    u = jnp.dot(hb, wu_ref[...], preferred_element_type=jnp.float32)
    umeta_ref[...] = u[:N_META, :]


def _in_proj_kernel(x_ref, g_ref, b_ref, wqT_ref, wk_ref, wvT_ref, wfT_ref, bf_ref, wgT_ref,
                    wu_ref, wgp_ref, wpool_ref, pscale_ref, e_ref, umeta_ref,
                    qT_ref, k_ref, vT_ref, sgT_ref, p_ref, carry_c, carry_u):
    t = pl.program_id(1)

    @pl.when(t == 0)
    def _():
        carry_c[...] = jnp.zeros_like(carry_c)
        carry_u[...] = umeta_ref[...]

    hn = _layer_norm(x_ref[0], g_ref[...], b_ref[...])
    hb = hn.astype(jnp.bfloat16)
    n_sub = TOKEN_TILE // ATT_TILE

    qT = lax.dot_general(wqT_ref[...], hb, _NT, preferred_element_type=jnp.float32) * Q_SCALE
    qTb = qT.astype(jnp.bfloat16)
    for j in range(n_sub):
        qT_ref[0, j] = qTb[:, j * ATT_TILE:(j + 1) * ATT_TILE]

    vT = lax.dot_general(wvT_ref[...], hb, _NT, preferred_element_type=jnp.float32)
    vTb = vT.astype(jnp.bfloat16)
    for j in range(n_sub):
        vT_ref[0, j] = vTb[:, j * ATT_TILE:(j + 1) * ATT_TILE]

    gT = lax.dot_general(wgT_ref[...], hb, _NT, preferred_element_type=jnp.float32)
    sgTb = _silu(gT).astype(jnp.bfloat16)
    for j in range(n_sub):
        sgT_ref[0, j] = sgTb[:, j * ATT_TILE:(j + 1) * ATT_TILE]

    flT = lax.dot_general(wfT_ref[...], hb, _NT, preferred_element_type=jnp.float32)
    logf = _log_sigmoid(flT[:HEADS, :] + bf_ref[:HEADS, :])
    cT = _lane_cumsum(logf) + carry_c[:, 0:1]
    carry_c[...] = jnp.broadcast_to(cT[:, TOKEN_TILE - 1:TOKEN_TILE], carry_c.shape)
    cT_pad = jnp.concatenate(
        [cT * LOG2E, jnp.zeros((LANES - HEADS, TOKEN_TILE), jnp.float32)], axis=0)
    c_rows = cT_pad.T
    aug = jnp.dot(_bias_columns(c_rows), e_ref[...], preferred_element_type=jnp.float32)
    kk = jnp.dot(hb, wk_ref[...], preferred_element_type=jnp.float32)
    k_ref[0] = (kk + aug).astype(jnp.bfloat16)

    u = jnp.dot(hb, wu_ref[...], preferred_element_type=jnp.float32)
    gp = jnp.dot(hb, wgp_ref[...], preferred_element_type=jnp.float32)
    u_ext = jnp.concatenate([carry_u[...], u], axis=0)
    carry_u[...] = u[TOKEN_TILE - N_META:, :]
    p = _pool_mix(u_ext, u, wpool_ref, pscale_ref[...], _silu(gp))
    p_ref[0] = p.astype(jnp.bfloat16)


def _attn_kernel(qT_ref, k_ref, vT_ref, sgT_ref, kmeta_ref, vTmeta_ref, o_ref,
                 qa_ref, sa_ref, sb_ref, smeta_ref, acc_ref):
    n_qb = qT_ref.shape[1]
    heads = range(HEAD_GROUP)
    row3 = lax.broadcasted_iota(jnp.int32, (KEY_WIDTH - HEAD_DIM, ATT_TILE), 0)
    q_aug = jnp.where(row3 < 3, -1.0, 0.0).astype(jnp.bfloat16)
    for g in heads:
        qa_ref[g, HEAD_DIM:, :] = q_aug
    ones = jnp.ones((ONES_ROWS, ATT_TILE), jnp.bfloat16)
    ones_meta = jnp.ones((ONES_ROWS, META_PAD), jnp.bfloat16)
    krow = lax.broadcasted_iota(jnp.int32, (ATT_TILE, ATT_TILE), 0)
    qcol = lax.broadcasted_iota(jnp.int32, (ATT_TILE, ATT_TILE), 1)
    causal = krow <= qcol

    def hrows(g, width):
        return slice(g * width, (g + 1) * width)

    def colmax(s):
        return jnp.max(s, axis=0, keepdims=True)

    def head_scores(g, kj, dst_ref):
        start = pl.multiple_of(kj * ATT_TILE, ATT_TILE)
        kblk = k_ref[0, pl.ds(start, ATT_TILE), hrows(g, KEY_WIDTH)]
        s = jnp.dot(kblk, qa_ref[g], preferred_element_type=jnp.float32)
        dst_ref[g] = s
        return colmax(s)

    def scores_to(kj, dst_ref):
        return tuple(head_scores(g, kj, dst_ref) for g in heads)

    def consume(kj, src_ref, smax, ms):
        ms_new = []
        for g in heads:
            m_new = jnp.maximum(ms[g], smax[g])
            alpha = jnp.exp2(ms[g] - m_new)
            p = jnp.exp2(src_ref[g] - m_new).astype(jnp.bfloat16)
            vext = jnp.concatenate([vT_ref[0, kj, hrows(g, HEAD_DIM), :], ones], axis=0)
            acc_ref[g] = alpha * acc_ref[g] + jnp.dot(vext, p, preferred_element_type=jnp.float32)
            ms_new.append(m_new)
        return tuple(ms_new)

    def start_block(g, qi):
        qa_ref[g, :HEAD_DIM, :] = qT_ref[0, qi, hrows(g, HEAD_DIM), :]
        smeta_ref[g] = jnp.dot(kmeta_ref[:, hrows(g, KEY_WIDTH)], qa_ref[g],
                               preferred_element_type=jnp.float32)
        return head_scores(g, 0, sa_ref)

    def q_body(qi, smax_a):
        m0 = tuple(jnp.full((1, ATT_TILE), -jnp.inf, jnp.float32) for _ in heads)

        def pair(k, carry):
            ms, smax_a = carry
            smax_b = scores_to(2 * k + 1, sb_ref)
            ms = consume(2 * k, sa_ref, smax_a, ms)
            smax_a = scores_to(2 * k + 2, sa_ref)
            ms = consume(2 * k + 1, sb_ref, smax_b, ms)
            return ms, smax_a

        ms, smax_a = lax.fori_loop(0, qi // 2, pair, (m0, smax_a))

        def odd(carry):
            ms, smax_a = carry
            ms = consume(qi - 1, sa_ref, smax_a, ms)
            return ms, scores_to(qi, sa_ref)

        ms, _ = lax.cond(qi % 2 == 1, odd, lambda c: c, (ms, smax_a))

        q_next = jnp.minimum(qi + 1, n_qb - 1)
        smax_next = []
        for g in heads:
            s = jnp.where(causal, sa_ref[g], -jnp.inf)
            sm = smeta_ref[g]
            m_new = jnp.maximum(ms[g], jnp.maximum(colmax(s), colmax(sm)))
            alpha = jnp.exp2(ms[g] - m_new)
            p = jnp.exp2(s - m_new).astype(jnp.bfloat16)
            pm = jnp.exp2(sm - m_new).astype(jnp.bfloat16)
            smax_next.append(start_block(g, q_next))
            vext = jnp.concatenate([vT_ref[0, qi, hrows(g, HEAD_DIM), :], ones], axis=0)
            vmeta = jnp.concatenate([vTmeta_ref[hrows(g, HEAD_DIM), :], ones_meta], axis=0)
            acc = (alpha * acc_ref[g]
                   + jnp.dot(vext, p, preferred_element_type=jnp.float32)
                   + jnp.dot(vmeta, pm, preferred_element_type=jnp.float32))
            acc_ref[g] = jnp.zeros(acc_ref.shape[1:], jnp.float32)
            o = acc[:HEAD_DIM, :] / acc[HEAD_DIM:HEAD_DIM + 1, :]
            o = o * sgT_ref[0, qi, hrows(g, HEAD_DIM), :].astype(jnp.float32)
            o_ref[0, qi, hrows(g, HEAD_DIM), :] = o.astype(jnp.bfloat16)
        return tuple(smax_next)

    smax_first = []
    for g in heads:
        acc_ref[g] = jnp.zeros(acc_ref.shape[1:], jnp.float32)
        smax_first.append(start_block(g, 0))
    lax.fori_loop(0, n_qb, q_body, tuple(smax_first))


def _out_proj_kernel(x_ref, gin_ref, bin_ref, aT_ref, p_ref, waT_ref, wp_ref, g_ref, b_ref, o_ref):
    n_sub = TOKEN_TILE // ATT_TILE
    for j in range(n_sub):
        rows = slice(j * ATT_TILE, (j + 1) * ATT_TILE)
        hn = _layer_norm(x_ref[0, rows, :], gin_ref[...], bin_ref[...])
        yT = jnp.dot(waT_ref[...], aT_ref[0, j], preferred_element_type=jnp.float32)
        y = yT.T + jnp.dot(p_ref[0, rows, :], wp_ref[...], preferred_element_type=jnp.float32)
        o_ref[0, rows, :] = _layer_norm(DEEPNORM_ALPHA * hn + y, g_ref[...], b_ref[...])


def _bias_placement_matrix():
    e = np.zeros((LANES, HEADS * KEY_WIDTH), np.float32)
    for j in range(3):
        for h in range(HEADS):
            e[HEADS * j + h, KEY_WIDTH * h + HEAD_DIM + j] = 1.0
    return e


def _const_spec(shape):
    nd = len(shape)
    return pl.BlockSpec(shape, lambda *_: (0,) * nd, pipeline_mode=pl.Buffered(1))


def kernel(x, meta_tokens, ln_in_g, ln_in_b, w_in, b_forget, w_pool, pool_scale, w_out, ln_g, ln_b):
    B, L, D = x.shape
    assert D == D_MODEL and L % TOKEN_TILE == 0 and w_in.shape[0] == 1
    bf16, f32 = jnp.bfloat16, jnp.float32
    w = w_in[0]
    o_k, o_v, o_f = ATT_WIDTH, 2 * ATT_WIDTH, 3 * ATT_WIDTH
    o_g = o_f + HEADS
    o_u = o_g + ATT_WIDTH
    o_gp = o_u + D_MODEL

    wqT = w[:, :o_k].T.astype(bf16)
    wk = jnp.pad(w[:, o_k:o_v].reshape(D, HEADS, HEAD_DIM),
                 ((0, 0), (0, 0), (0, KEY_WIDTH - HEAD_DIM))).reshape(D, HEADS * KEY_WIDTH).astype(bf16)
    wvT = w[:, o_v:o_f].T.astype(bf16)
    wfT = jnp.pad(w[:, o_f:o_g].T, ((0, LANES - HEADS), (0, 0))).astype(bf16)
    bf_col = jnp.pad(b_forget[0].astype(f32), (0, LANES - HEADS)).reshape(LANES, 1)
    wgT = w[:, o_g:o_u].T.astype(bf16)
    wu = w[:, o_u:o_gp].astype(bf16)
    wgp = w[:, o_gp:].astype(bf16)
    wpool = w_pool[0].astype(bf16)
    pscale = pool_scale[0].reshape(1, D).astype(f32)
    waT = w_out[0, :ATT_WIDTH].T.astype(bf16)
    wp = w_out[0, ATT_WIDTH:].astype(bf16)
    e_mat = jnp.asarray(_bias_placement_matrix(), dtype=bf16)
    g_in = ln_in_g.reshape(1, D).astype(f32)
    b_in = ln_in_b.reshape(1, D).astype(f32)
    g_out = ln_g[0].reshape(1, D).astype(f32)
    b_out = ln_b[0].reshape(1, D).astype(f32)
    mt_pad = jnp.pad(meta_tokens.astype(f32), ((0, META_PAD - N_META), (0, 0)))

    kw = HEADS * KEY_WIDTH
    kmeta, vTmeta, umeta = pl.pallas_call(
        _meta_kernel,
        out_shape=(jax.ShapeDtypeStruct((META_PAD, kw), bf16),
                   jax.ShapeDtypeStruct((ATT_WIDTH, META_PAD), bf16),
                   jax.ShapeDtypeStruct((N_META, D), f32)),
        compiler_params=pltpu.CompilerParams(vmem_limit_bytes=VMEM_LIMIT),
        name="meta_proj",
    )(mt_pad, g_in, b_in, wk, wvT, wfT, bf_col, wu, e_mat)

    n_t = L // TOKEN_TILE
    n_sub = TOKEN_TILE // ATT_TILE
    n_ab = L // ATT_TILE
    tiled = jax.ShapeDtypeStruct((B, n_ab, ATT_WIDTH, ATT_TILE), bf16)
    tiled_spec = pl.BlockSpec((1, n_sub, ATT_WIDTH, ATT_TILE), lambda b, t: (b, t, 0, 0))
    qT, kaug, vT, sgT, pmix = pl.pallas_call(
        _in_proj_kernel,
        grid=(B, n_t),
        in_specs=[
            pl.BlockSpec((1, TOKEN_TILE, D), lambda b, t: (b, t, 0)),
            _const_spec((1, D)), _const_spec((1, D)),
            _const_spec((ATT_WIDTH, D)), _const_spec((D, kw)), _const_spec((ATT_WIDTH, D)),
            _const_spec((LANES, D)), _const_spec((LANES, 1)), _const_spec((ATT_WIDTH, D)),
            _const_spec((D, D)), _const_spec((D, D)),
            _const_spec((len(POOL_WINDOWS), POOL_GROUP_WIDTH, POOL_GROUP_WIDTH)),
            _const_spec((1, D)), _const_spec((LANES, kw)), _const_spec((N_META, D)),
        ],
        out_specs=(tiled_spec,
                   pl.BlockSpec((1, TOKEN_TILE, kw), lambda b, t: (b, t, 0)),
                   tiled_spec, tiled_spec,
                   pl.BlockSpec((1, TOKEN_TILE, D), lambda b, t: (b, t, 0))),
        out_shape=(tiled, jax.ShapeDtypeStruct((B, L, kw), bf16), tiled, tiled,
                   jax.ShapeDtypeStruct((B, L, D), bf16)),
        scratch_shapes=[pltpu.VMEM((HEADS, LANES), f32), pltpu.VMEM((N_META, D), f32)],
        compiler_params=pltpu.CompilerParams(
            dimension_semantics=("arbitrary", "arbitrary"), vmem_limit_bytes=VMEM_LIMIT),
        name="in_proj",
    )(x, g_in, b_in, wqT, wk, wvT, wfT, bf_col, wgT, wu, wgp, wpool, pscale, e_mat, umeta)

    gd, gk = HEAD_GROUP * HEAD_DIM, HEAD_GROUP * KEY_WIDTH
    head_tiled = pl.BlockSpec((1, n_ab, gd, ATT_TILE), lambda b, h: (b, 0, h, 0))
    aT = pl.pallas_call(
        _attn_kernel,
        grid=(B, HEADS // HEAD_GROUP),
        in_specs=[
            head_tiled,
            pl.BlockSpec((1, L, gk), lambda b, h: (b, 0, h)),
            head_tiled, head_tiled,
            pl.BlockSpec((META_PAD, gk), lambda b, h: (0, h)),
            pl.BlockSpec((gd, META_PAD), lambda b, h: (h, 0)),
        ],
        out_specs=head_tiled,
        out_shape=tiled,
        scratch_shapes=[pltpu.VMEM((HEAD_GROUP, KEY_WIDTH, ATT_TILE), bf16),
                        pltpu.VMEM((HEAD_GROUP, ATT_TILE, ATT_TILE), f32),
                        pltpu.VMEM((HEAD_GROUP, ATT_TILE, ATT_TILE), f32),
                        pltpu.VMEM((HEAD_GROUP, META_PAD, ATT_TILE), f32),
                        pltpu.VMEM((HEAD_GROUP, HEAD_DIM + ONES_ROWS, ATT_TILE), f32)],
        compiler_params=pltpu.CompilerParams(
            dimension_semantics=("arbitrary", "arbitrary"), vmem_limit_bytes=VMEM_LIMIT),
        name="fox_attn",
    )(qT, kaug, vT, sgT, kmeta, vTmeta)

    out = pl.pallas_call(
        _out_proj_kernel,
        grid=(B, n_t),
        in_specs=[
            pl.BlockSpec((1, TOKEN_TILE, D), lambda b, t: (b, t, 0)),
            _const_spec((1, D)), _const_spec((1, D)),
            tiled_spec,
            pl.BlockSpec((1, TOKEN_TILE, D), lambda b, t: (b, t, 0)),
            _const_spec((D, ATT_WIDTH)), _const_spec((D, D)),
            _const_spec((1, D)), _const_spec((1, D)),
        ],
        out_specs=pl.BlockSpec((1, TOKEN_TILE, D), lambda b, t: (b, t, 0)),
        out_shape=jax.ShapeDtypeStruct((B, L, D), x.dtype),
        compiler_params=pltpu.CompilerParams(
            dimension_semantics=("arbitrary", "arbitrary"), vmem_limit_bytes=VMEM_LIMIT),
        name="out_proj",
    )(x, g_in, b_in, aT, pmix, waT, wp, g_out, b_out)
    return out
```

```python
import math

import numpy as np
import jax
import jax.numpy as jnp
from jax import lax
from jax.experimental import pallas as pl
from jax.experimental.pallas import tpu as pltpu

D_MODEL = 1024
N_META = 16
HEADS = 16
HEAD_DIM = 64
ATT_WIDTH = HEADS * HEAD_DIM
POOL_WINDOWS = (2, 4, 8, 16)
POOL_GROUP_WIDTH = D_MODEL // len(POOL_WINDOWS)
LN_EPS = 1e-5
DEEPNORM_ALPHA = 2.0 ** 0.25
LOG2E = math.log2(math.e)
Q_SCALE = HEAD_DIM ** -0.5 * LOG2E

LANES = 128
PAIR_WIDTH = 2 * HEAD_DIM
KEY_WIDTH = PAIR_WIDTH + LANES
META_PAD = 128
MASKED_BIAS = 1e30
TOKEN_TILE = 512
ATT_TILE = 256
ONES_ROWS = 16
HEAD_GROUP = 4
VMEM_LIMIT = 56 * 1024 * 1024

_NT = (((1,), (1,)), ((), ()))


def _layer_norm(x, g, b):
    mu = jnp.mean(x, axis=-1, keepdims=True)
    xc = x - mu
    var = jnp.mean(xc * xc, axis=-1, keepdims=True)
    return xc * lax.rsqrt(var + LN_EPS) * g + b


def _log_sigmoid(z):
    return jnp.minimum(z, 0.0) - jnp.log(1.0 + jnp.exp(-jnp.abs(z)))


def _silu(z):
    return z / (1.0 + jnp.exp(-z))


def _lane_cumsum(x):
    n = x.shape[-1]
    lane = lax.broadcasted_iota(jnp.int32, x.shape, x.ndim - 1)
    d = 1
    while d < n:
        x = x + jnp.where(lane >= d, pltpu.roll(x, d, x.ndim - 1), 0.0)
        d *= 2
    return x


def _bias_columns(c_rows):
    hi = c_rows.astype(jnp.bfloat16).astype(jnp.float32)
    r1 = c_rows - hi
    mid = r1.astype(jnp.bfloat16).astype(jnp.float32)
    lo = (r1 - mid).astype(jnp.bfloat16).astype(jnp.float32)
    packed = hi + pltpu.roll(mid, HEADS, 1) + pltpu.roll(lo, 2 * HEADS, 1)
    return packed.astype(jnp.bfloat16)


def _pool_mix(u_ext, u, wpool_ref, pscale, gate):
    outs = []
    for gi, w in enumerate(POOL_WINDOWS):
        sl = slice(gi * POOL_GROUP_WIDTH, (gi + 1) * POOL_GROUP_WIDTH)
        r = u_ext[:, sl]
        s = 1
        while s < w:
            r = r + pltpu.roll(r, s, 0)
            s *= 2
        d = r[N_META:, :] * (1.0 / w) - u[:, sl]
        outs.append(jnp.dot(d.astype(jnp.bfloat16), wpool_ref[gi],
                            preferred_element_type=jnp.float32))
    y = jnp.concatenate(outs, axis=-1) * pscale
    return y * gate


def _meta_kernel(mt_ref, g_ref, b_ref, wk_ref, wvT_ref, wfT_ref, bf_ref, wu_ref,
                 kmeta_ref, augmeta_ref, vTmeta_ref, umeta_ref):
    hn = _layer_norm(mt_ref[...], g_ref[...], b_ref[...])
    hb = hn.astype(jnp.bfloat16)
    flT = lax.dot_general(wfT_ref[...], hb, _NT, preferred_element_type=jnp.float32)
    row = lax.broadcasted_iota(jnp.int32, flT.shape, 0)
    col = lax.broadcasted_iota(jnp.int32, flT.shape, 1)
    valid = (row < HEADS) & (col < N_META)
    logf = jnp.where(valid, _log_sigmoid(flT + bf_ref[...]), 0.0)
    cT = _lane_cumsum(logf)
    total = cT[:, META_PAD - 1:META_PAD]
    c_rows = ((cT - total) * LOG2E).T
    trow = lax.broadcasted_iota(jnp.int32, c_rows.shape, 0)
    tcol = lax.broadcasted_iota(jnp.int32, c_rows.shape, 1)
    c_rows = jnp.where((trow >= N_META) & (tcol < HEADS), MASKED_BIAS, c_rows)
    augmeta_ref[...] = _bias_columns(c_rows)
    kk = jnp.dot(hb, wk_ref[...], preferred_element_type=jnp.float32)
    kmeta_ref[...] = kk.astype(jnp.bfloat16)
    vT = lax.dot_general(wvT_ref[...], hb, _NT, preferred_element_type=jnp.float32)
    vTmeta_ref[...] = vT.astype(jnp.bfloat16)
    u = jnp.dot(hb, wu_ref[...], preferred_element_type=jnp.float32)
    umeta_ref[...] = u[:N_META, :]


def _in_proj_kernel(x_ref, g_ref, b_ref, wqT_ref, wk_ref, wvT_ref, wfT_ref, bf_ref, wgT_ref,
                    wu_ref, wgp_ref, wpool_ref, pscale_ref, umeta_ref,
                    qT_ref, k_ref, aug_ref, vT_ref, sgT_ref, p_ref, carry_c, carry_u):
    t = pl.program_id(1)

    @pl.when(t == 0)
    def _():
        carry_c[...] = jnp.zeros_like(carry_c)
        carry_u[...] = umeta_ref[...]

    hn = _layer_norm(x_ref[0], g_ref[...], b_ref[...])
    hb = hn.astype(jnp.bfloat16)
    n_sub = TOKEN_TILE // ATT_TILE

    qT = lax.dot_general(wqT_ref[...], hb, _NT, preferred_element_type=jnp.float32) * Q_SCALE
    qTb = qT.astype(jnp.bfloat16)
    for j in range(n_sub):
        qT_ref[0, j] = qTb[:, j * ATT_TILE:(j + 1) * ATT_TILE]

    vT = lax.dot_general(wvT_ref[...], hb, _NT, preferred_element_type=jnp.float32)
    vTb = vT.astype(jnp.bfloat16)
    for j in range(n_sub):
        vT_ref[0, j] = vTb[:, j * ATT_TILE:(j + 1) * ATT_TILE]

    gT = lax.dot_general(wgT_ref[...], hb, _NT, preferred_element_type=jnp.float32)
    sgTb = _silu(gT).astype(jnp.bfloat16)
    for j in range(n_sub):
        sgT_ref[0, j] = sgTb[:, j * ATT_TILE:(j + 1) * ATT_TILE]

    flT = lax.dot_general(wfT_ref[...], hb, _NT, preferred_element_type=jnp.float32)
    logf = _log_sigmoid(flT[:HEADS, :] + bf_ref[:HEADS, :])
    cT = _lane_cumsum(logf) + carry_c[:, 0:1]
    carry_c[...] = jnp.broadcast_to(cT[:, TOKEN_TILE - 1:TOKEN_TILE], carry_c.shape)
    cT_pad = jnp.concatenate(
        [cT * LOG2E, jnp.zeros((LANES - HEADS, TOKEN_TILE), jnp.float32)], axis=0)
    c_rows = cT_pad.T
    aug_ref[0] = _bias_columns(c_rows)
    kk = jnp.dot(hb, wk_ref[...], preferred_element_type=jnp.float32)
    k_ref[0] = kk.astype(jnp.bfloat16)

    u = jnp.dot(hb, wu_ref[...], preferred_element_type=jnp.float32)
    gp = jnp.dot(hb, wgp_ref[...], preferred_element_type=jnp.float32)
    u_ext = jnp.concatenate([carry_u[...], u], axis=0)
    carry_u[...] = u[TOKEN_TILE - N_META:, :]
    p = _pool_mix(u_ext, u, wpool_ref, pscale_ref[...], _silu(gp))
    p_ref[0] = p.astype(jnp.bfloat16)


def _attn_kernel(qT_ref, k_ref, aug_ref, vT_ref, sgT_ref, kmeta_ref, augmeta_ref, vTmeta_ref, o_ref,
                 qa_ref, sa_ref, sb_ref, smeta_ref, acc_ref):
    n_qb = qT_ref.shape[1]
    heads = range(HEAD_GROUP)
    sel_row = lax.broadcasted_iota(jnp.int32, (LANES, ATT_TILE), 0)
    for g in heads:
        h = pl.program_id(1) * HEAD_GROUP + g
        picks = (sel_row == h) | (sel_row == HEADS + h) | (sel_row == 2 * HEADS + h)
        qa_ref[g, :PAIR_WIDTH, :] = jnp.zeros((PAIR_WIDTH, ATT_TILE), jnp.bfloat16)
        qa_ref[g, PAIR_WIDTH:, :] = jnp.where(picks, -1.0, 0.0).astype(jnp.bfloat16)
    ones = jnp.ones((ONES_ROWS, ATT_TILE), jnp.bfloat16)
    ones_meta = jnp.ones((ONES_ROWS, META_PAD), jnp.bfloat16)
    krow = lax.broadcasted_iota(jnp.int32, (ATT_TILE, ATT_TILE), 0)
    qcol = lax.broadcasted_iota(jnp.int32, (ATT_TILE, ATT_TILE), 1)
    causal = krow <= qcol

    def hrows(g, width):
        return slice(g * width, (g + 1) * width)

    def colmax(s):
        return jnp.max(s, axis=0, keepdims=True)

    def head_scores(g, kj, dst_ref):
        rows = pl.ds(pl.multiple_of(kj * ATT_TILE, ATT_TILE), ATT_TILE)
        kblk = jnp.concatenate([k_ref[0, rows, hrows(g // 2, PAIR_WIDTH)], aug_ref[0, rows, :]], axis=1)
        s = jnp.dot(kblk, qa_ref[g], preferred_element_type=jnp.float32)
        dst_ref[g] = s
        return colmax(s)

    def scores_to(kj, dst_ref):
        return tuple(head_scores(g, kj, dst_ref) for g in heads)

    def consume(kj, src_ref, smax, ms):
        ms_new = []
        for g in heads:
            m_new = jnp.maximum(ms[g], smax[g])
            alpha = jnp.exp2(ms[g] - m_new)
            p = jnp.exp2(src_ref[g] - m_new).astype(jnp.bfloat16)
            vext = jnp.concatenate([vT_ref[0, kj, hrows(g, HEAD_DIM), :], ones], axis=0)
            acc_ref[g] = alpha * acc_ref[g] + jnp.dot(vext, p, preferred_element_type=jnp.float32)
            ms_new.append(m_new)
        return tuple(ms_new)

    def start_block(g, qi):
        qa_ref[g, hrows(g % 2, HEAD_DIM), :] = qT_ref[0, qi, hrows(g, HEAD_DIM), :]
        kmeta = jnp.concatenate([kmeta_ref[:, hrows(g // 2, PAIR_WIDTH)], augmeta_ref[...]], axis=1)
        smeta_ref[g] = jnp.dot(kmeta, qa_ref[g], preferred_element_type=jnp.float32)
        return head_scores(g, 0, sa_ref)

    def q_body(qi, smax_a):
        m0 = tuple(jnp.full((1, ATT_TILE), -jnp.inf, jnp.float32) for _ in heads)

        def pair(k, carry):
            ms, smax_a = carry
            smax_b = scores_to(2 * k + 1, sb_ref)
            ms = consume(2 * k, sa_ref, smax_a, ms)
            smax_a = scores_to(2 * k + 2, sa_ref)
            ms = consume(2 * k + 1, sb_ref, smax_b, ms)
            return ms, smax_a

        ms, smax_a = lax.fori_loop(0, qi // 2, pair, (m0, smax_a))

        def odd(carry):
            ms, smax_a = carry
            ms = consume(qi - 1, sa_ref, smax_a, ms)
            return ms, scores_to(qi, sa_ref)

        ms, _ = lax.cond(qi % 2 == 1, odd, lambda c: c, (ms, smax_a))

        q_next = jnp.minimum(qi + 1, n_qb - 1)
        smax_next = []
        for g in heads:
            s = jnp.where(causal, sa_ref[g], -jnp.inf)
            sm = smeta_ref[g]
            m_new = jnp.maximum(ms[g], jnp.maximum(colmax(s), colmax(sm)))
            alpha = jnp.exp2(ms[g] - m_new)
            p = jnp.exp2(s - m_new).astype(jnp.bfloat16)
            pm = jnp.exp2(sm - m_new).astype(jnp.bfloat16)
            smax_next.append(start_block(g, q_next))
            vext = jnp.concatenate([vT_ref[0, qi, hrows(g, HEAD_DIM), :], ones], axis=0)
            vmeta = jnp.concatenate([vTmeta_ref[hrows(g, HEAD_DIM), :], ones_meta], axis=0)
            acc = (alpha * acc_ref[g]
                   + jnp.dot(vext, p, preferred_element_type=jnp.float32)
                   + jnp.dot(vmeta, pm, preferred_element_type=jnp.float32))
            acc_ref[g] = jnp.zeros(acc_ref.shape[1:], jnp.float32)
            o = acc[:HEAD_DIM, :] / acc[HEAD_DIM:HEAD_DIM + 1, :]
            o = o * sgT_ref[0, qi, hrows(g, HEAD_DIM), :].astype(jnp.float32)
            o_ref[0, qi, hrows(g, HEAD_DIM), :] = o.astype(jnp.bfloat16)
        return tuple(smax_next)

    smax_first = []
    for g in heads:
        acc_ref[g] = jnp.zeros(acc_ref.shape[1:], jnp.float32)
        smax_first.append(start_block(g, 0))
    lax.fori_loop(0, n_qb, q_body, tuple(smax_first))


def _out_proj_kernel(x_ref, gin_ref, bin_ref, aT_ref, p_ref, waT_ref, wp_ref, g_ref, b_ref, o_ref):
    n_sub = TOKEN_TILE // ATT_TILE
    for j in range(n_sub):
        rows = slice(j * ATT_TILE, (j + 1) * ATT_TILE)
        hn = _layer_norm(x_ref[0, rows, :], gin_ref[...], bin_ref[...])
        yT = jnp.dot(waT_ref[...], aT_ref[0, j], preferred_element_type=jnp.float32)
        y = yT.T + jnp.dot(p_ref[0, rows, :], wp_ref[...], preferred_element_type=jnp.float32)
        o_ref[0, rows, :] = _layer_norm(DEEPNORM_ALPHA * hn + y, g_ref[...], b_ref[...])


def _const_spec(shape):
    nd = len(shape)
    return pl.BlockSpec(shape, lambda *_: (0,) * nd, pipeline_mode=pl.Buffered(1))


def kernel(x, meta_tokens, ln_in_g, ln_in_b, w_in, b_forget, w_pool, pool_scale, w_out, ln_g, ln_b):
    B, L, D = x.shape
    assert D == D_MODEL and L % TOKEN_TILE == 0 and w_in.shape[0] == 1
    bf16, f32 = jnp.bfloat16, jnp.float32
    w = w_in[0]
    o_k, o_v, o_f = ATT_WIDTH, 2 * ATT_WIDTH, 3 * ATT_WIDTH
    o_g = o_f + HEADS
    o_u = o_g + ATT_WIDTH
    o_gp = o_u + D_MODEL

    wqT = w[:, :o_k].T.astype(bf16)
    wk = w[:, o_k:o_v].astype(bf16)
    wvT = w[:, o_v:o_f].T.astype(bf16)
    wfT = jnp.pad(w[:, o_f:o_g].T, ((0, LANES - HEADS), (0, 0))).astype(bf16)
    bf_col = jnp.pad(b_forget[0].astype(f32), (0, LANES - HEADS)).reshape(LANES, 1)
    wgT = w[:, o_g:o_u].T.astype(bf16)
    wu = w[:, o_u:o_gp].astype(bf16)
    wgp = w[:, o_gp:].astype(bf16)
    wpool = w_pool[0].astype(bf16)
    pscale = pool_scale[0].reshape(1, D).astype(f32)
    waT = w_out[0, :ATT_WIDTH].T.astype(bf16)
    wp = w_out[0, ATT_WIDTH:].astype(bf16)
    g_in = ln_in_g.reshape(1, D).astype(f32)
    b_in = ln_in_b.reshape(1, D).astype(f32)
    g_out = ln_g[0].reshape(1, D).astype(f32)
    b_out = ln_b[0].reshape(1, D).astype(f32)
    mt_pad = jnp.pad(meta_tokens.astype(f32), ((0, META_PAD - N_META), (0, 0)))

    kmeta, augmeta, vTmeta, umeta = pl.pallas_call(
        _meta_kernel,
        out_shape=(jax.ShapeDtypeStruct((META_PAD, ATT_WIDTH), bf16),
                   jax.ShapeDtypeStruct((META_PAD, LANES), bf16),
                   jax.ShapeDtypeStruct((ATT_WIDTH, META_PAD), bf16),
                   jax.ShapeDtypeStruct((N_META, D), f32)),
        compiler_params=pltpu.CompilerParams(vmem_limit_bytes=VMEM_LIMIT),
        name="meta_proj",
    )(mt_pad, g_in, b_in, wk, wvT, wfT, bf_col, wu)

    n_t = L // TOKEN_TILE
    n_sub = TOKEN_TILE // ATT_TILE
    n_ab = L // ATT_TILE
    tiled = jax.ShapeDtypeStruct((B, n_ab, ATT_WIDTH, ATT_TILE), bf16)
    tiled_spec = pl.BlockSpec((1, n_sub, ATT_WIDTH, ATT_TILE), lambda b, t: (b, t, 0, 0))
    qT, kk, aug, vT, sgT, pmix = pl.pallas_call(
        _in_proj_kernel,
        grid=(B, n_t),
        in_specs=[
            pl.BlockSpec((1, TOKEN_TILE, D), lambda b, t: (b, t, 0)),
            _const_spec((1, D)), _const_spec((1, D)),
            _const_spec((ATT_WIDTH, D)), _const_spec((D, ATT_WIDTH)), _const_spec((ATT_WIDTH, D)),
            _const_spec((LANES, D)), _const_spec((LANES, 1)), _const_spec((ATT_WIDTH, D)),
            _const_spec((D, D)), _const_spec((D, D)),
            _const_spec((len(POOL_WINDOWS), POOL_GROUP_WIDTH, POOL_GROUP_WIDTH)),
            _const_spec((1, D)), _const_spec((N_META, D)),
        ],
        out_specs=(tiled_spec,
                   pl.BlockSpec((1, TOKEN_TILE, ATT_WIDTH), lambda b, t: (b, t, 0)),
                   pl.BlockSpec((1, TOKEN_TILE, LANES), lambda b, t: (b, t, 0)),
                   tiled_spec, tiled_spec,
                   pl.BlockSpec((1, TOKEN_TILE, D), lambda b, t: (b, t, 0))),
        out_shape=(tiled, jax.ShapeDtypeStruct((B, L, ATT_WIDTH), bf16),
                   jax.ShapeDtypeStruct((B, L, LANES), bf16), tiled, tiled,
                   jax.ShapeDtypeStruct((B, L, D), bf16)),
        scratch_shapes=[pltpu.VMEM((HEADS, LANES), f32), pltpu.VMEM((N_META, D), f32)],
        compiler_params=pltpu.CompilerParams(
            dimension_semantics=("arbitrary", "arbitrary"), vmem_limit_bytes=VMEM_LIMIT),
        name="in_proj",
    )(x, g_in, b_in, wqT, wk, wvT, wfT, bf_col, wgT, wu, wgp, wpool, pscale, umeta)

    gd = HEAD_GROUP * HEAD_DIM
    head_tiled = pl.BlockSpec((1, n_ab, gd, ATT_TILE), lambda b, h: (b, 0, h, 0))
    aT = pl.pallas_call(
        _attn_kernel,
        grid=(B, HEADS // HEAD_GROUP),
        in_specs=[
            head_tiled,
            pl.BlockSpec((1, L, gd), lambda b, h: (b, 0, h)),
            pl.BlockSpec((1, L, LANES), lambda b, h: (b, 0, 0)),
            head_tiled, head_tiled,
            pl.BlockSpec((META_PAD, gd), lambda b, h: (0, h)),
            pl.BlockSpec((META_PAD, LANES), lambda b, h: (0, 0)),
            pl.BlockSpec((gd, META_PAD), lambda b, h: (h, 0)),
        ],
        out_specs=head_tiled,
        out_shape=tiled,
        scratch_shapes=[pltpu.VMEM((HEAD_GROUP, KEY_WIDTH, ATT_TILE), bf16),
                        pltpu.VMEM((HEAD_GROUP, ATT_TILE, ATT_TILE), f32),
                        pltpu.VMEM((HEAD_GROUP, ATT_TILE, ATT_TILE), f32),
                        pltpu.VMEM((HEAD_GROUP, META_PAD, ATT_TILE), f32),
                        pltpu.VMEM((HEAD_GROUP, HEAD_DIM + ONES_ROWS, ATT_TILE), f32)],
        compiler_params=pltpu.CompilerParams(
            dimension_semantics=("arbitrary", "arbitrary"), vmem_limit_bytes=VMEM_LIMIT),
        name="fox_attn",
    )(qT, kk, aug, vT, sgT, kmeta, augmeta, vTmeta)

    out = pl.pallas_call(
        _out_proj_kernel,
        grid=(B, n_t),
        in_specs=[
            pl.BlockSpec((1, TOKEN_TILE, D), lambda b, t: (b, t, 0)),
            _const_spec((1, D)), _const_spec((1, D)),
            tiled_spec,
            pl.BlockSpec((1, TOKEN_TILE, D), lambda b, t: (b, t, 0)),
            _const_spec((D, ATT_WIDTH)), _const_spec((D, D)),
            _const_spec((1, D)), _const_spec((1, D)),
        ],
        out_specs=pl.BlockSpec((1, TOKEN_TILE, D), lambda b, t: (b, t, 0)),
        out_shape=jax.ShapeDtypeStruct((B, L, D), x.dtype),
        compiler_params=pltpu.CompilerParams(
            dimension_semantics=("arbitrary", "arbitrary"), vmem_limit_bytes=VMEM_LIMIT),
        name="out_proj",
    )(x, g_in, b_in, aT, pmix, waT, wp, g_out, b_out)
    return out
```

```python
import math

import jax
import jax.numpy as jnp
from jax import lax
from jax.experimental import pallas as pl
from jax.experimental.pallas import tpu as pltpu

D_MODEL = 1024
N_META = 16
HEADS = 16
HEAD_DIM = 64
ATT_WIDTH = HEADS * HEAD_DIM
POOL_WINDOWS = (2, 4, 8, 16)
POOL_GROUP_WIDTH = D_MODEL // len(POOL_WINDOWS)
LN_EPS = 1e-5
DEEPNORM_ALPHA = 2.0 ** 0.25
LOG2E = math.log2(math.e)
Q_SCALE = HEAD_DIM ** -0.5 * LOG2E

LANES = 128
PAIR_WIDTH = 2 * HEAD_DIM
KEY_WIDTH = PAIR_WIDTH + LANES
META_PAD = 128
MASKED_BIAS = 1e30
TOKEN_TILE = 512
ATT_TILE = 256
Q_TILE = 2 * ATT_TILE
ONES_ROWS = 16
HEAD_GROUP = 4
VMEM_LIMIT = 56 * 1024 * 1024

_NT = (((1,), (1,)), ((), ()))


def _layer_norm(x, g, b):
    mu = jnp.mean(x, axis=-1, keepdims=True)
    xc = x - mu
    var = jnp.mean(xc * xc, axis=-1, keepdims=True)
    return xc * lax.rsqrt(var + LN_EPS) * g + b


def _log_sigmoid(z):
    return jnp.minimum(z, 0.0) - jnp.log(1.0 + jnp.exp(-jnp.abs(z)))


def _silu(z):
    return z / (1.0 + jnp.exp(-z))


def _lane_cumsum(x):
    n = x.shape[-1]
    lane = lax.broadcasted_iota(jnp.int32, x.shape, x.ndim - 1)
    d = 1
    while d < n:
        x = x + jnp.where(lane >= d, pltpu.roll(x, d, x.ndim - 1), 0.0)
        d *= 2
    return x


def _bias_columns(c_rows):
    hi = c_rows.astype(jnp.bfloat16).astype(jnp.float32)
    r1 = c_rows - hi
    mid = r1.astype(jnp.bfloat16).astype(jnp.float32)
    lo = (r1 - mid).astype(jnp.bfloat16).astype(jnp.float32)
    packed = hi + pltpu.roll(mid, HEADS, 1) + pltpu.roll(lo, 2 * HEADS, 1)
    return packed.astype(jnp.bfloat16)


def _pool_mix(u_ext, u, wpool_ref, pscale, gate):
    outs = []
    for gi, w in enumerate(POOL_WINDOWS):
        sl = slice(gi * POOL_GROUP_WIDTH, (gi + 1) * POOL_GROUP_WIDTH)
        r = u_ext[:, sl]
        s = 1
        while s < w:
            r = r + pltpu.roll(r, s, 0)
            s *= 2
        d = r[N_META:, :] * (1.0 / w) - u[:, sl]
        outs.append(jnp.dot(d.astype(jnp.bfloat16), wpool_ref[gi],
                            preferred_element_type=jnp.float32))
    y = jnp.concatenate(outs, axis=-1) * pscale
    return y * gate


def _meta_kernel(mt_ref, g_ref, b_ref, wk_ref, wvT_ref, wfT_ref, bf_ref, wu_ref,
                 kmeta_ref, augmeta_ref, vTmeta_ref, umeta_ref):
    hn = _layer_norm(mt_ref[...], g_ref[...], b_ref[...])
    hb = hn.astype(jnp.bfloat16)
    flT = lax.dot_general(wfT_ref[...], hb, _NT, preferred_element_type=jnp.float32)
    row = lax.broadcasted_iota(jnp.int32, flT.shape, 0)
    col = lax.broadcasted_iota(jnp.int32, flT.shape, 1)
    valid = (row < HEADS) & (col < N_META)
    logf = jnp.where(valid, _log_sigmoid(flT + bf_ref[...]), 0.0)
    cT = _lane_cumsum(logf)
    total = cT[:, META_PAD - 1:META_PAD]
    c_rows = ((cT - total) * LOG2E).T
    trow = lax.broadcasted_iota(jnp.int32, c_rows.shape, 0)
    tcol = lax.broadcasted_iota(jnp.int32, c_rows.shape, 1)
    c_rows = jnp.where((trow >= N_META) & (tcol < HEADS), MASKED_BIAS, c_rows)
    augmeta_ref[...] = _bias_columns(c_rows)
    kk = jnp.dot(hb, wk_ref[...], preferred_element_type=jnp.float32)
    kmeta_ref[...] = kk.astype(jnp.bfloat16)
    vT = lax.dot_general(wvT_ref[...], hb, _NT, preferred_element_type=jnp.float32)
    vTmeta_ref[...] = vT.astype(jnp.bfloat16)
    u = jnp.dot(hb, wu_ref[...], preferred_element_type=jnp.float32)
    umeta_ref[...] = u[:N_META, :]


def _in_proj_kernel(x_ref, g_ref, b_ref, wqT_ref, wk_ref, wvT_ref, wfT_ref, bf_ref, wgT_ref,
                    wu_ref, wgp_ref, wpool_ref, pscale_ref, umeta_ref,
                    qT_ref, k_ref, aug_ref, vT_ref, sgT_ref, p_ref, carry_c, carry_u):
    t = pl.program_id(1)

    @pl.when(t == 0)
    def _():
        carry_c[...] = jnp.zeros_like(carry_c)
        carry_u[...] = umeta_ref[...]

    hn = _layer_norm(x_ref[0], g_ref[...], b_ref[...])
    hb = hn.astype(jnp.bfloat16)

    qT = lax.dot_general(wqT_ref[...], hb, _NT, preferred_element_type=jnp.float32) * Q_SCALE
    qT_ref[0, 0] = qT.astype(jnp.bfloat16)

    vT = lax.dot_general(wvT_ref[...], hb, _NT, preferred_element_type=jnp.float32)
    vTb = vT.astype(jnp.bfloat16)
    for j in range(TOKEN_TILE // ATT_TILE):
        vT_ref[0, j] = vTb[:, j * ATT_TILE:(j + 1) * ATT_TILE]

    gT = lax.dot_general(wgT_ref[...], hb, _NT, preferred_element_type=jnp.float32)
    sgT_ref[0, 0] = _silu(gT).astype(jnp.bfloat16)

    flT = lax.dot_general(wfT_ref[...], hb, _NT, preferred_element_type=jnp.float32)
    logf = _log_sigmoid(flT[:HEADS, :] + bf_ref[:HEADS, :])
    cT = _lane_cumsum(logf) + carry_c[:, 0:1]
    carry_c[...] = jnp.broadcast_to(cT[:, TOKEN_TILE - 1:TOKEN_TILE], carry_c.shape)
    cT_pad = jnp.concatenate(
        [cT * LOG2E, jnp.zeros((LANES - HEADS, TOKEN_TILE), jnp.float32)], axis=0)
    c_rows = cT_pad.T
    aug_ref[0] = _bias_columns(c_rows)
    kk = jnp.dot(hb, wk_ref[...], preferred_element_type=jnp.float32)
    k_ref[0] = kk.astype(jnp.bfloat16)

    u = jnp.dot(hb, wu_ref[...], preferred_element_type=jnp.float32)
    gp = jnp.dot(hb, wgp_ref[...], preferred_element_type=jnp.float32)
    u_ext = jnp.concatenate([carry_u[...], u], axis=0)
    carry_u[...] = u[TOKEN_TILE - N_META:, :]
    p = _pool_mix(u_ext, u, wpool_ref, pscale_ref[...], _silu(gp))
    p_ref[0] = p.astype(jnp.bfloat16)


def _attn_kernel(qT_ref, k_ref, aug_ref, vT_ref, sgT_ref, kmeta_ref, augmeta_ref, vTmeta_ref, o_ref,
                 qa_ref, s_ref, smeta_ref, acc_ref):
    n_qb = qT_ref.shape[1]
    heads = range(HEAD_GROUP)
    sel_row = lax.broadcasted_iota(jnp.int32, (LANES, Q_TILE), 0)
    for g in heads:
        h = pl.program_id(1) * HEAD_GROUP + g
        picks = (sel_row == h) | (sel_row == HEADS + h) | (sel_row == 2 * HEADS + h)
        for par in range(2):
            qa_ref[par, g, :PAIR_WIDTH, :] = jnp.zeros((PAIR_WIDTH, Q_TILE), jnp.bfloat16)
            qa_ref[par, g, PAIR_WIDTH:, :] = jnp.where(picks, -1.0, 0.0).astype(jnp.bfloat16)
    ones = jnp.ones((ONES_ROWS, ATT_TILE), jnp.bfloat16)
    ones_meta = jnp.ones((ONES_ROWS, META_PAD), jnp.bfloat16)
    krow = lax.broadcasted_iota(jnp.int32, (ATT_TILE, Q_TILE), 0)
    qcol = lax.broadcasted_iota(jnp.int32, (ATT_TILE, Q_TILE), 1)
    causal_a = krow <= qcol
    causal_b = causal_a[:, :ATT_TILE]
    hidden_b = jnp.full((ATT_TILE, Q_TILE - ATT_TILE), -jnp.inf, jnp.float32)
    acc_zero = jnp.zeros(acc_ref.shape[2:], jnp.float32)

    def hrows(g, width):
        return slice(g * width, (g + 1) * width)

    def colmax(s):
        return jnp.max(s, axis=0, keepdims=True)

    def key_tile(g, kj):
        rows = pl.ds(pl.multiple_of(kj * ATT_TILE, ATT_TILE), ATT_TILE)
        return jnp.concatenate([k_ref[0, rows, hrows(g // 2, PAIR_WIDTH)], aug_ref[0, rows, :]], axis=1)

    def value_tile(g, kj):
        return jnp.concatenate([vT_ref[0, kj, hrows(g, HEAD_DIM), :], ones], axis=0)

    def head_scores(g, kj, qa, dst):
        s = jnp.dot(key_tile(g, kj), qa[g], preferred_element_type=jnp.float32)
        dst[g] = s
        return colmax(s)

    def head_consume(g, kj, src, acc, smax, m):
        m_new = jnp.maximum(m, smax)
        alpha = jnp.exp2(m - m_new)
        p = jnp.exp2(src[g] - m_new).astype(jnp.bfloat16)
        acc[g] = alpha * acc[g] + jnp.dot(value_tile(g, kj), p, preferred_element_type=jnp.float32)
        return m_new

    def start_block(g, qi, qa, smeta, dst):
        qa[g, hrows(g % 2, HEAD_DIM), :] = qT_ref[0, qi, hrows(g, HEAD_DIM), :]
        kmeta = jnp.concatenate([kmeta_ref[:, hrows(g // 2, PAIR_WIDTH)], augmeta_ref[...]], axis=1)
        smeta[g] = jnp.dot(kmeta, qa[g], preferred_element_type=jnp.float32)
        return head_scores(g, 0, qa, dst)

    def block(qi, par, smax_first):
        qa, qa_nx = qa_ref.at[par], qa_ref.at[1 - par]
        smeta, smeta_nx = smeta_ref.at[par], smeta_ref.at[1 - par]
        acc, acc_nx = acc_ref.at[par], acc_ref.at[1 - par]
        s_first, s_other, s_nx = s_ref.at[2 * par], s_ref.at[1], s_ref.at[2 - 2 * par]
        m0 = tuple(jnp.full((1, Q_TILE), -jnp.inf, jnp.float32) for _ in heads)

        def step(k_next, dst, k_cur, src, smax_cur, ms):
            smax_next, ms_new = [], []
            for g in heads:
                smax_next.append(head_scores(g, k_next, qa, dst))
                ms_new.append(head_consume(g, k_cur, src, acc, smax_cur[g], ms[g]))
            return tuple(ms_new), tuple(smax_next)

        def pair(k, carry):
            ms, smax = carry
            ms, smax_o = step(2 * k + 1, s_other, 2 * k, s_first, smax, ms)
            ms, smax = step(2 * k + 2, s_first, 2 * k + 1, s_other, smax_o, ms)
            return ms, smax

        ms, _ = lax.fori_loop(0, qi, pair, (m0, smax_first))

        q_next = jnp.minimum(qi + 1, n_qb - 1)
        smax_next = []
        for g in heads:
            sb = jnp.dot(key_tile(g, 2 * qi + 1), qa[g][:, ATT_TILE:],
                         preferred_element_type=jnp.float32)
            sb = jnp.concatenate([hidden_b, jnp.where(causal_b, sb, -jnp.inf)], axis=1)
            s = jnp.where(causal_a, s_first[g], -jnp.inf)
            sm = smeta[g]
            m_new = jnp.maximum(jnp.maximum(ms[g], colmax(s)), jnp.maximum(colmax(sb), colmax(sm)))
            alpha = jnp.exp2(ms[g] - m_new)
            p = jnp.exp2(s - m_new).astype(jnp.bfloat16)
            pb = jnp.exp2(sb - m_new).astype(jnp.bfloat16)
            pm = jnp.exp2(sm - m_new).astype(jnp.bfloat16)
            smax_next.append(start_block(g, q_next, qa_nx, smeta_nx, s_nx))
            acc_nx[g] = acc_zero
            vmeta = jnp.concatenate([vTmeta_ref[hrows(g, HEAD_DIM), :], ones_meta], axis=0)
            a = (alpha * acc[g]
                 + jnp.dot(value_tile(g, 2 * qi), p, preferred_element_type=jnp.float32)
                 + jnp.dot(value_tile(g, 2 * qi + 1), pb, preferred_element_type=jnp.float32)
                 + jnp.dot(vmeta, pm, preferred_element_type=jnp.float32))
            o = a[:HEAD_DIM, :] / a[HEAD_DIM:HEAD_DIM + 1, :]
            o = o * sgT_ref[0, qi, hrows(g, HEAD_DIM), :].astype(jnp.float32)
            o_ref[0, qi, hrows(g, HEAD_DIM), :] = o.astype(jnp.bfloat16)
        return tuple(smax_next)

    def two_blocks(j, smax_first):
        return block(2 * j + 1, 1, block(2 * j, 0, smax_first))

    smax_first = []
    for g in heads:
        acc_ref[0, g] = acc_zero
        smax_first.append(start_block(g, 0, qa_ref.at[0], smeta_ref.at[0], s_ref.at[0]))
    lax.fori_loop(0, n_qb // 2, two_blocks, tuple(smax_first))


def _out_proj_kernel(x_ref, gin_ref, bin_ref, aT_ref, p_ref, waT_ref, wp_ref, g_ref, b_ref, o_ref):
    for j in range(TOKEN_TILE // ATT_TILE):
        rows = slice(j * ATT_TILE, (j + 1) * ATT_TILE)
        hn = _layer_norm(x_ref[0, rows, :], gin_ref[...], bin_ref[...])
        yT = jnp.dot(waT_ref[...], aT_ref[0, 0, :, rows], preferred_element_type=jnp.float32)
        y = yT.T + jnp.dot(p_ref[0, rows, :], wp_ref[...], preferred_element_type=jnp.float32)
        o_ref[0, rows, :] = _layer_norm(DEEPNORM_ALPHA * hn + y, g_ref[...], b_ref[...])


def _const_spec(shape):
    nd = len(shape)
    return pl.BlockSpec(shape, lambda *_: (0,) * nd, pipeline_mode=pl.Buffered(1))


def kernel(x, meta_tokens, ln_in_g, ln_in_b, w_in, b_forget, w_pool, pool_scale, w_out, ln_g, ln_b):
    B, L, D = x.shape
    assert D == D_MODEL and L % (2 * Q_TILE) == 0 and TOKEN_TILE == Q_TILE and w_in.shape[0] == 1
    bf16, f32 = jnp.bfloat16, jnp.float32
    w = w_in[0]
    o_k, o_v, o_f = ATT_WIDTH, 2 * ATT_WIDTH, 3 * ATT_WIDTH
    o_g = o_f + HEADS
    o_u = o_g + ATT_WIDTH
    o_gp = o_u + D_MODEL

    wqT = w[:, :o_k].T.astype(bf16)
    wk = w[:, o_k:o_v].astype(bf16)
    wvT = w[:, o_v:o_f].T.astype(bf16)
    wfT = jnp.pad(w[:, o_f:o_g].T, ((0, LANES - HEADS), (0, 0))).astype(bf16)
    bf_col = jnp.pad(b_forget[0].astype(f32), (0, LANES - HEADS)).reshape(LANES, 1)
    wgT = w[:, o_g:o_u].T.astype(bf16)
    wu = w[:, o_u:o_gp].astype(bf16)
    wgp = w[:, o_gp:].astype(bf16)
    wpool = w_pool[0].astype(bf16)
    pscale = pool_scale[0].reshape(1, D).astype(f32)
    waT = w_out[0, :ATT_WIDTH].T.astype(bf16)
    wp = w_out[0, ATT_WIDTH:].astype(bf16)
    g_in = ln_in_g.reshape(1, D).astype(f32)
    b_in = ln_in_b.reshape(1, D).astype(f32)
    g_out = ln_g[0].reshape(1, D).astype(f32)
    b_out = ln_b[0].reshape(1, D).astype(f32)
    mt_pad = jnp.pad(meta_tokens.astype(f32), ((0, META_PAD - N_META), (0, 0)))

    kmeta, augmeta, vTmeta, umeta = pl.pallas_call(
        _meta_kernel,
        out_shape=(jax.ShapeDtypeStruct((META_PAD, ATT_WIDTH), bf16),
                   jax.ShapeDtypeStruct((META_PAD, LANES), bf16),
                   jax.ShapeDtypeStruct((ATT_WIDTH, META_PAD), bf16),
                   jax.ShapeDtypeStruct((N_META, D), f32)),
        compiler_params=pltpu.CompilerParams(vmem_limit_bytes=VMEM_LIMIT),
        name="meta_proj",
    )(mt_pad, g_in, b_in, wk, wvT, wfT, bf_col, wu)

    n_t = L // TOKEN_TILE
    n_sub = TOKEN_TILE // ATT_TILE
    n_ab = L // ATT_TILE
    ktiled = jax.ShapeDtypeStruct((B, n_ab, ATT_WIDTH, ATT_TILE), bf16)
    ktiled_spec = pl.BlockSpec((1, n_sub, ATT_WIDTH, ATT_TILE), lambda b, t: (b, t, 0, 0))
    qtiled = jax.ShapeDtypeStruct((B, n_t, ATT_WIDTH, Q_TILE), bf16)
    qtiled_spec = pl.BlockSpec((1, 1, ATT_WIDTH, Q_TILE), lambda b, t: (b, t, 0, 0))
    qT, kk, aug, vT, sgT, pmix = pl.pallas_call(
        _in_proj_kernel,
        grid=(B, n_t),
        in_specs=[
            pl.BlockSpec((1, TOKEN_TILE, D), lambda b, t: (b, t, 0)),
            _const_spec((1, D)), _const_spec((1, D)),
            _const_spec((ATT_WIDTH, D)), _const_spec((D, ATT_WIDTH)), _const_spec((ATT_WIDTH, D)),
            _const_spec((LANES, D)), _const_spec((LANES, 1)), _const_spec((ATT_WIDTH, D)),
            _const_spec((D, D)), _const_spec((D, D)),
            _const_spec((len(POOL_WINDOWS), POOL_GROUP_WIDTH, POOL_GROUP_WIDTH)),
            _const_spec((1, D)), _const_spec((N_META, D)),
        ],
        out_specs=(qtiled_spec,
                   pl.BlockSpec((1, TOKEN_TILE, ATT_WIDTH), lambda b, t: (b, t, 0)),
                   pl.BlockSpec((1, TOKEN_TILE, LANES), lambda b, t: (b, t, 0)),
                   ktiled_spec, qtiled_spec,
                   pl.BlockSpec((1, TOKEN_TILE, D), lambda b, t: (b, t, 0))),
        out_shape=(qtiled, jax.ShapeDtypeStruct((B, L, ATT_WIDTH), bf16),
                   jax.ShapeDtypeStruct((B, L, LANES), bf16), ktiled, qtiled,
                   jax.ShapeDtypeStruct((B, L, D), bf16)),
        scratch_shapes=[pltpu.VMEM((HEADS, LANES), f32), pltpu.VMEM((N_META, D), f32)],
        compiler_params=pltpu.CompilerParams(
            dimension_semantics=("arbitrary", "arbitrary"), vmem_limit_bytes=VMEM_LIMIT),
        name="in_proj",
    )(x, g_in, b_in, wqT, wk, wvT, wfT, bf_col, wgT, wu, wgp, wpool, pscale, umeta)

    gd = HEAD_GROUP * HEAD_DIM
    head_ktiled = pl.BlockSpec((1, n_ab, gd, ATT_TILE), lambda b, h: (b, 0, h, 0))
    head_qtiled = pl.BlockSpec((1, n_t, gd, Q_TILE), lambda b, h: (b, 0, h, 0))
    aT = pl.pallas_call(
        _attn_kernel,
        grid=(B, HEADS // HEAD_GROUP),
        in_specs=[
            head_qtiled,
            pl.BlockSpec((1, L, gd), lambda b, h: (b, 0, h)),
            pl.BlockSpec((1, L, LANES), lambda b, h: (b, 0, 0)),
            head_ktiled, head_qtiled,
            pl.BlockSpec((META_PAD, gd), lambda b, h: (0, h)),
            pl.BlockSpec((META_PAD, LANES), lambda b, h: (0, 0)),
            pl.BlockSpec((gd, META_PAD), lambda b, h: (h, 0)),
        ],
        out_specs=head_qtiled,
        out_shape=qtiled,
        scratch_shapes=[pltpu.VMEM((2, HEAD_GROUP, KEY_WIDTH, Q_TILE), bf16),
                        pltpu.VMEM((3, HEAD_GROUP, ATT_TILE, Q_TILE), f32),
                        pltpu.VMEM((2, HEAD_GROUP, META_PAD, Q_TILE), f32),
                        pltpu.VMEM((2, HEAD_GROUP, HEAD_DIM + ONES_ROWS, Q_TILE), f32)],
        compiler_params=pltpu.CompilerParams(
            dimension_semantics=("arbitrary", "arbitrary"), vmem_limit_bytes=VMEM_LIMIT),
        name="fox_attn",
    )(qT, kk, aug, vT, sgT, kmeta, augmeta, vTmeta)

    out = pl.pallas_call(
        _out_proj_kernel,
        grid=(B, n_t),
        in_specs=[
            pl.BlockSpec((1, TOKEN_TILE, D), lambda b, t: (b, t, 0)),
            _const_spec((1, D)), _const_spec((1, D)),
            qtiled_spec,
            pl.BlockSpec((1, TOKEN_TILE, D), lambda b, t: (b, t, 0)),
            _const_spec((D, ATT_WIDTH)), _const_spec((D, D)),
            _const_spec((1, D)), _const_spec((1, D)),
        ],
        out_specs=pl.BlockSpec((1, TOKEN_TILE, D), lambda b, t: (b, t, 0)),
        out_shape=jax.ShapeDtypeStruct((B, L, D), x.dtype),
        compiler_params=pltpu.CompilerParams(
            dimension_semantics=("arbitrary", "arbitrary"), vmem_limit_bytes=VMEM_LIMIT),
        name="out_proj",
    )(x, g_in, b_in, aT, pmix, waT, wp, g_out, b_out)
    return out
```

```python
import math

import jax
import jax.numpy as jnp
from jax import lax
from jax.experimental import pallas as pl
from jax.experimental.pallas import tpu as pltpu

D_MODEL = 1024
N_META = 16
HEADS = 16
HEAD_DIM = 64
ATT_WIDTH = HEADS * HEAD_DIM
POOL_WINDOWS = (2, 4, 8, 16)
POOL_GROUP_WIDTH = D_MODEL // len(POOL_WINDOWS)
LN_EPS = 1e-5
DEEPNORM_ALPHA = 2.0 ** 0.25
LOG2E = math.log2(math.e)
Q_SCALE = HEAD_DIM ** -0.5 * LOG2E

LANES = 128
PAIR_WIDTH = 2 * HEAD_DIM
KEY_WIDTH = PAIR_WIDTH + LANES
META_PAD = 128
MASKED_BIAS = 1e30
TOKEN_TILE = 512
ATT_TILE = 256
Q_TILE = 2 * ATT_TILE
ONES_ROWS = 16
HEAD_GROUP = 4
VMEM_LIMIT = 56 * 1024 * 1024

_NT = (((1,), (1,)), ((), ()))


def _layer_norm(x, g, b):
    mu = jnp.mean(x, axis=-1, keepdims=True)
    xc = x - mu
    var = jnp.mean(xc * xc, axis=-1, keepdims=True)
    return xc * lax.rsqrt(var + LN_EPS) * g + b


def _log_sigmoid(z):
    return jnp.minimum(z, 0.0) - jnp.log(1.0 + jnp.exp(-jnp.abs(z)))


def _silu(z):
    return z / (1.0 + jnp.exp(-z))


def _lane_cumsum(x):
    n = x.shape[-1]
    lane = lax.broadcasted_iota(jnp.int32, x.shape, x.ndim - 1)
    d = 1
    while d < n:
        x = x + jnp.where(lane >= d, pltpu.roll(x, d, x.ndim - 1), 0.0)
        d *= 2
    return x


def _bias_columns(c_rows):
    hi = c_rows.astype(jnp.bfloat16).astype(jnp.float32)
    r1 = c_rows - hi
    mid = r1.astype(jnp.bfloat16).astype(jnp.float32)
    lo = (r1 - mid).astype(jnp.bfloat16).astype(jnp.float32)
    packed = hi + pltpu.roll(mid, HEADS, 1) + pltpu.roll(lo, 2 * HEADS, 1)
    return packed.astype(jnp.bfloat16)


def _pool_mix(u_ext, u, wpool_ref, pscale, gate):
    outs = []
    for gi, w in enumerate(POOL_WINDOWS):
        sl = slice(gi * POOL_GROUP_WIDTH, (gi + 1) * POOL_GROUP_WIDTH)
        r = u_ext[:, sl]
        s = 1
        while s < w:
            r = r + pltpu.roll(r, s, 0)
            s *= 2
        d = r[N_META:, :] * (1.0 / w) - u[:, sl]
        outs.append(jnp.dot(d.astype(jnp.bfloat16), wpool_ref[gi],
                            preferred_element_type=jnp.float32))
    y = jnp.concatenate(outs, axis=-1) * pscale
    return y * gate


def _meta_kernel(mt_ref, g_ref, b_ref, wk_ref, wvT_ref, wfT_ref, bf_ref, wu_ref,
                 kmeta_ref, augmeta_ref, vTmeta_ref, umeta_ref):
    hn = _layer_norm(mt_ref[...], g_ref[...], b_ref[...])
    hb = hn.astype(jnp.bfloat16)
    flT = lax.dot_general(wfT_ref[...], hb, _NT, preferred_element_type=jnp.float32)
    row = lax.broadcasted_iota(jnp.int32, flT.shape, 0)
    col = lax.broadcasted_iota(jnp.int32, flT.shape, 1)
    valid = (row < HEADS) & (col < N_META)
    logf = jnp.where(valid, _log_sigmoid(flT + bf_ref[...]), 0.0)
    cT = _lane_cumsum(logf)
    total = cT[:, META_PAD - 1:META_PAD]
    c_rows = ((cT - total) * LOG2E).T
    trow = lax.broadcasted_iota(jnp.int32, c_rows.shape, 0)
    tcol = lax.broadcasted_iota(jnp.int32, c_rows.shape, 1)
    c_rows = jnp.where((trow >= N_META) & (tcol < HEADS), MASKED_BIAS, c_rows)
    augmeta_ref[...] = _bias_columns(c_rows)
    kk = jnp.dot(hb, wk_ref[...], preferred_element_type=jnp.float32)
    kmeta_ref[...] = kk.astype(jnp.bfloat16)
    vT = lax.dot_general(wvT_ref[...], hb, _NT, preferred_element_type=jnp.float32)
    vTmeta_ref[...] = vT.astype(jnp.bfloat16)
    u = jnp.dot(hb, wu_ref[...], preferred_element_type=jnp.float32)
    umeta_ref[...] = u[:N_META, :]


def _in_proj_kernel(x_ref, g_ref, b_ref, wqT_ref, wk_ref, wvT_ref, wfT_ref, bf_ref, wgT_ref,
                    wu_ref, wgp_ref, wpool_ref, pscale_ref, umeta_ref,
                    qT_ref, k_ref, aug_ref, vT_ref, sgT_ref, p_ref, carry_c, carry_u):
    t = pl.program_id(1)

    @pl.when(t == 0)
    def _():
        carry_c[...] = jnp.zeros_like(carry_c)
        carry_u[...] = umeta_ref[...]

    hn = _layer_norm(x_ref[0], g_ref[...], b_ref[...])
    hb = hn.astype(jnp.bfloat16)

    qT = lax.dot_general(wqT_ref[...], hb, _NT, preferred_element_type=jnp.float32) * Q_SCALE
    qT_ref[0, 0] = qT.astype(jnp.bfloat16)

    vT = lax.dot_general(wvT_ref[...], hb, _NT, preferred_element_type=jnp.float32)
    vTb = vT.astype(jnp.bfloat16)
    for j in range(TOKEN_TILE // ATT_TILE):
        vT_ref[0, j] = vTb[:, j * ATT_TILE:(j + 1) * ATT_TILE]

    gT = lax.dot_general(wgT_ref[...], hb, _NT, preferred_element_type=jnp.float32)
    sgT_ref[0, 0] = _silu(gT).astype(jnp.bfloat16)

    flT = lax.dot_general(wfT_ref[...], hb, _NT, preferred_element_type=jnp.float32)
    logf = _log_sigmoid(flT[:HEADS, :] + bf_ref[:HEADS, :])
    cT = _lane_cumsum(logf) + carry_c[:, 0:1]
    carry_c[...] = jnp.broadcast_to(cT[:, TOKEN_TILE - 1:TOKEN_TILE], carry_c.shape)
    cT_pad = jnp.concatenate(
        [cT * LOG2E, jnp.zeros((LANES - HEADS, TOKEN_TILE), jnp.float32)], axis=0)
    c_rows = cT_pad.T
    aug_ref[0] = _bias_columns(c_rows)
    kk = jnp.dot(hb, wk_ref[...], preferred_element_type=jnp.float32)
    k_ref[0] = kk.astype(jnp.bfloat16)

    u = jnp.dot(hb, wu_ref[...], preferred_element_type=jnp.float32)
    gp = jnp.dot(hb, wgp_ref[...], preferred_element_type=jnp.float32)
    u_ext = jnp.concatenate([carry_u[...], u], axis=0)
    carry_u[...] = u[TOKEN_TILE - N_META:, :]
    p = _pool_mix(u_ext, u, wpool_ref, pscale_ref[...], _silu(gp))
    p_ref[0] = p.astype(jnp.bfloat16)


def _attn_kernel(qT_ref, k_ref, aug_ref, vT_ref, sgT_ref, kmeta_ref, augmeta_ref, vTmeta_ref, o_ref,
                 qa_ref, s_ref, smeta_ref, acc_ref):
    n_qb = qT_ref.shape[1]
    heads = range(HEAD_GROUP)
    sel_row = lax.broadcasted_iota(jnp.int32, (LANES, Q_TILE), 0)
    for g in heads:
        h = pl.program_id(1) * HEAD_GROUP + g
        picks = (sel_row == h) | (sel_row == HEADS + h) | (sel_row == 2 * HEADS + h)
        for par in range(2):
            qa_ref[par, g, :PAIR_WIDTH, :] = jnp.zeros((PAIR_WIDTH, Q_TILE), jnp.bfloat16)
            qa_ref[par, g, PAIR_WIDTH:, :] = jnp.where(picks, -1.0, 0.0).astype(jnp.bfloat16)
    ones = jnp.ones((ONES_ROWS, ATT_TILE), jnp.bfloat16)
    ones_meta = jnp.ones((ONES_ROWS, META_PAD), jnp.bfloat16)
    krow = lax.broadcasted_iota(jnp.int32, (ATT_TILE, Q_TILE), 0)
    qcol = lax.broadcasted_iota(jnp.int32, (ATT_TILE, Q_TILE), 1)
    causal_a = krow <= qcol
    causal_b = causal_a[:, :ATT_TILE]
    hidden_b = jnp.full((ATT_TILE, Q_TILE - ATT_TILE), -jnp.inf, jnp.float32)
    acc_zero = jnp.zeros(acc_ref.shape[2:], jnp.float32)

    def hrows(g, width):
        return slice(g * width, (g + 1) * width)

    def colmax(s):
        return jnp.max(s, axis=0, keepdims=True)

    def key_tile(g, kj):
        rows = pl.ds(pl.multiple_of(kj * ATT_TILE, ATT_TILE), ATT_TILE)
        return jnp.concatenate([k_ref[0, rows, hrows(g // 2, PAIR_WIDTH)], aug_ref[0, rows, :]], axis=1)

    def value_tile(g, kj):
        return jnp.concatenate([vT_ref[0, kj, hrows(g, HEAD_DIM), :], ones], axis=0)

    def head_scores(g, kj, qa, dst):
        s = jnp.dot(key_tile(g, kj), qa[g], preferred_element_type=jnp.float32)
        dst[g] = s
        return colmax(s)

    def head_consume(g, kj, src, acc, smax, m):
        m_new = jnp.maximum(m, smax)
        alpha = jnp.exp2(m - m_new)
        p = jnp.exp2(src[g] - m_new).astype(jnp.bfloat16)
        acc[g] = alpha * acc[g] + jnp.dot(value_tile(g, kj), p, preferred_element_type=jnp.float32)
        return m_new

    def start_block(g, qi, qa, smeta, dst):
        qa[g, hrows(g % 2, HEAD_DIM), :] = qT_ref[0, qi, hrows(g, HEAD_DIM), :]
        kmeta = jnp.concatenate([kmeta_ref[:, hrows(g // 2, PAIR_WIDTH)], augmeta_ref[...]], axis=1)
        smeta[g] = jnp.dot(kmeta, qa[g], preferred_element_type=jnp.float32)
        return head_scores(g, 0, qa, dst)

    def block(qi, par, smax_first):
        qa, qa_nx = qa_ref.at[par], qa_ref.at[1 - par]
        smeta, smeta_nx = smeta_ref.at[par], smeta_ref.at[1 - par]
        acc, acc_nx = acc_ref.at[par], acc_ref.at[1 - par]
        s_first, s_other, s_nx = s_ref.at[2 * par], s_ref.at[1], s_ref.at[2 - 2 * par]
        m0 = tuple(jnp.full((1, Q_TILE), -jnp.inf, jnp.float32) for _ in heads)

        def step(k_next, dst, k_cur, src, smax_cur, ms):
            smax_next, ms_new = [], []
            for g in heads:
                smax_next.append(head_scores(g, k_next, qa, dst))
                ms_new.append(head_consume(g, k_cur, src, acc, smax_cur[g], ms[g]))
            return tuple(ms_new), tuple(smax_next)

        def pair(t, ms, smax):
            ms, smax_o = step(t + 1, s_other, t, s_first, smax, ms)
            return step(t + 2, s_first, t + 1, s_other, smax_o, ms)

        def quad(k, carry):
            return pair(4 * k + 2, *pair(4 * k, *carry))

        ms, smax = lax.fori_loop(0, qi // 2, quad, (m0, smax_first))
        if par == 1:
            ms, _ = pair(2 * qi - 2, ms, smax)

        q_next = jnp.minimum(qi + 1, n_qb - 1)
        smax_next = []
        for g in heads:
            sb = jnp.dot(key_tile(g, 2 * qi + 1), qa[g][:, ATT_TILE:],
                         preferred_element_type=jnp.float32)
            sb = jnp.concatenate([hidden_b, jnp.where(causal_b, sb, -jnp.inf)], axis=1)
            s = jnp.where(causal_a, s_first[g], -jnp.inf)
            sm = smeta[g]
            m_new = jnp.maximum(jnp.maximum(ms[g], colmax(s)), jnp.maximum(colmax(sb), colmax(sm)))
            alpha = jnp.exp2(ms[g] - m_new)
            p = jnp.exp2(s - m_new).astype(jnp.bfloat16)
            pb = jnp.exp2(sb - m_new).astype(jnp.bfloat16)
            pm = jnp.exp2(sm - m_new).astype(jnp.bfloat16)
            smax_next.append(start_block(g, q_next, qa_nx, smeta_nx, s_nx))
            acc_nx[g] = acc_zero
            vmeta = jnp.concatenate([vTmeta_ref[hrows(g, HEAD_DIM), :], ones_meta], axis=0)
            a = (alpha * acc[g]
                 + jnp.dot(value_tile(g, 2 * qi), p, preferred_element_type=jnp.float32)
                 + jnp.dot(value_tile(g, 2 * qi + 1), pb, preferred_element_type=jnp.float32)
                 + jnp.dot(vmeta, pm, preferred_element_type=jnp.float32))
            o = a[:HEAD_DIM, :] / a[HEAD_DIM:HEAD_DIM + 1, :]
            o = o * sgT_ref[0, qi, hrows(g, HEAD_DIM), :].astype(jnp.float32)
            o_ref[0, qi, hrows(g, HEAD_DIM), :] = o.astype(jnp.bfloat16)
        return tuple(smax_next)

    def two_blocks(j, smax_first):
        return block(2 * j + 1, 1, block(2 * j, 0, smax_first))

    smax_first = []
    for g in heads:
        acc_ref[0, g] = acc_zero
        smax_first.append(start_block(g, 0, qa_ref.at[0], smeta_ref.at[0], s_ref.at[0]))
    lax.fori_loop(0, n_qb // 2, two_blocks, tuple(smax_first))


def _out_proj_kernel(x_ref, gin_ref, bin_ref, aT_ref, p_ref, waT_ref, wp_ref, g_ref, b_ref, o_ref):
    for j in range(TOKEN_TILE // ATT_TILE):
        rows = slice(j * ATT_TILE, (j + 1) * ATT_TILE)
        hn = _layer_norm(x_ref[0, rows, :], gin_ref[...], bin_ref[...])
        yT = jnp.dot(waT_ref[...], aT_ref[0, 0, :, rows], preferred_element_type=jnp.float32)
        y = yT.T + jnp.dot(p_ref[0, rows, :], wp_ref[...], preferred_element_type=jnp.float32)
        o_ref[0, rows, :] = _layer_norm(DEEPNORM_ALPHA * hn + y, g_ref[...], b_ref[...])


def _const_spec(shape):
    nd = len(shape)
    return pl.BlockSpec(shape, lambda *_: (0,) * nd, pipeline_mode=pl.Buffered(1))


def kernel(x, meta_tokens, ln_in_g, ln_in_b, w_in, b_forget, w_pool, pool_scale, w_out, ln_g, ln_b):
    B, L, D = x.shape
    assert D == D_MODEL and L % (2 * Q_TILE) == 0 and TOKEN_TILE == Q_TILE and w_in.shape[0] == 1
    bf16, f32 = jnp.bfloat16, jnp.float32
    w = w_in[0]
    o_k, o_v, o_f = ATT_WIDTH, 2 * ATT_WIDTH, 3 * ATT_WIDTH
    o_g = o_f + HEADS
    o_u = o_g + ATT_WIDTH
    o_gp = o_u + D_MODEL

    wqT = w[:, :o_k].T.astype(bf16)
    wk = w[:, o_k:o_v].astype(bf16)
    wvT = w[:, o_v:o_f].T.astype(bf16)
    wfT = jnp.pad(w[:, o_f:o_g].T, ((0, LANES - HEADS), (0, 0))).astype(bf16)
    bf_col = jnp.pad(b_forget[0].astype(f32), (0, LANES - HEADS)).reshape(LANES, 1)
    wgT = w[:, o_g:o_u].T.astype(bf16)
    wu = w[:, o_u:o_gp].astype(bf16)
    wgp = w[:, o_gp:].astype(bf16)
    wpool = w_pool[0].astype(bf16)
    pscale = pool_scale[0].reshape(1, D).astype(f32)
    waT = w_out[0, :ATT_WIDTH].T.astype(bf16)
    wp = w_out[0, ATT_WIDTH:].astype(bf16)
    g_in = ln_in_g.reshape(1, D).astype(f32)
    b_in = ln_in_b.reshape(1, D).astype(f32)
    g_out = ln_g[0].reshape(1, D).astype(f32)
    b_out = ln_b[0].reshape(1, D).astype(f32)
    mt_pad = jnp.pad(meta_tokens.astype(f32), ((0, META_PAD - N_META), (0, 0)))

    kmeta, augmeta, vTmeta, umeta = pl.pallas_call(
        _meta_kernel,
        out_shape=(jax.ShapeDtypeStruct((META_PAD, ATT_WIDTH), bf16),
                   jax.ShapeDtypeStruct((META_PAD, LANES), bf16),
                   jax.ShapeDtypeStruct((ATT_WIDTH, META_PAD), bf16),
                   jax.ShapeDtypeStruct((N_META, D), f32)),
        compiler_params=pltpu.CompilerParams(vmem_limit_bytes=VMEM_LIMIT),
        name="meta_proj",
    )(mt_pad, g_in, b_in, wk, wvT, wfT, bf_col, wu)

    n_t = L // TOKEN_TILE
    n_sub = TOKEN_TILE // ATT_TILE
    n_ab = L // ATT_TILE
    ktiled = jax.ShapeDtypeStruct((B, n_ab, ATT_WIDTH, ATT_TILE), bf16)
    ktiled_spec = pl.BlockSpec((1, n_sub, ATT_WIDTH, ATT_TILE), lambda b, t: (b, t, 0, 0))
    qtiled = jax.ShapeDtypeStruct((B, n_t, ATT_WIDTH, Q_TILE), bf16)
    qtiled_spec = pl.BlockSpec((1, 1, ATT_WIDTH, Q_TILE), lambda b, t: (b, t, 0, 0))
    qT, kk, aug, vT, sgT, pmix = pl.pallas_call(
        _in_proj_kernel,
        grid=(B, n_t),
        in_specs=[
            pl.BlockSpec((1, TOKEN_TILE, D), lambda b, t: (b, t, 0)),
            _const_spec((1, D)), _const_spec((1, D)),
            _const_spec((ATT_WIDTH, D)), _const_spec((D, ATT_WIDTH)), _const_spec((ATT_WIDTH, D)),
            _const_spec((LANES, D)), _const_spec((LANES, 1)), _const_spec((ATT_WIDTH, D)),
            _const_spec((D, D)), _const_spec((D, D)),
            _const_spec((len(POOL_WINDOWS), POOL_GROUP_WIDTH, POOL_GROUP_WIDTH)),
            _const_spec((1, D)), _const_spec((N_META, D)),
        ],
        out_specs=(qtiled_spec,
                   pl.BlockSpec((1, TOKEN_TILE, ATT_WIDTH), lambda b, t: (b, t, 0)),
                   pl.BlockSpec((1, TOKEN_TILE, LANES), lambda b, t: (b, t, 0)),
                   ktiled_spec, qtiled_spec,
                   pl.BlockSpec((1, TOKEN_TILE, D), lambda b, t: (b, t, 0))),
        out_shape=(qtiled, jax.ShapeDtypeStruct((B, L, ATT_WIDTH), bf16),
                   jax.ShapeDtypeStruct((B, L, LANES), bf16), ktiled, qtiled,
                   jax.ShapeDtypeStruct((B, L, D), bf16)),
        scratch_shapes=[pltpu.VMEM((HEADS, LANES), f32), pltpu.VMEM((N_META, D), f32)],
        compiler_params=pltpu.CompilerParams(
            dimension_semantics=("arbitrary", "arbitrary"), vmem_limit_bytes=VMEM_LIMIT),
        name="in_proj",
    )(x, g_in, b_in, wqT, wk, wvT, wfT, bf_col, wgT, wu, wgp, wpool, pscale, umeta)

    gd = HEAD_GROUP * HEAD_DIM
    head_ktiled = pl.BlockSpec((1, n_ab, gd, ATT_TILE), lambda b, h: (b, 0, h, 0))
    head_qtiled = pl.BlockSpec((1, n_t, gd, Q_TILE), lambda b, h: (b, 0, h, 0))
    aT = pl.pallas_call(
        _attn_kernel,
        grid=(B, HEADS // HEAD_GROUP),
        in_specs=[
            head_qtiled,
            pl.BlockSpec((1, L, gd), lambda b, h: (b, 0, h)),
            pl.BlockSpec((1, L, LANES), lambda b, h: (b, 0, 0)),
            head_ktiled, head_qtiled,
            pl.BlockSpec((META_PAD, gd), lambda b, h: (0, h)),
            pl.BlockSpec((META_PAD, LANES), lambda b, h: (0, 0)),
            pl.BlockSpec((gd, META_PAD), lambda b, h: (h, 0)),
        ],
        out_specs=head_qtiled,
        out_shape=qtiled,
        scratch_shapes=[pltpu.VMEM((2, HEAD_GROUP, KEY_WIDTH, Q_TILE), bf16),
                        pltpu.VMEM((3, HEAD_GROUP, ATT_TILE, Q_TILE), f32),
                        pltpu.VMEM((2, HEAD_GROUP, META_PAD, Q_TILE), f32),
                        pltpu.VMEM((2, HEAD_GROUP, HEAD_DIM + ONES_ROWS, Q_TILE), f32)],
        compiler_params=pltpu.CompilerParams(
            dimension_semantics=("arbitrary", "arbitrary"), vmem_limit_bytes=VMEM_LIMIT),
        name="fox_attn",
    )(qT, kk, aug, vT, sgT, kmeta, augmeta, vTmeta)

    out = pl.pallas_call(
        _out_proj_kernel,
        grid=(B, n_t),
        in_specs=[
            pl.BlockSpec((1, TOKEN_TILE, D), lambda b, t: (b, t, 0)),
            _const_spec((1, D)), _const_spec((1, D)),
            qtiled_spec,
            pl.BlockSpec((1, TOKEN_TILE, D), lambda b, t: (b, t, 0)),
            _const_spec((D, ATT_WIDTH)), _const_spec((D, D)),
            _const_spec((1, D)), _const_spec((1, D)),
        ],
        out_specs=pl.BlockSpec((1, TOKEN_TILE, D), lambda b, t: (b, t, 0)),
        out_shape=jax.ShapeDtypeStruct((B, L, D), x.dtype),
        compiler_params=pltpu.CompilerParams(
            dimension_semantics=("arbitrary", "arbitrary"), vmem_limit_bytes=VMEM_LIMIT),
        name="out_proj",
    )(x, g_in, b_in, aT, pmix, waT, wp, g_out, b_out)
    return out
```

```python
import math

import jax
import jax.numpy as jnp
from jax import lax
from jax.experimental import pallas as pl
from jax.experimental.pallas import tpu as pltpu

D_MODEL = 1024
N_META = 16
HEADS = 16
HEAD_DIM = 64
ATT_WIDTH = HEADS * HEAD_DIM
POOL_WINDOWS = (2, 4, 8, 16)
POOL_GROUP_WIDTH = D_MODEL // len(POOL_WINDOWS)
LN_EPS = 1e-5
DEEPNORM_ALPHA = 2.0 ** 0.25
LOG2E = math.log2(math.e)
Q_SCALE = HEAD_DIM ** -0.5 * LOG2E

LANES = 128
PAIR_WIDTH = 2 * HEAD_DIM
KEY_WIDTH = PAIR_WIDTH + LANES
META_PAD = 128
MASKED_BIAS = 1e30
TOKEN_TILE = 512
PREP_ROWS = 256
ATT_TILE = 256
Q_TILE = 2 * ATT_TILE
ONES_ROWS = 16
HEAD_GROUP = 4
VMEM_LIMIT = 56 * 1024 * 1024

_NT = (((1,), (1,)), ((), ()))


def _layer_norm(x, g, b):
    mu = jnp.mean(x, axis=-1, keepdims=True)
    xc = x - mu
    var = jnp.mean(xc * xc, axis=-1, keepdims=True)
    return xc * lax.rsqrt(var + LN_EPS) * g + b


def _log_sigmoid(z):
    return jnp.minimum(z, 0.0) - jnp.log(1.0 + jnp.exp(-jnp.abs(z)))


def _silu(z):
    return z / (1.0 + jnp.exp(-z))


def _lane_cumsum(x):
    n = x.shape[-1]
    lane = lax.broadcasted_iota(jnp.int32, x.shape, x.ndim - 1)
    d = 1
    while d < n:
        x = x + jnp.where(lane >= d, pltpu.roll(x, d, x.ndim - 1), 0.0)
        d *= 2
    return x


def _bias_columns(c_rows):
    hi = c_rows.astype(jnp.bfloat16).astype(jnp.float32)
    r1 = c_rows - hi
    mid = r1.astype(jnp.bfloat16).astype(jnp.float32)
    lo = (r1 - mid).astype(jnp.bfloat16).astype(jnp.float32)
    packed = hi + pltpu.roll(mid, HEADS, 1) + pltpu.roll(lo, 2 * HEADS, 1)
    return packed.astype(jnp.bfloat16)


def _pool_mix(u_ext, u, wpool_ref, pscale, gate):
    outs = []
    for gi, w in enumerate(POOL_WINDOWS):
        sl = slice(gi * POOL_GROUP_WIDTH, (gi + 1) * POOL_GROUP_WIDTH)
        r = u_ext[:, sl]
        s = 1
        while s < w:
            r = r + pltpu.roll(r, s, 0)
            s *= 2
        d = r[N_META:, :] * (1.0 / w) - u[:, sl]
        outs.append(jnp.dot(d.astype(jnp.bfloat16), wpool_ref[gi],
                            preferred_element_type=jnp.float32))
    y = jnp.concatenate(outs, axis=-1) * pscale
    return y * gate


def _meta_kernel(mt_ref, g_ref, b_ref, wk_ref, wvT_ref, wfT_ref, bf_ref, wu_ref,
                 kmeta_ref, augmeta_ref, vTmeta_ref, umeta_ref):
    hn = _layer_norm(mt_ref[...], g_ref[...], b_ref[...])
    hb = hn.astype(jnp.bfloat16)
    flT = lax.dot_general(wfT_ref[...], hb, _NT, preferred_element_type=jnp.float32)
    row = lax.broadcasted_iota(jnp.int32, flT.shape, 0)
    col = lax.broadcasted_iota(jnp.int32, flT.shape, 1)
    valid = (row < HEADS) & (col < N_META)
    logf = jnp.where(valid, _log_sigmoid(flT + bf_ref[...]), 0.0)
    cT = _lane_cumsum(logf)
    total = cT[:, META_PAD - 1:META_PAD]
    c_rows = ((cT - total) * LOG2E).T
    trow = lax.broadcasted_iota(jnp.int32, c_rows.shape, 0)
    tcol = lax.broadcasted_iota(jnp.int32, c_rows.shape, 1)
    c_rows = jnp.where((trow >= N_META) & (tcol < HEADS), MASKED_BIAS, c_rows)
    augmeta_ref[...] = _bias_columns(c_rows)
    kk = jnp.dot(hb, wk_ref[...], preferred_element_type=jnp.float32)
    kmeta_ref[...] = kk.astype(jnp.bfloat16)
    vT = lax.dot_general(wvT_ref[...], hb, _NT, preferred_element_type=jnp.float32)
    vTmeta_ref[...] = vT.astype(jnp.bfloat16)
    u = jnp.dot(hb, wu_ref[...], preferred_element_type=jnp.float32)
    umeta_ref[...] = u[:N_META, :]


def _in_proj_kernel(x_ref, g_ref, b_ref, wqT_ref, wk_ref, wvT_ref, wfT_ref, bf_ref, wgT_ref,
                    wu_ref, wgp_ref, wpool_ref, pscale_ref, umeta_ref,
                    qT_ref, k_ref, aug_ref, vT_ref, sgT_ref, p_ref, carry_c, carry_u):
    t = pl.program_id(1)

    @pl.when(t == 0)
    def _():
        carry_c[...] = jnp.zeros_like(carry_c)
        carry_u[...] = umeta_ref[...]

    hn = _layer_norm(x_ref[0], g_ref[...], b_ref[...])
    hb = hn.astype(jnp.bfloat16)

    qT = lax.dot_general(wqT_ref[...], hb, _NT, preferred_element_type=jnp.float32) * Q_SCALE
    qT_ref[0, 0] = qT.astype(jnp.bfloat16)

    vT = lax.dot_general(wvT_ref[...], hb, _NT, preferred_element_type=jnp.float32)
    vTb = vT.astype(jnp.bfloat16)
    for j in range(TOKEN_TILE // ATT_TILE):
        vT_ref[0, j] = vTb[:, j * ATT_TILE:(j + 1) * ATT_TILE]

    gT = lax.dot_general(wgT_ref[...], hb, _NT, preferred_element_type=jnp.float32)
    sgT_ref[0, 0] = _silu(gT).astype(jnp.bfloat16)

    flT = lax.dot_general(wfT_ref[...], hb, _NT, preferred_element_type=jnp.float32)
    logf = _log_sigmoid(flT[:HEADS, :] + bf_ref[:HEADS, :])
    cT = _lane_cumsum(logf) + carry_c[:, 0:1]
    carry_c[...] = jnp.broadcast_to(cT[:, TOKEN_TILE - 1:TOKEN_TILE], carry_c.shape)
    cT_pad = jnp.concatenate(
        [cT * LOG2E, jnp.zeros((LANES - HEADS, TOKEN_TILE), jnp.float32)], axis=0)
    c_rows = cT_pad.T
    aug_ref[0] = _bias_columns(c_rows)
    kk = jnp.dot(hb, wk_ref[...], preferred_element_type=jnp.float32)
    k_ref[0] = kk.astype(jnp.bfloat16)

    u = jnp.dot(hb, wu_ref[...], preferred_element_type=jnp.float32)
    gp = jnp.dot(hb, wgp_ref[...], preferred_element_type=jnp.float32)
    u_ext = jnp.concatenate([carry_u[...], u], axis=0)
    carry_u[...] = u[TOKEN_TILE - N_META:, :]
    p = _pool_mix(u_ext, u, wpool_ref, pscale_ref[...], _silu(gp))
    p_ref[0] = p.astype(jnp.bfloat16)


def _attn_kernel(qT_ref, k_ref, aug_ref, vT_ref, sgT_ref, kmeta_ref, augmeta_ref, vTmeta_ref, o_ref,
                 qa_ref, s_ref, smeta_ref, acc_ref):
    n_qb = qT_ref.shape[1]
    heads = range(HEAD_GROUP)
    sel_row = lax.broadcasted_iota(jnp.int32, (LANES, Q_TILE), 0)
    for g in heads:
        h = pl.program_id(1) * HEAD_GROUP + g
        picks = (sel_row == h) | (sel_row == HEADS + h) | (sel_row == 2 * HEADS + h)
        for par in range(2):
            qa_ref[par, g, :PAIR_WIDTH, :] = jnp.zeros((PAIR_WIDTH, Q_TILE), jnp.bfloat16)
            qa_ref[par, g, PAIR_WIDTH:, :] = jnp.where(picks, -1.0, 0.0).astype(jnp.bfloat16)
    ones = jnp.ones((ONES_ROWS, ATT_TILE), jnp.bfloat16)
    ones_meta = jnp.ones((ONES_ROWS, META_PAD), jnp.bfloat16)
    krow = lax.broadcasted_iota(jnp.int32, (ATT_TILE, Q_TILE), 0)
    qcol = lax.broadcasted_iota(jnp.int32, (ATT_TILE, Q_TILE), 1)
    causal_a = krow <= qcol
    causal_b = causal_a[:, :ATT_TILE]
    meta_fill = jnp.zeros((META_PAD - N_META, Q_TILE), jnp.bfloat16)
    acc_zero = jnp.zeros(acc_ref.shape[2:], jnp.float32)

    def hrows(g, width):
        return slice(g * width, (g + 1) * width)

    def colmax(s):
        return jnp.max(s, axis=0, keepdims=True)

    def key_tile(g, kj):
        rows = pl.ds(pl.multiple_of(kj * ATT_TILE, ATT_TILE), ATT_TILE)
        return jnp.concatenate([k_ref[0, rows, hrows(g // 2, PAIR_WIDTH)], aug_ref[0, rows, :]], axis=1)

    def value_tile(g, kj):
        return jnp.concatenate([vT_ref[0, kj, hrows(g, HEAD_DIM), :], ones], axis=0)

    def head_scores(g, kj, qa, dst):
        s = jnp.dot(key_tile(g, kj), qa[g], preferred_element_type=jnp.float32)
        dst[g] = s
        return colmax(s)

    def head_consume(g, kj, src, acc, smax, m):
        m_new = jnp.maximum(m, smax)
        alpha = jnp.exp2(m - m_new)
        p = jnp.exp2(src[g] - m_new).astype(jnp.bfloat16)
        acc[g] = alpha * acc[g] + jnp.dot(value_tile(g, kj), p, preferred_element_type=jnp.float32)
        return m_new

    def start_block(g, qi, qa, smeta, dst):
        qa[g, hrows(g % 2, HEAD_DIM), :] = qT_ref[0, qi, hrows(g, HEAD_DIM), :]
        kmeta = jnp.concatenate([kmeta_ref[:N_META, hrows(g // 2, PAIR_WIDTH)],
                                 augmeta_ref[:N_META, :]], axis=1)
        smeta[g] = jnp.dot(kmeta, qa[g], preferred_element_type=jnp.float32)
        return head_scores(g, 0, qa, dst)

    def block(qi, par, smax_first):
        qa, qa_nx = qa_ref.at[par], qa_ref.at[1 - par]
        smeta, smeta_nx = smeta_ref.at[par], smeta_ref.at[1 - par]
        acc, acc_nx = acc_ref.at[par], acc_ref.at[1 - par]
        s_first, s_other, s_nx = s_ref.at[2 * par], s_ref.at[1], s_ref.at[2 - 2 * par]
        m0 = tuple(jnp.full((1, Q_TILE), -jnp.inf, jnp.float32) for _ in heads)

        def step(k_next, dst, k_cur, src, smax_cur, ms):
            smax_next, ms_new = [], []
            for g in heads:
                smax_next.append(head_scores(g, k_next, qa, dst))
                ms_new.append(head_consume(g, k_cur, src, acc, smax_cur[g], ms[g]))
            return tuple(ms_new), tuple(smax_next)

        def pair(t, ms, smax):
            ms, smax_o = step(t + 1, s_other, t, s_first, smax, ms)
            return step(t + 2, s_first, t + 1, s_other, smax_o, ms)

        def quad(k, carry):
            return pair(4 * k + 2, *pair(4 * k, *carry))

        ms, smax = lax.fori_loop(0, qi // 2, quad, (m0, smax_first))
        if par == 1:
            ms, _ = pair(2 * qi - 2, ms, smax)

        q_next = jnp.minimum(qi + 1, n_qb - 1)
        smax_next = []
        for g in heads:
            sb = jnp.dot(key_tile(g, 2 * qi + 1), qa[g][:, ATT_TILE:],
                         preferred_element_type=jnp.float32)
            sb = jnp.where(causal_b, sb, -jnp.inf)
            s = jnp.where(causal_a, s_first[g], -jnp.inf)
            sm = smeta[g]
            m_new = jnp.maximum(jnp.maximum(ms[g], colmax(s)), colmax(sm))
            m_new = jnp.concatenate(
                [m_new[:, :ATT_TILE], jnp.maximum(m_new[:, ATT_TILE:], colmax(sb))], axis=1)
            alpha = jnp.exp2(ms[g] - m_new)
            p = jnp.exp2(s - m_new).astype(jnp.bfloat16)
            pb = jnp.exp2(sb - m_new[:, ATT_TILE:]).astype(jnp.bfloat16)
            pm = jnp.concatenate([jnp.exp2(sm - m_new).astype(jnp.bfloat16), meta_fill], axis=0)
            smax_next.append(start_block(g, q_next, qa_nx, smeta_nx, s_nx))
            acc_nx[g] = acc_zero
            vmeta = jnp.concatenate([vTmeta_ref[hrows(g, HEAD_DIM), :], ones_meta], axis=0)
            a = (alpha * acc[g]
                 + jnp.dot(value_tile(g, 2 * qi), p, preferred_element_type=jnp.float32)
                 + jnp.dot(vmeta, pm, preferred_element_type=jnp.float32))
            a_late = a[:, ATT_TILE:] + jnp.dot(value_tile(g, 2 * qi + 1), pb,
                                               preferred_element_type=jnp.float32)
            a = jnp.concatenate([a[:, :ATT_TILE], a_late], axis=1)
            o = a[:HEAD_DIM, :] / a[HEAD_DIM:HEAD_DIM + 1, :]
            o = o * sgT_ref[0, qi, hrows(g, HEAD_DIM), :].astype(jnp.float32)
            o_ref[0, qi, hrows(g, HEAD_DIM), :] = o.astype(jnp.bfloat16)
        return tuple(smax_next)

    def two_blocks(j, smax_first):
        return block(2 * j + 1, 1, block(2 * j, 0, smax_first))

    smax_first = []
    for g in heads:
        acc_ref[0, g] = acc_zero
        smax_first.append(start_block(g, 0, qa_ref.at[0], smeta_ref.at[0], s_ref.at[0]))
    lax.fori_loop(0, n_qb // 2, two_blocks, tuple(smax_first))


def _out_proj_kernel(x_ref, gin_ref, bin_ref, aT_ref, p_ref, waT_ref, wp_ref, g_ref, b_ref, o_ref):
    for j in range(TOKEN_TILE // ATT_TILE):
        rows = slice(j * ATT_TILE, (j + 1) * ATT_TILE)
        hn = _layer_norm(x_ref[0, rows, :], gin_ref[...], bin_ref[...])
        yT = jnp.dot(waT_ref[...], aT_ref[0, 0, :, rows], preferred_element_type=jnp.float32)
        y = yT.T + jnp.dot(p_ref[0, rows, :], wp_ref[...], preferred_element_type=jnp.float32)
        o_ref[0, rows, :] = _layer_norm(DEEPNORM_ALPHA * hn + y, g_ref[...], b_ref[...])


def _prep_kernel(c0_ref, c1_ref, c2_ref, c3_ref, c4_ref, c5_ref, c6_ref, wa_ref, wpin_ref,
                 wqT_ref, wk_ref, wvT_ref, wfT_ref, wgT_ref, wu_ref, wgp_ref, waT_ref, wp_ref):
    bf16 = jnp.bfloat16
    wqT_ref[...] = c0_ref[0].T.astype(bf16)
    wk_ref[...] = c1_ref[0].astype(bf16)
    wvT_ref[...] = c2_ref[0].T.astype(bf16)
    x3, x4, x5, x6 = c3_ref[0], c4_ref[0], c5_ref[0], c6_ref[0]
    fT = x3[:, :LANES].T
    head_row = lax.broadcasted_iota(jnp.int32, fT.shape, 0)
    wfT_ref[...] = jnp.where(head_row < HEADS, fT, 0.0).astype(bf16)

    def realign(lo, hi):
        wide = jnp.concatenate([lo, hi[:, :LANES]], axis=1)
        return pltpu.roll(wide, wide.shape[1] - HEADS, 1)[:, :D_MODEL]

    wgT_ref[...] = realign(x3, x4).T.astype(bf16)
    wu_ref[...] = realign(x4, x5).astype(bf16)
    wgp_ref[...] = realign(x5, x6).astype(bf16)
    waT_ref[...] = wa_ref[0].T.astype(bf16)
    wp_ref[...] = wpin_ref[0].astype(bf16)


def _const_spec(shape):
    nd = len(shape)
    return pl.BlockSpec(shape, lambda *_: (0,) * nd, pipeline_mode=pl.Buffered(1))


def kernel(x, meta_tokens, ln_in_g, ln_in_b, w_in, b_forget, w_pool, pool_scale, w_out, ln_g, ln_b):
    B, L, D = x.shape
    assert D == D_MODEL and L % (2 * Q_TILE) == 0 and TOKEN_TILE == Q_TILE and w_in.shape[0] == 1
    bf16, f32 = jnp.bfloat16, jnp.float32
    assert w_in.shape[2] == 4 * ATT_WIDTH + HEADS + 2 * D_MODEL and w_out.shape[1] == 2 * D_MODEL

    col_block = lambda c: pl.BlockSpec((1, PREP_ROWS, D), lambda r: (0, r, c))
    rows_out = pl.BlockSpec((PREP_ROWS, D), lambda r: (r, 0))
    cols_out = lambda n: pl.BlockSpec((n, PREP_ROWS), lambda r: (0, r))
    square = jax.ShapeDtypeStruct((D, D), bf16)
    wqT, wk, wvT, wfT, wgT, wu, wgp, waT, wp = pl.pallas_call(
        _prep_kernel,
        grid=(D // PREP_ROWS,),
        in_specs=[col_block(c) for c in range(6)]
        + [pl.BlockSpec((1, PREP_ROWS, LANES), lambda r: (0, r, 6 * D // LANES)),
           pl.BlockSpec((1, PREP_ROWS, D), lambda r: (0, r, 0)),
           pl.BlockSpec((1, PREP_ROWS, D), lambda r: (0, r + D // PREP_ROWS, 0))],
        out_specs=(cols_out(D), rows_out, cols_out(D), cols_out(LANES), cols_out(D),
                   rows_out, rows_out, cols_out(D), rows_out),
        out_shape=(square, square, square, jax.ShapeDtypeStruct((LANES, D), bf16), square,
                   square, square, square, square),
        compiler_params=pltpu.CompilerParams(
            dimension_semantics=("arbitrary",), vmem_limit_bytes=VMEM_LIMIT),
        name="weight_prep",
    )(w_in, w_in, w_in, w_in, w_in, w_in, w_in, w_out, w_out)
    bf_col = jnp.pad(b_forget[0].astype(f32), (0, LANES - HEADS)).reshape(LANES, 1)
    wpool = w_pool[0].astype(bf16)
    pscale = pool_scale[0].reshape(1, D).astype(f32)
    g_in = ln_in_g.reshape(1, D).astype(f32)
    b_in = ln_in_b.reshape(1, D).astype(f32)
    g_out = ln_g[0].reshape(1, D).astype(f32)
    b_out = ln_b[0].reshape(1, D).astype(f32)
    mt_pad = jnp.pad(meta_tokens.astype(f32), ((0, META_PAD - N_META), (0, 0)))

    kmeta, augmeta, vTmeta, umeta = pl.pallas_call(
        _meta_kernel,
        out_shape=(jax.ShapeDtypeStruct((META_PAD, ATT_WIDTH), bf16),
                   jax.ShapeDtypeStruct((META_PAD, LANES), bf16),
                   jax.ShapeDtypeStruct((ATT_WIDTH, META_PAD), bf16),
                   jax.ShapeDtypeStruct((N_META, D), f32)),
        compiler_params=pltpu.CompilerParams(vmem_limit_bytes=VMEM_LIMIT),
        name="meta_proj",
    )(mt_pad, g_in, b_in, wk, wvT, wfT, bf_col, wu)

    n_t = L // TOKEN_TILE
    n_sub = TOKEN_TILE // ATT_TILE
    n_ab = L // ATT_TILE
    ktiled = jax.ShapeDtypeStruct((B, n_ab, ATT_WIDTH, ATT_TILE), bf16)
    ktiled_spec = pl.BlockSpec((1, n_sub, ATT_WIDTH, ATT_TILE), lambda b, t: (b, t, 0, 0))
    qtiled = jax.ShapeDtypeStruct((B, n_t, ATT_WIDTH, Q_TILE), bf16)
    qtiled_spec = pl.BlockSpec((1, 1, ATT_WIDTH, Q_TILE), lambda b, t: (b, t, 0, 0))
    qT, kk, aug, vT, sgT, pmix = pl.pallas_call(
        _in_proj_kernel,
        grid=(B, n_t),
        in_specs=[
            pl.BlockSpec((1, TOKEN_TILE, D), lambda b, t: (b, t, 0)),
            _const_spec((1, D)), _const_spec((1, D)),
            _const_spec((ATT_WIDTH, D)), _const_spec((D, ATT_WIDTH)), _const_spec((ATT_WIDTH, D)),
            _const_spec((LANES, D)), _const_spec((LANES, 1)), _const_spec((ATT_WIDTH, D)),
            _const_spec((D, D)), _const_spec((D, D)),
            _const_spec((len(POOL_WINDOWS), POOL_GROUP_WIDTH, POOL_GROUP_WIDTH)),
            _const_spec((1, D)), _const_spec((N_META, D)),
        ],
        out_specs=(qtiled_spec,
                   pl.BlockSpec((1, TOKEN_TILE, ATT_WIDTH), lambda b, t: (b, t, 0)),
                   pl.BlockSpec((1, TOKEN_TILE, LANES), lambda b, t: (b, t, 0)),
                   ktiled_spec, qtiled_spec,
                   pl.BlockSpec((1, TOKEN_TILE, D), lambda b, t: (b, t, 0))),
        out_shape=(qtiled, jax.ShapeDtypeStruct((B, L, ATT_WIDTH), bf16),
                   jax.ShapeDtypeStruct((B, L, LANES), bf16), ktiled, qtiled,
                   jax.ShapeDtypeStruct((B, L, D), bf16)),
        scratch_shapes=[pltpu.VMEM((HEADS, LANES), f32), pltpu.VMEM((N_META, D), f32)],
        compiler_params=pltpu.CompilerParams(
            dimension_semantics=("arbitrary", "arbitrary"), vmem_limit_bytes=VMEM_LIMIT),
        name="in_proj",
    )(x, g_in, b_in, wqT, wk, wvT, wfT, bf_col, wgT, wu, wgp, wpool, pscale, umeta)

    gd = HEAD_GROUP * HEAD_DIM
    head_ktiled = pl.BlockSpec((1, n_ab, gd, ATT_TILE), lambda b, h: (b, 0, h, 0))
    head_qtiled = pl.BlockSpec((1, n_t, gd, Q_TILE), lambda b, h: (b, 0, h, 0))
    aT = pl.pallas_call(
        _attn_kernel,
        grid=(B, HEADS // HEAD_GROUP),
        in_specs=[
            head_qtiled,
            pl.BlockSpec((1, L, gd), lambda b, h: (b, 0, h)),
            pl.BlockSpec((1, L, LANES), lambda b, h: (b, 0, 0)),
            head_ktiled, head_qtiled,
            pl.BlockSpec((META_PAD, gd), lambda b, h: (0, h)),
            pl.BlockSpec((META_PAD, LANES), lambda b, h: (0, 0)),
            pl.BlockSpec((gd, META_PAD), lambda b, h: (h, 0)),
        ],
        out_specs=head_qtiled,
        out_shape=qtiled,
        scratch_shapes=[pltpu.VMEM((2, HEAD_GROUP, KEY_WIDTH, Q_TILE), bf16),
                        pltpu.VMEM((3, HEAD_GROUP, ATT_TILE, Q_TILE), f32),
                        pltpu.VMEM((2, HEAD_GROUP, N_META, Q_TILE), f32),
                        pltpu.VMEM((2, HEAD_GROUP, HEAD_DIM + ONES_ROWS, Q_TILE), f32)],
        compiler_params=pltpu.CompilerParams(
            dimension_semantics=("arbitrary", "arbitrary"), vmem_limit_bytes=VMEM_LIMIT),
        name="fox_attn",
    )(qT, kk, aug, vT, sgT, kmeta, augmeta, vTmeta)

    out = pl.pallas_call(
        _out_proj_kernel,
        grid=(B, n_t),
        in_specs=[
            pl.BlockSpec((1, TOKEN_TILE, D), lambda b, t: (b, t, 0)),
            _const_spec((1, D)), _const_spec((1, D)),
            qtiled_spec,
            pl.BlockSpec((1, TOKEN_TILE, D), lambda b, t: (b, t, 0)),
            _const_spec((D, ATT_WIDTH)), _const_spec((D, D)),
            _const_spec((1, D)), _const_spec((1, D)),
        ],
        out_specs=pl.BlockSpec((1, TOKEN_TILE, D), lambda b, t: (b, t, 0)),
        out_shape=jax.ShapeDtypeStruct((B, L, D), x.dtype),
        compiler_params=pltpu.CompilerParams(
            dimension_semantics=("arbitrary", "arbitrary"), vmem_limit_bytes=VMEM_LIMIT),
        name="out_proj",
    )(x, g_in, b_in, aT, pmix, waT, wp, g_out, b_out)
    return out
```

```python
import math

import jax
import jax.numpy as jnp
from jax import lax
from jax.experimental import pallas as pl
from jax.experimental.pallas import tpu as pltpu

D_MODEL = 1024
N_META = 16
HEADS = 16
HEAD_DIM = 64
ATT_WIDTH = HEADS * HEAD_DIM
POOL_WINDOWS = (2, 4, 8, 16)
POOL_GROUP_WIDTH = D_MODEL // len(POOL_WINDOWS)
LN_EPS = 1e-5
DEEPNORM_ALPHA = 2.0 ** 0.25
LOG2E = math.log2(math.e)
Q_SCALE = HEAD_DIM ** -0.5 * LOG2E

LANES = 128
PAIR_WIDTH = 2 * HEAD_DIM
KEY_WIDTH = PAIR_WIDTH + LANES
META_PAD = 128
MASKED_BIAS = 1e30
TOKEN_TILE = 512
PREP_ROWS = 256
CAST_ROWS = 560
O_Q, O_K, O_V, O_F = 0, ATT_WIDTH, 2 * ATT_WIDTH, 3 * ATT_WIDTH
O_G = O_F + HEADS
O_U = O_G + ATT_WIDTH
O_GP = O_U + D_MODEL
IN_COLS = O_GP + D_MODEL
ATT_TILE = 256
Q_TILE = 2 * ATT_TILE
ONES_ROWS = 16
HEAD_GROUP = 4
VMEM_LIMIT = 56 * 1024 * 1024

_NT = (((1,), (1,)), ((), ()))


def _layer_norm(x, g, b):
    mu = jnp.mean(x, axis=-1, keepdims=True)
    xc = x - mu
    var = jnp.mean(xc * xc, axis=-1, keepdims=True)
    return xc * lax.rsqrt(var + LN_EPS) * g + b


def _log_sigmoid(z):
    return jnp.minimum(z, 0.0) - jnp.log(1.0 + jnp.exp(-jnp.abs(z)))


def _silu(z):
    return z / (1.0 + jnp.exp(-z))


def _proj(w_ref, hb, lo, n):
    return lax.dot_general(hb, w_ref[lo:lo + n, :], _NT, preferred_element_type=jnp.float32)


def _proj_t(w_ref, hb, lo, n):
    return lax.dot_general(w_ref[lo:lo + n, :], hb, _NT, preferred_element_type=jnp.float32)


def _lane_cumsum(x):
    n = x.shape[-1]
    lane = lax.broadcasted_iota(jnp.int32, x.shape, x.ndim - 1)
    d = 1
    while d < n:
        x = x + jnp.where(lane >= d, pltpu.roll(x, d, x.ndim - 1), 0.0)
        d *= 2
    return x


def _bias_columns(c_rows):
    hi = c_rows.astype(jnp.bfloat16).astype(jnp.float32)
    r1 = c_rows - hi
    mid = r1.astype(jnp.bfloat16).astype(jnp.float32)
    lo = (r1 - mid).astype(jnp.bfloat16).astype(jnp.float32)
    packed = hi + pltpu.roll(mid, HEADS, 1) + pltpu.roll(lo, 2 * HEADS, 1)
    return packed.astype(jnp.bfloat16)


def _pool_mix(u_ext, u, wpool_ref, pscale, gate):
    outs = []
    for gi, w in enumerate(POOL_WINDOWS):
        sl = slice(gi * POOL_GROUP_WIDTH, (gi + 1) * POOL_GROUP_WIDTH)
        r = u_ext[:, sl]
        s = 1
        while s < w:
            r = r + pltpu.roll(r, s, 0)
            s *= 2
        d = r[N_META:, :] * (1.0 / w) - u[:, sl]
        outs.append(jnp.dot(d.astype(jnp.bfloat16), wpool_ref[gi],
                            preferred_element_type=jnp.float32))
    y = jnp.concatenate(outs, axis=-1) * pscale
    return y * gate


def _meta_kernel(mt_ref, g_ref, b_ref, w_ref, bf_ref,
                 kmeta_ref, augmeta_ref, vTmeta_ref, umeta_ref):
    hn = _layer_norm(mt_ref[...], g_ref[...], b_ref[...])
    hb = hn.astype(jnp.bfloat16)
    flT = _proj_t(w_ref, hb, O_F, LANES)
    row = lax.broadcasted_iota(jnp.int32, flT.shape, 0)
    col = lax.broadcasted_iota(jnp.int32, flT.shape, 1)
    valid = (row < HEADS) & (col < N_META)
    logf = jnp.where(valid, _log_sigmoid(flT + bf_ref[...]), 0.0)
    cT = _lane_cumsum(logf)
    total = cT[:, META_PAD - 1:META_PAD]
    c_rows = ((cT - total) * LOG2E).T
    trow = lax.broadcasted_iota(jnp.int32, c_rows.shape, 0)
    tcol = lax.broadcasted_iota(jnp.int32, c_rows.shape, 1)
    c_rows = jnp.where((trow >= N_META) & (tcol < HEADS), MASKED_BIAS, c_rows)
    augmeta_ref[...] = _bias_columns(c_rows)
    kmeta_ref[...] = _proj(w_ref, hb, O_K, ATT_WIDTH).astype(jnp.bfloat16)
    vTmeta_ref[...] = _proj_t(w_ref, hb, O_V, ATT_WIDTH).astype(jnp.bfloat16)
    u = _proj(w_ref, hb, O_U, D_MODEL)
    umeta_ref[...] = u[:N_META, :]


def _in_proj_kernel(x_ref, g_ref, b_ref, w_ref, bf_ref, wpool_ref, pscale_ref, umeta_ref,
                    qT_ref, k_ref, aug_ref, vT_ref, sgT_ref, p_ref, carry_c, carry_u):
    t = pl.program_id(1)

    @pl.when(t == 0)
    def _():
        carry_c[...] = jnp.zeros_like(carry_c)
        carry_u[...] = umeta_ref[...]

    hn = _layer_norm(x_ref[0], g_ref[...], b_ref[...])
    hb = hn.astype(jnp.bfloat16)

    qT = _proj_t(w_ref, hb, O_Q, ATT_WIDTH) * Q_SCALE
    qT_ref[0, 0] = qT.astype(jnp.bfloat16)

    vT = _proj_t(w_ref, hb, O_V, ATT_WIDTH)
    vTb = vT.astype(jnp.bfloat16)
    for j in range(TOKEN_TILE // ATT_TILE):
        vT_ref[0, j] = vTb[:, j * ATT_TILE:(j + 1) * ATT_TILE]

    gT = _proj_t(w_ref, hb, O_G, ATT_WIDTH)
    sgT_ref[0, 0] = _silu(gT).astype(jnp.bfloat16)

    flT = _proj_t(w_ref, hb, O_F, LANES)
    logf = _log_sigmoid(flT[:HEADS, :] + bf_ref[:HEADS, :])
    cT = _lane_cumsum(logf) + carry_c[:, 0:1]
    carry_c[...] = jnp.broadcast_to(cT[:, TOKEN_TILE - 1:TOKEN_TILE], carry_c.shape)
    cT_pad = jnp.concatenate(
        [cT * LOG2E, jnp.zeros((LANES - HEADS, TOKEN_TILE), jnp.float32)], axis=0)
    c_rows = cT_pad.T
    aug_ref[0] = _bias_columns(c_rows)
    k_ref[0] = _proj(w_ref, hb, O_K, ATT_WIDTH).astype(jnp.bfloat16)

    u = _proj(w_ref, hb, O_U, D_MODEL)
    gp = _proj(w_ref, hb, O_GP, D_MODEL)
    u_ext = jnp.concatenate([carry_u[...], u], axis=0)
    carry_u[...] = u[TOKEN_TILE - N_META:, :]
    p = _pool_mix(u_ext, u, wpool_ref, pscale_ref[...], _silu(gp))
    p_ref[0] = p.astype(jnp.bfloat16)


def _attn_kernel(qT_ref, k_ref, aug_ref, vT_ref, sgT_ref, kmeta_ref, augmeta_ref, vTmeta_ref, o_ref,
                 qa_ref, s_ref, smeta_ref, acc_ref):
    n_qb = qT_ref.shape[1]
    heads = range(HEAD_GROUP)
    sel_row = lax.broadcasted_iota(jnp.int32, (LANES, Q_TILE), 0)
    for g in heads:
        h = pl.program_id(1) * HEAD_GROUP + g
        picks = (sel_row == h) | (sel_row == HEADS + h) | (sel_row == 2 * HEADS + h)
        for par in range(2):
            qa_ref[par, g, :PAIR_WIDTH, :] = jnp.zeros((PAIR_WIDTH, Q_TILE), jnp.bfloat16)
            qa_ref[par, g, PAIR_WIDTH:, :] = jnp.where(picks, -1.0, 0.0).astype(jnp.bfloat16)
    ones = jnp.ones((ONES_ROWS, ATT_TILE), jnp.bfloat16)
    ones_meta = jnp.ones((ONES_ROWS, META_PAD), jnp.bfloat16)
    krow = lax.broadcasted_iota(jnp.int32, (ATT_TILE, Q_TILE), 0)
    qcol = lax.broadcasted_iota(jnp.int32, (ATT_TILE, Q_TILE), 1)
    causal_a = krow <= qcol
    causal_b = causal_a[:, :ATT_TILE]
    meta_fill = jnp.zeros((META_PAD - N_META, Q_TILE), jnp.bfloat16)
    acc_zero = jnp.zeros(acc_ref.shape[2:], jnp.float32)

    def hrows(g, width):
        return slice(g * width, (g + 1) * width)

    def colmax(s):
        return jnp.max(s, axis=0, keepdims=True)

    def key_tile(g, kj):
        rows = pl.ds(pl.multiple_of(kj * ATT_TILE, ATT_TILE), ATT_TILE)
        return jnp.concatenate([k_ref[0, rows, hrows(g // 2, PAIR_WIDTH)], aug_ref[0, rows, :]], axis=1)

    def value_tile(g, kj):
        return jnp.concatenate([vT_ref[0, kj, hrows(g, HEAD_DIM), :], ones], axis=0)

    def head_scores(g, kj, qa, dst):
        s = jnp.dot(key_tile(g, kj), qa[g], preferred_element_type=jnp.float32)
        dst[g] = s
        return colmax(s)

    def head_consume(g, kj, src, acc, smax, m):
        m_new = jnp.maximum(m, smax)
        alpha = jnp.exp2(m - m_new)
        p = jnp.exp2(src[g] - m_new).astype(jnp.bfloat16)
        acc[g] = alpha * acc[g] + jnp.dot(value_tile(g, kj), p, preferred_element_type=jnp.float32)
        return m_new

    def start_block(g, qi, qa, smeta, dst):
        qa[g, hrows(g % 2, HEAD_DIM), :] = qT_ref[0, qi, hrows(g, HEAD_DIM), :]
        kmeta = jnp.concatenate([kmeta_ref[:N_META, hrows(g // 2, PAIR_WIDTH)],
                                 augmeta_ref[:N_META, :]], axis=1)
        smeta[g] = jnp.dot(kmeta, qa[g], preferred_element_type=jnp.float32)
        return head_scores(g, 0, qa, dst)

    def block(qi, par, smax_first):
        qa, qa_nx = qa_ref.at[par], qa_ref.at[1 - par]
        smeta, smeta_nx = smeta_ref.at[par], smeta_ref.at[1 - par]
        acc, acc_nx = acc_ref.at[par], acc_ref.at[1 - par]
        s_first, s_other, s_nx = s_ref.at[2 * par], s_ref.at[1], s_ref.at[2 - 2 * par]
        m0 = tuple(jnp.full((1, Q_TILE), -jnp.inf, jnp.float32) for _ in heads)

        def step(k_next, dst, k_cur, src, smax_cur, ms):
            smax_next, ms_new = [], []
            for g in heads:
                smax_next.append(head_scores(g, k_next, qa, dst))
                ms_new.append(head_consume(g, k_cur, src, acc, smax_cur[g], ms[g]))
            return tuple(ms_new), tuple(smax_next)

        def pair(t, ms, smax):
            ms, smax_o = step(t + 1, s_other, t, s_first, smax, ms)
            return step(t + 2, s_first, t + 1, s_other, smax_o, ms)

        def quad(k, carry):
            return pair(4 * k + 2, *pair(4 * k, *carry))

        ms, smax = lax.fori_loop(0, qi // 2, quad, (m0, smax_first))
        if par == 1:
            ms, _ = pair(2 * qi - 2, ms, smax)

        q_next = jnp.minimum(qi + 1, n_qb - 1)
        smax_next = []
        for g in heads:
            sb = jnp.dot(key_tile(g, 2 * qi + 1), qa[g][:, ATT_TILE:],
                         preferred_element_type=jnp.float32)
            sb = jnp.where(causal_b, sb, -jnp.inf)
            s = jnp.where(causal_a, s_first[g], -jnp.inf)
            sm = smeta[g]
            m_new = jnp.maximum(jnp.maximum(ms[g], colmax(s)), colmax(sm))
            m_new = jnp.concatenate(
                [m_new[:, :ATT_TILE], jnp.maximum(m_new[:, ATT_TILE:], colmax(sb))], axis=1)
            alpha = jnp.exp2(ms[g] - m_new)
            p = jnp.exp2(s - m_new).astype(jnp.bfloat16)
            pb = jnp.exp2(sb - m_new[:, ATT_TILE:]).astype(jnp.bfloat16)
            pm = jnp.concatenate([jnp.exp2(sm - m_new).astype(jnp.bfloat16), meta_fill], axis=0)
            smax_next.append(start_block(g, q_next, qa_nx, smeta_nx, s_nx))
            acc_nx[g] = acc_zero
            vmeta = jnp.concatenate([vTmeta_ref[hrows(g, HEAD_DIM), :], ones_meta], axis=0)
            a = (alpha * acc[g]
                 + jnp.dot(value_tile(g, 2 * qi), p, preferred_element_type=jnp.float32)
                 + jnp.dot(vmeta, pm, preferred_element_type=jnp.float32))
            a_late = a[:, ATT_TILE:] + jnp.dot(value_tile(g, 2 * qi + 1), pb,
                                               preferred_element_type=jnp.float32)
            a = jnp.concatenate([a[:, :ATT_TILE], a_late], axis=1)
            o = a[:HEAD_DIM, :] / a[HEAD_DIM:HEAD_DIM + 1, :]
            o = o * sgT_ref[0, qi, hrows(g, HEAD_DIM), :].astype(jnp.float32)
            o_ref[0, qi, hrows(g, HEAD_DIM), :] = o.astype(jnp.bfloat16)
        return tuple(smax_next)

    def two_blocks(j, smax_first):
        return block(2 * j + 1, 1, block(2 * j, 0, smax_first))

    smax_first = []
    for g in heads:
        acc_ref[0, g] = acc_zero
        smax_first.append(start_block(g, 0, qa_ref.at[0], smeta_ref.at[0], s_ref.at[0]))
    lax.fori_loop(0, n_qb // 2, two_blocks, tuple(smax_first))


def _out_proj_kernel(x_ref, gin_ref, bin_ref, aT_ref, p_ref, waT_ref, wp_ref, g_ref, b_ref, o_ref):
    for j in range(TOKEN_TILE // ATT_TILE):
        rows = slice(j * ATT_TILE, (j + 1) * ATT_TILE)
        hn = _layer_norm(x_ref[0, rows, :], gin_ref[...], bin_ref[...])
        yT = jnp.dot(waT_ref[...], aT_ref[0, 0, :, rows], preferred_element_type=jnp.float32)
        y = yT.T + jnp.dot(p_ref[0, rows, :], wp_ref[...], preferred_element_type=jnp.float32)
        o_ref[0, rows, :] = _layer_norm(DEEPNORM_ALPHA * hn + y, g_ref[...], b_ref[...])


def _cast_kernel(w_ref, o_ref):
    o_ref[...] = w_ref[...].astype(jnp.bfloat16)


def _out_weight_kernel(wa_ref, wpin_ref, waT_ref, wp_ref):
    waT_ref[...] = wa_ref[0].T.astype(jnp.bfloat16)
    wp_ref[...] = wpin_ref[0].astype(jnp.bfloat16)


def _const_spec(shape):
    nd = len(shape)
    return pl.BlockSpec(shape, lambda *_: (0,) * nd, pipeline_mode=pl.Buffered(1))


def kernel(x, meta_tokens, ln_in_g, ln_in_b, w_in, b_forget, w_pool, pool_scale, w_out, ln_g, ln_b):
    B, L, D = x.shape
    assert D == D_MODEL and L % (2 * Q_TILE) == 0 and TOKEN_TILE == Q_TILE and w_in.shape[0] == 1
    bf16, f32 = jnp.bfloat16, jnp.float32
    assert w_in.shape[2] == IN_COLS and w_out.shape[1] == 2 * D_MODEL

    w_t = jnp.swapaxes(w_in, 1, 2)[0]
    w_bf = pl.pallas_call(
        _cast_kernel,
        grid=(IN_COLS // CAST_ROWS,),
        in_specs=[pl.BlockSpec((CAST_ROWS, D), lambda r: (r, 0))],
        out_specs=pl.BlockSpec((CAST_ROWS, D), lambda r: (r, 0)),
        out_shape=jax.ShapeDtypeStruct((IN_COLS, D), bf16),
        compiler_params=pltpu.CompilerParams(dimension_semantics=("arbitrary",)),
        name="in_weight_cast",
    )(w_t)
    square = jax.ShapeDtypeStruct((D, D), bf16)
    waT, wp = pl.pallas_call(
        _out_weight_kernel,
        grid=(D // PREP_ROWS,),
        in_specs=[pl.BlockSpec((1, PREP_ROWS, D), lambda r: (0, r, 0)),
                  pl.BlockSpec((1, PREP_ROWS, D), lambda r: (0, r + D // PREP_ROWS, 0))],
        out_specs=(pl.BlockSpec((D, PREP_ROWS), lambda r: (0, r)),
                   pl.BlockSpec((PREP_ROWS, D), lambda r: (r, 0))),
        out_shape=(square, square),
        compiler_params=pltpu.CompilerParams(dimension_semantics=("arbitrary",)),
        name="out_weight_prep",
    )(w_out, w_out)
    bf_col = jnp.pad(b_forget[0].astype(f32), (0, LANES - HEADS)).reshape(LANES, 1)
    wpool = w_pool[0].astype(bf16)
    pscale = pool_scale[0].reshape(1, D).astype(f32)
    g_in = ln_in_g.reshape(1, D).astype(f32)
    b_in = ln_in_b.reshape(1, D).astype(f32)
    g_out = ln_g[0].reshape(1, D).astype(f32)
    b_out = ln_b[0].reshape(1, D).astype(f32)
    mt_pad = jnp.pad(meta_tokens.astype(f32), ((0, META_PAD - N_META), (0, 0)))

    kmeta, augmeta, vTmeta, umeta = pl.pallas_call(
        _meta_kernel,
        out_shape=(jax.ShapeDtypeStruct((META_PAD, ATT_WIDTH), bf16),
                   jax.ShapeDtypeStruct((META_PAD, LANES), bf16),
                   jax.ShapeDtypeStruct((ATT_WIDTH, META_PAD), bf16),
                   jax.ShapeDtypeStruct((N_META, D), f32)),
        compiler_params=pltpu.CompilerParams(vmem_limit_bytes=VMEM_LIMIT),
        name="meta_proj",
    )(mt_pad, g_in, b_in, w_bf, bf_col)

    n_t = L // TOKEN_TILE
    n_sub = TOKEN_TILE // ATT_TILE
    n_ab = L // ATT_TILE
    ktiled = jax.ShapeDtypeStruct((B, n_ab, ATT_WIDTH, ATT_TILE), bf16)
    ktiled_spec = pl.BlockSpec((1, n_sub, ATT_WIDTH, ATT_TILE), lambda b, t: (b, t, 0, 0))
    qtiled = jax.ShapeDtypeStruct((B, n_t, ATT_WIDTH, Q_TILE), bf16)
    qtiled_spec = pl.BlockSpec((1, 1, ATT_WIDTH, Q_TILE), lambda b, t: (b, t, 0, 0))
    qT, kk, aug, vT, sgT, pmix = pl.pallas_call(
        _in_proj_kernel,
        grid=(B, n_t),
        in_specs=[
            pl.BlockSpec((1, TOKEN_TILE, D), lambda b, t: (b, t, 0)),
            _const_spec((1, D)), _const_spec((1, D)),
            _const_spec((IN_COLS, D)), _const_spec((LANES, 1)),
            _const_spec((len(POOL_WINDOWS), POOL_GROUP_WIDTH, POOL_GROUP_WIDTH)),
            _const_spec((1, D)), _const_spec((N_META, D)),
        ],
        out_specs=(qtiled_spec,
                   pl.BlockSpec((1, TOKEN_TILE, ATT_WIDTH), lambda b, t: (b, t, 0)),
                   pl.BlockSpec((1, TOKEN_TILE, LANES), lambda b, t: (b, t, 0)),
                   ktiled_spec, qtiled_spec,
                   pl.BlockSpec((1, TOKEN_TILE, D), lambda b, t: (b, t, 0))),
        out_shape=(qtiled, jax.ShapeDtypeStruct((B, L, ATT_WIDTH), bf16),
                   jax.ShapeDtypeStruct((B, L, LANES), bf16), ktiled, qtiled,
                   jax.ShapeDtypeStruct((B, L, D), bf16)),
        scratch_shapes=[pltpu.VMEM((HEADS, LANES), f32), pltpu.VMEM((N_META, D), f32)],
        compiler_params=pltpu.CompilerParams(
            dimension_semantics=("arbitrary", "arbitrary"), vmem_limit_bytes=VMEM_LIMIT),
        name="in_proj",
    )(x, g_in, b_in, w_bf, bf_col, wpool, pscale, umeta)

    gd = HEAD_GROUP * HEAD_DIM
    head_ktiled = pl.BlockSpec((1, n_ab, gd, ATT_TILE), lambda b, h: (b, 0, h, 0))
    head_qtiled = pl.BlockSpec((1, n_t, gd, Q_TILE), lambda b, h: (b, 0, h, 0))
    aT = pl.pallas_call(
        _attn_kernel,
        grid=(B, HEADS // HEAD_GROUP),
        in_specs=[
            head_qtiled,
            pl.BlockSpec((1, L, gd), lambda b, h: (b, 0, h)),
            pl.BlockSpec((1, L, LANES), lambda b, h: (b, 0, 0)),
            head_ktiled, head_qtiled,
            pl.BlockSpec((META_PAD, gd), lambda b, h: (0, h)),
            pl.BlockSpec((META_PAD, LANES), lambda b, h: (0, 0)),
            pl.BlockSpec((gd, META_PAD), lambda b, h: (h, 0)),
        ],
        out_specs=head_qtiled,
        out_shape=qtiled,
        scratch_shapes=[pltpu.VMEM((2, HEAD_GROUP, KEY_WIDTH, Q_TILE), bf16),
                        pltpu.VMEM((3, HEAD_GROUP, ATT_TILE, Q_TILE), f32),
                        pltpu.VMEM((2, HEAD_GROUP, N_META, Q_TILE), f32),
                        pltpu.VMEM((2, HEAD_GROUP, HEAD_DIM + ONES_ROWS, Q_TILE), f32)],
        compiler_params=pltpu.CompilerParams(
            dimension_semantics=("arbitrary", "arbitrary"), vmem_limit_bytes=VMEM_LIMIT),
        name="fox_attn",
    )(qT, kk, aug, vT, sgT, kmeta, augmeta, vTmeta)

    out = pl.pallas_call(
        _out_proj_kernel,
        grid=(B, n_t),
        in_specs=[
            pl.BlockSpec((1, TOKEN_TILE, D), lambda b, t: (b, t, 0)),
            _const_spec((1, D)), _const_spec((1, D)),
            qtiled_spec,
            pl.BlockSpec((1, TOKEN_TILE, D), lambda b, t: (b, t, 0)),
            _const_spec((D, ATT_WIDTH)), _const_spec((D, D)),
            _const_spec((1, D)), _const_spec((1, D)),
        ],
        out_specs=pl.BlockSpec((1, TOKEN_TILE, D), lambda b, t: (b, t, 0)),
        out_shape=jax.ShapeDtypeStruct((B, L, D), x.dtype),
        compiler_params=pltpu.CompilerParams(
            dimension_semantics=("arbitrary", "arbitrary"), vmem_limit_bytes=VMEM_LIMIT),
        name="out_proj",
    )(x, g_in, b_in, aT, pmix, waT, wp, g_out, b_out)
    return out
```

```python
import math

import jax
import jax.numpy as jnp
from jax import lax
from jax.experimental import pallas as pl
from jax.experimental.pallas import tpu as pltpu

D_MODEL = 1024
N_META = 16
HEADS = 16
HEAD_DIM = 64
ATT_WIDTH = HEADS * HEAD_DIM
POOL_WINDOWS = (2, 4, 8, 16)
POOL_GROUP_WIDTH = D_MODEL // len(POOL_WINDOWS)
LN_EPS = 1e-5
DEEPNORM_ALPHA = 2.0 ** 0.25
LOG2E = math.log2(math.e)
Q_SCALE = HEAD_DIM ** -0.5 * LOG2E

LANES = 128
PAIR_WIDTH = 2 * HEAD_DIM
KEY_WIDTH = PAIR_WIDTH + LANES
META_PAD = 128
MASKED_BIAS = 1e30
TOKEN_TILE = 512
PREP_ROWS = 256
CAST_ROWS = 560
O_Q, O_K, O_V, O_F = 0, ATT_WIDTH, 2 * ATT_WIDTH, 3 * ATT_WIDTH
O_G = O_F + HEADS
O_U = O_G + ATT_WIDTH
O_GP = O_U + D_MODEL
IN_COLS = O_GP + D_MODEL
ATT_TILE = 256
Q_TILE = 2 * ATT_TILE
ONES_ROWS = 16
HEAD_GROUP = 4
VMEM_LIMIT = 56 * 1024 * 1024

_NT = (((1,), (1,)), ((), ()))


def _layer_norm(x, g, b):
    mu = jnp.mean(x, axis=-1, keepdims=True)
    xc = x - mu
    var = jnp.mean(xc * xc, axis=-1, keepdims=True)
    return xc * lax.rsqrt(var + LN_EPS) * g + b


def _log_sigmoid(z):
    return jnp.minimum(z, 0.0) - jnp.log(1.0 + jnp.exp(-jnp.abs(z)))


def _silu(z):
    return z / (1.0 + jnp.exp(-z))


def _proj(w_ref, hb, lo, n):
    return lax.dot_general(hb, w_ref[lo:lo + n, :], _NT, preferred_element_type=jnp.float32)


def _proj_t(w_ref, hb, lo, n):
    return lax.dot_general(w_ref[lo:lo + n, :], hb, _NT, preferred_element_type=jnp.float32)


def _lane_cumsum(x):
    n = x.shape[-1]
    lane = lax.broadcasted_iota(jnp.int32, x.shape, x.ndim - 1)
    d = 1
    while d < n:
        x = x + jnp.where(lane >= d, pltpu.roll(x, d, x.ndim - 1), 0.0)
        d *= 2
    return x


def _bias_columns(c_rows):
    hi = c_rows.astype(jnp.bfloat16).astype(jnp.float32)
    r1 = c_rows - hi
    mid = r1.astype(jnp.bfloat16).astype(jnp.float32)
    lo = (r1 - mid).astype(jnp.bfloat16).astype(jnp.float32)
    packed = hi + pltpu.roll(mid, HEADS, 1) + pltpu.roll(lo, 2 * HEADS, 1)
    return packed.astype(jnp.bfloat16)


def _pool_mix(u_ext, u, wpool_ref, pscale, gate):
    outs = []
    for gi, w in enumerate(POOL_WINDOWS):
        sl = slice(gi * POOL_GROUP_WIDTH, (gi + 1) * POOL_GROUP_WIDTH)
        r = u_ext[:, sl]
        s = 1
        while s < w:
            r = r + pltpu.roll(r, s, 0)
            s *= 2
        d = r[N_META:, :] * (1.0 / w) - u[:, sl]
        outs.append(jnp.dot(d.astype(jnp.bfloat16), wpool_ref[gi],
                            preferred_element_type=jnp.float32))
    y = jnp.concatenate(outs, axis=-1) * pscale
    return y * gate


def _meta_kernel(mt_ref, g_ref, b_ref, w_ref, bf_ref,
                 kmeta_ref, augmeta_ref, vTmeta_ref, umeta_ref):
    hn = _layer_norm(mt_ref[...], g_ref[...], b_ref[...])
    hb = hn.astype(jnp.bfloat16)
    flT = _proj_t(w_ref, hb, O_F, LANES)
    row = lax.broadcasted_iota(jnp.int32, flT.shape, 0)
    col = lax.broadcasted_iota(jnp.int32, flT.shape, 1)
    valid = (row < HEADS) & (col < N_META)
    logf = jnp.where(valid, _log_sigmoid(flT + bf_ref[...]), 0.0)
    cT = _lane_cumsum(logf)
    total = cT[:, META_PAD - 1:META_PAD]
    c_rows = ((cT - total) * LOG2E).T
    trow = lax.broadcasted_iota(jnp.int32, c_rows.shape, 0)
    tcol = lax.broadcasted_iota(jnp.int32, c_rows.shape, 1)
    c_rows = jnp.where((trow >= N_META) & (tcol < HEADS), MASKED_BIAS, c_rows)
    augmeta_ref[...] = _bias_columns(c_rows)
    kmeta_ref[...] = _proj(w_ref, hb, O_K, ATT_WIDTH).astype(jnp.bfloat16)
    vTmeta_ref[...] = _proj_t(w_ref, hb, O_V, ATT_WIDTH).astype(jnp.bfloat16)
    u = _proj(w_ref, hb, O_U, D_MODEL)
    umeta_ref[...] = u[:N_META, :]


def _in_proj_kernel(x_ref, g_ref, b_ref, w_ref, bf_ref, wpool_ref, pscale_ref, umeta_ref,
                    qT_ref, k_ref, aug_ref, vT_ref, sgT_ref, p_ref, carry_c, carry_u):
    t = pl.program_id(1)

    @pl.when(t == 0)
    def _():
        carry_c[...] = jnp.zeros_like(carry_c)
        carry_u[...] = umeta_ref[...]

    hn = _layer_norm(x_ref[0], g_ref[...], b_ref[...])
    hb = hn.astype(jnp.bfloat16)

    qT = _proj_t(w_ref, hb, O_Q, ATT_WIDTH) * Q_SCALE
    qT_ref[0, 0] = qT.astype(jnp.bfloat16)

    vT = _proj_t(w_ref, hb, O_V, ATT_WIDTH)
    vTb = vT.astype(jnp.bfloat16)
    for j in range(TOKEN_TILE // ATT_TILE):
        vT_ref[0, j] = vTb[:, j * ATT_TILE:(j + 1) * ATT_TILE]

    gT = _proj_t(w_ref, hb, O_G, ATT_WIDTH)
    sgT_ref[0, 0] = _silu(gT).astype(jnp.bfloat16)

    flT = _proj_t(w_ref, hb, O_F, LANES)
    logf = _log_sigmoid(flT[:HEADS, :] + bf_ref[:HEADS, :])
    cT = _lane_cumsum(logf) + carry_c[:, 0:1]
    carry_c[...] = jnp.broadcast_to(cT[:, TOKEN_TILE - 1:TOKEN_TILE], carry_c.shape)
    cT_pad = jnp.concatenate(
        [cT * LOG2E, jnp.zeros((LANES - HEADS, TOKEN_TILE), jnp.float32)], axis=0)
    c_rows = cT_pad.T
    aug_ref[0] = _bias_columns(c_rows)
    k_ref[0] = _proj(w_ref, hb, O_K, ATT_WIDTH).astype(jnp.bfloat16)

    u = _proj(w_ref, hb, O_U, D_MODEL)
    gp = _proj(w_ref, hb, O_GP, D_MODEL)
    u_ext = jnp.concatenate([carry_u[...], u], axis=0)
    carry_u[...] = u[TOKEN_TILE - N_META:, :]
    p = _pool_mix(u_ext, u, wpool_ref, pscale_ref[...], _silu(gp))
    p_ref[0] = p.astype(jnp.bfloat16)


def _attn_kernel(qT_ref, k_ref, aug_ref, vT_ref, sgT_ref, kmeta_ref, augmeta_ref, vTmeta_ref, o_ref,
                 qa_ref, s_ref, smeta_ref, acc_ref):
    n_qb = qT_ref.shape[1]
    heads = range(HEAD_GROUP)
    sel_row = lax.broadcasted_iota(jnp.int32, (LANES, Q_TILE), 0)
    for g in heads:
        h = pl.program_id(1) * HEAD_GROUP + g
        picks = (sel_row == h) | (sel_row == HEADS + h) | (sel_row == 2 * HEADS + h)
        for par in range(2):
            qa_ref[par, g, :PAIR_WIDTH, :] = jnp.zeros((PAIR_WIDTH, Q_TILE), jnp.bfloat16)
            qa_ref[par, g, PAIR_WIDTH:, :] = jnp.where(picks, -1.0, 0.0).astype(jnp.bfloat16)
    ones = jnp.ones((ONES_ROWS, ATT_TILE), jnp.bfloat16)
    ones_meta = jnp.ones((ONES_ROWS, META_PAD), jnp.bfloat16)
    krow = lax.broadcasted_iota(jnp.int32, (ATT_TILE, Q_TILE), 0)
    qcol = lax.broadcasted_iota(jnp.int32, (ATT_TILE, Q_TILE), 1)
    causal_a = krow <= qcol
    causal_b = causal_a[:, :ATT_TILE]
    meta_fill = jnp.zeros((META_PAD - N_META, Q_TILE), jnp.bfloat16)
    acc_zero = jnp.zeros(acc_ref.shape[2:], jnp.float32)

    def hrows(g, width):
        return slice(g * width, (g + 1) * width)

    def colmax(s):
        return jnp.max(s, axis=0, keepdims=True)

    def key_tile(g, kj):
        rows = pl.ds(pl.multiple_of(kj * ATT_TILE, ATT_TILE), ATT_TILE)
        return jnp.concatenate([k_ref[0, rows, hrows(g // 2, PAIR_WIDTH)], aug_ref[0, rows, :]], axis=1)

    def value_tile(g, kj):
        return jnp.concatenate([vT_ref[0, kj, hrows(g, HEAD_DIM), :], ones], axis=0)

    def head_scores(g, kj, qa, dst):
        s = jnp.dot(key_tile(g, kj), qa[g], preferred_element_type=jnp.float32)
        dst[g] = s
        return colmax(s)

    def head_consume(g, kj, src, acc, smax, m):
        m_new = jnp.maximum(m, smax)
        alpha = jnp.exp2(m - m_new)
        p = jnp.exp2(src[g] - m_new).astype(jnp.bfloat16)
        acc[g] = alpha * acc[g] + jnp.dot(value_tile(g, kj), p, preferred_element_type=jnp.float32)
        return m_new

    def start_block(g, qi, qa, smeta, dst):
        qa[g, hrows(g % 2, HEAD_DIM), :] = qT_ref[0, qi, hrows(g, HEAD_DIM), :]
        kmeta = jnp.concatenate([kmeta_ref[:N_META, hrows(g // 2, PAIR_WIDTH)],
                                 augmeta_ref[:N_META, :]], axis=1)
        smeta[g] = jnp.dot(kmeta, qa[g], preferred_element_type=jnp.float32)
        return head_scores(g, 0, qa, dst)

    def block(qi, par, smax_first):
        qa, qa_nx = qa_ref.at[par], qa_ref.at[1 - par]
        smeta, smeta_nx = smeta_ref.at[par], smeta_ref.at[1 - par]
        acc, acc_nx = acc_ref.at[par], acc_ref.at[1 - par]
        s_first, s_other, s_nx = s_ref.at[2 * par], s_ref.at[1], s_ref.at[2 - 2 * par]
        m0 = tuple(jnp.full((1, Q_TILE), -jnp.inf, jnp.float32) for _ in heads)

        def step(k_next, dst, k_cur, src, smax_cur, ms):
            smax_next, ms_new = [], []
            for g in heads:
                smax_next.append(head_scores(g, k_next, qa, dst))
                ms_new.append(head_consume(g, k_cur, src, acc, smax_cur[g], ms[g]))
            return tuple(ms_new), tuple(smax_next)

        def pair(t, ms, smax):
            ms, smax_o = step(t + 1, s_other, t, s_first, smax, ms)
            return step(t + 2, s_first, t + 1, s_other, smax_o, ms)

        def quad(k, carry):
            return pair(4 * k + 2, *pair(4 * k, *carry))

        ms, smax = lax.fori_loop(0, qi // 2, quad, (m0, smax_first))
        if par == 1:
            ms, _ = pair(2 * qi - 2, ms, smax)

        q_next = jnp.minimum(qi + 1, n_qb - 1)
        sbs = [jnp.dot(key_tile(g, 2 * qi + 1), qa[g][:, ATT_TILE:],
                       preferred_element_type=jnp.float32) for g in heads]
        smax_next = [start_block(g, q_next, qa_nx, smeta_nx, s_nx) for g in heads]
        for g in heads:
            sb = jnp.where(causal_b, sbs[g], -jnp.inf)
            s = jnp.where(causal_a, s_first[g], -jnp.inf)
            sm = smeta[g]
            m_new = jnp.maximum(jnp.maximum(ms[g], colmax(s)), colmax(sm))
            m_new = jnp.concatenate(
                [m_new[:, :ATT_TILE], jnp.maximum(m_new[:, ATT_TILE:], colmax(sb))], axis=1)
            alpha = jnp.exp2(ms[g] - m_new)
            p = jnp.exp2(s - m_new).astype(jnp.bfloat16)
            pb = jnp.exp2(sb - m_new[:, ATT_TILE:]).astype(jnp.bfloat16)
            pm = jnp.concatenate([jnp.exp2(sm - m_new).astype(jnp.bfloat16), meta_fill], axis=0)
            acc_nx[g] = acc_zero
            vmeta = jnp.concatenate([vTmeta_ref[hrows(g, HEAD_DIM), :], ones_meta], axis=0)
            a = (alpha * acc[g]
                 + jnp.dot(value_tile(g, 2 * qi), p, preferred_element_type=jnp.float32)
                 + jnp.dot(vmeta, pm, preferred_element_type=jnp.float32))
            a_late = a[:, ATT_TILE:] + jnp.dot(value_tile(g, 2 * qi + 1), pb,
                                               preferred_element_type=jnp.float32)
            a = jnp.concatenate([a[:, :ATT_TILE], a_late], axis=1)
            o = a[:HEAD_DIM, :] / a[HEAD_DIM:HEAD_DIM + 1, :]
            o = o * sgT_ref[0, qi, hrows(g, HEAD_DIM), :].astype(jnp.float32)
            o_ref[0, qi, hrows(g, HEAD_DIM), :] = o.astype(jnp.bfloat16)
        return tuple(smax_next)

    def two_blocks(j, smax_first):
        return block(2 * j + 1, 1, block(2 * j, 0, smax_first))

    smax_first = []
    for g in heads:
        acc_ref[0, g] = acc_zero
        smax_first.append(start_block(g, 0, qa_ref.at[0], smeta_ref.at[0], s_ref.at[0]))
    lax.fori_loop(0, n_qb // 2, two_blocks, tuple(smax_first))


def _out_proj_kernel(x_ref, gin_ref, bin_ref, aT_ref, p_ref, waT_ref, wp_ref, g_ref, b_ref, o_ref):
    for j in range(TOKEN_TILE // ATT_TILE):
        rows = slice(j * ATT_TILE, (j + 1) * ATT_TILE)
        hn = _layer_norm(x_ref[0, rows, :], gin_ref[...], bin_ref[...])
        yT = jnp.dot(waT_ref[...], aT_ref[0, 0, :, rows], preferred_element_type=jnp.float32)
        y = yT.T + jnp.dot(p_ref[0, rows, :], wp_ref[...], preferred_element_type=jnp.float32)
        o_ref[0, rows, :] = _layer_norm(DEEPNORM_ALPHA * hn + y, g_ref[...], b_ref[...])


def _cast_kernel(w_ref, o_ref):
    o_ref[...] = w_ref[...].astype(jnp.bfloat16)


def _out_weight_kernel(wa_ref, wpin_ref, waT_ref, wp_ref):
    waT_ref[...] = wa_ref[0].T.astype(jnp.bfloat16)
    wp_ref[...] = wpin_ref[0].astype(jnp.bfloat16)


def _const_spec(shape):
    nd = len(shape)
    return pl.BlockSpec(shape, lambda *_: (0,) * nd, pipeline_mode=pl.Buffered(1))


def kernel(x, meta_tokens, ln_in_g, ln_in_b, w_in, b_forget, w_pool, pool_scale, w_out, ln_g, ln_b):
    B, L, D = x.shape
    assert D == D_MODEL and L % (2 * Q_TILE) == 0 and TOKEN_TILE == Q_TILE and w_in.shape[0] == 1
    bf16, f32 = jnp.bfloat16, jnp.float32
    assert w_in.shape[2] == IN_COLS and w_out.shape[1] == 2 * D_MODEL

    w_t = jnp.swapaxes(w_in, 1, 2)[0]
    w_bf = pl.pallas_call(
        _cast_kernel,
        grid=(IN_COLS // CAST_ROWS,),
        in_specs=[pl.BlockSpec((CAST_ROWS, D), lambda r: (r, 0))],
        out_specs=pl.BlockSpec((CAST_ROWS, D), lambda r: (r, 0)),
        out_shape=jax.ShapeDtypeStruct((IN_COLS, D), bf16),
        compiler_params=pltpu.CompilerParams(dimension_semantics=("arbitrary",)),
        name="in_weight_cast",
    )(w_t)
    square = jax.ShapeDtypeStruct((D, D), bf16)
    waT, wp = pl.pallas_call(
        _out_weight_kernel,
        grid=(D // PREP_ROWS,),
        in_specs=[pl.BlockSpec((1, PREP_ROWS, D), lambda r: (0, r, 0)),
                  pl.BlockSpec((1, PREP_ROWS, D), lambda r: (0, r + D // PREP_ROWS, 0))],
        out_specs=(pl.BlockSpec((D, PREP_ROWS), lambda r: (0, r)),
                   pl.BlockSpec((PREP_ROWS, D), lambda r: (r, 0))),
        out_shape=(square, square),
        compiler_params=pltpu.CompilerParams(dimension_semantics=("arbitrary",)),
        name="out_weight_prep",
    )(w_out, w_out)
    bf_col = jnp.pad(b_forget[0].astype(f32), (0, LANES - HEADS)).reshape(LANES, 1)
    wpool = w_pool[0].astype(bf16)
    pscale = pool_scale[0].reshape(1, D).astype(f32)
    g_in = ln_in_g.reshape(1, D).astype(f32)
    b_in = ln_in_b.reshape(1, D).astype(f32)
    g_out = ln_g[0].reshape(1, D).astype(f32)
    b_out = ln_b[0].reshape(1, D).astype(f32)
    mt_pad = jnp.pad(meta_tokens.astype(f32), ((0, META_PAD - N_META), (0, 0)))

    kmeta, augmeta, vTmeta, umeta = pl.pallas_call(
        _meta_kernel,
        out_shape=(jax.ShapeDtypeStruct((META_PAD, ATT_WIDTH), bf16),
                   jax.ShapeDtypeStruct((META_PAD, LANES), bf16),
                   jax.ShapeDtypeStruct((ATT_WIDTH, META_PAD), bf16),
                   jax.ShapeDtypeStruct((N_META, D), f32)),
        compiler_params=pltpu.CompilerParams(vmem_limit_bytes=VMEM_LIMIT),
        name="meta_proj",
    )(mt_pad, g_in, b_in, w_bf, bf_col)

    n_t = L // TOKEN_TILE
    n_sub = TOKEN_TILE // ATT_TILE
    n_ab = L // ATT_TILE
    ktiled = jax.ShapeDtypeStruct((B, n_ab, ATT_WIDTH, ATT_TILE), bf16)
    ktiled_spec = pl.BlockSpec((1, n_sub, ATT_WIDTH, ATT_TILE), lambda b, t: (b, t, 0, 0))
    qtiled = jax.ShapeDtypeStruct((B, n_t, ATT_WIDTH, Q_TILE), bf16)
    qtiled_spec = pl.BlockSpec((1, 1, ATT_WIDTH, Q_TILE), lambda b, t: (b, t, 0, 0))
    qT, kk, aug, vT, sgT, pmix = pl.pallas_call(
        _in_proj_kernel,
        grid=(B, n_t),
        in_specs=[
            pl.BlockSpec((1, TOKEN_TILE, D), lambda b, t: (b, t, 0)),
            _const_spec((1, D)), _const_spec((1, D)),
            _const_spec((IN_COLS, D)), _const_spec((LANES, 1)),
            _const_spec((len(POOL_WINDOWS), POOL_GROUP_WIDTH, POOL_GROUP_WIDTH)),
            _const_spec((1, D)), _const_spec((N_META, D)),
        ],
        out_specs=(qtiled_spec,
                   pl.BlockSpec((1, TOKEN_TILE, ATT_WIDTH), lambda b, t: (b, t, 0)),
                   pl.BlockSpec((1, TOKEN_TILE, LANES), lambda b, t: (b, t, 0)),
                   ktiled_spec, qtiled_spec,
                   pl.BlockSpec((1, TOKEN_TILE, D), lambda b, t: (b, t, 0))),
        out_shape=(qtiled, jax.ShapeDtypeStruct((B, L, ATT_WIDTH), bf16),
                   jax.ShapeDtypeStruct((B, L, LANES), bf16), ktiled, qtiled,
                   jax.ShapeDtypeStruct((B, L, D), bf16)),
        scratch_shapes=[pltpu.VMEM((HEADS, LANES), f32), pltpu.VMEM((N_META, D), f32)],
        compiler_params=pltpu.CompilerParams(
            dimension_semantics=("arbitrary", "arbitrary"), vmem_limit_bytes=VMEM_LIMIT),
        name="in_proj",
    )(x, g_in, b_in, w_bf, bf_col, wpool, pscale, umeta)

    gd = HEAD_GROUP * HEAD_DIM
    head_ktiled = pl.BlockSpec((1, n_ab, gd, ATT_TILE), lambda b, h: (b, 0, h, 0))
    head_qtiled = pl.BlockSpec((1, n_t, gd, Q_TILE), lambda b, h: (b, 0, h, 0))
    aT = pl.pallas_call(
        _attn_kernel,
        grid=(B, HEADS // HEAD_GROUP),
        in_specs=[
            head_qtiled,
            pl.BlockSpec((1, L, gd), lambda b, h: (b, 0, h)),
            pl.BlockSpec((1, L, LANES), lambda b, h: (b, 0, 0)),
            head_ktiled, head_qtiled,
            pl.BlockSpec((META_PAD, gd), lambda b, h: (0, h)),
            pl.BlockSpec((META_PAD, LANES), lambda b, h: (0, 0)),
            pl.BlockSpec((gd, META_PAD), lambda b, h: (h, 0)),
        ],
        out_specs=head_qtiled,
        out_shape=qtiled,
        scratch_shapes=[pltpu.VMEM((2, HEAD_GROUP, KEY_WIDTH, Q_TILE), bf16),
                        pltpu.VMEM((3, HEAD_GROUP, ATT_TILE, Q_TILE), f32),
                        pltpu.VMEM((2, HEAD_GROUP, N_META, Q_TILE), f32),
                        pltpu.VMEM((2, HEAD_GROUP, HEAD_DIM + ONES_ROWS, Q_TILE), f32)],
        compiler_params=pltpu.CompilerParams(
            dimension_semantics=("arbitrary", "arbitrary"), vmem_limit_bytes=VMEM_LIMIT),
        name="fox_attn",
    )(qT, kk, aug, vT, sgT, kmeta, augmeta, vTmeta)

    out = pl.pallas_call(
        _out_proj_kernel,
        grid=(B, n_t),
        in_specs=[
            pl.BlockSpec((1, TOKEN_TILE, D), lambda b, t: (b, t, 0)),
            _const_spec((1, D)), _const_spec((1, D)),
            qtiled_spec,
            pl.BlockSpec((1, TOKEN_TILE, D), lambda b, t: (b, t, 0)),
            _const_spec((D, ATT_WIDTH)), _const_spec((D, D)),
            _const_spec((1, D)), _const_spec((1, D)),
        ],
        out_specs=pl.BlockSpec((1, TOKEN_TILE, D), lambda b, t: (b, t, 0)),
        out_shape=jax.ShapeDtypeStruct((B, L, D), x.dtype),
        compiler_params=pltpu.CompilerParams(
            dimension_semantics=("arbitrary", "arbitrary"), vmem_limit_bytes=VMEM_LIMIT),
        name="out_proj",
    )(x, g_in, b_in, aT, pmix, waT, wp, g_out, b_out)
    return out
```

```python
import math

import jax
import jax.numpy as jnp
from jax import lax
from jax.experimental import pallas as pl
from jax.experimental.pallas import tpu as pltpu

D_MODEL = 1024
N_META = 16
HEADS = 16
HEAD_DIM = 64
ATT_WIDTH = HEADS * HEAD_DIM
POOL_WINDOWS = (2, 4, 8, 16)
POOL_GROUP_WIDTH = D_MODEL // len(POOL_WINDOWS)
LN_EPS = 1e-5
DEEPNORM_ALPHA = 2.0 ** 0.25
LOG2E = math.log2(math.e)
Q_SCALE = HEAD_DIM ** -0.5 * LOG2E

LANES = 128
PAIR_WIDTH = 2 * HEAD_DIM
KEY_WIDTH = PAIR_WIDTH + LANES
META_PAD = 128
MASKED_BIAS = 1e30
TOKEN_TILE = 512
OUT_TILE = 1024
PREP_ROWS = 256
CAST_ROWS = 560
O_Q, O_K, O_V, O_F = 0, ATT_WIDTH, 2 * ATT_WIDTH, 3 * ATT_WIDTH
O_G = O_F + HEADS
O_U = O_G + ATT_WIDTH
O_GP = O_U + D_MODEL
IN_COLS = O_GP + D_MODEL
ATT_TILE = 256
Q_TILE = 2 * ATT_TILE
ONES_ROWS = 16
HEAD_GROUP = 4
VMEM_LIMIT = 56 * 1024 * 1024

_NT = (((1,), (1,)), ((), ()))


def _layer_norm(x, g, b):
    mu = jnp.mean(x, axis=-1, keepdims=True)
    xc = x - mu
    var = jnp.mean(xc * xc, axis=-1, keepdims=True)
    return xc * lax.rsqrt(var + LN_EPS) * g + b


def _log_sigmoid(z):
    return jnp.minimum(z, 0.0) - jnp.log(1.0 + jnp.exp(-jnp.abs(z)))


def _silu(z):
    return z / (1.0 + jnp.exp(-z))


def _proj(w_ref, hb, lo, n):
    return lax.dot_general(hb, w_ref[lo:lo + n, :], _NT, preferred_element_type=jnp.float32)


def _proj_t(w_ref, hb, lo, n):
    return lax.dot_general(w_ref[lo:lo + n, :], hb, _NT, preferred_element_type=jnp.float32)


def _lane_cumsum(x):
    n = x.shape[-1]
    lane = lax.broadcasted_iota(jnp.int32, x.shape, x.ndim - 1)
    d = 1
    while d < n:
        x = x + jnp.where(lane >= d, pltpu.roll(x, d, x.ndim - 1), 0.0)
        d *= 2
    return x


def _bias_columns(c_rows):
    hi = c_rows.astype(jnp.bfloat16).astype(jnp.float32)
    r1 = c_rows - hi
    mid = r1.astype(jnp.bfloat16).astype(jnp.float32)
    lo = (r1 - mid).astype(jnp.bfloat16).astype(jnp.float32)
    packed = hi + pltpu.roll(mid, HEADS, 1) + pltpu.roll(lo, 2 * HEADS, 1)
    return packed.astype(jnp.bfloat16)


def _pool_mix(u_ext, u, wpool_ref, pscale, gate):
    outs = []
    for gi, w in enumerate(POOL_WINDOWS):
        sl = slice(gi * POOL_GROUP_WIDTH, (gi + 1) * POOL_GROUP_WIDTH)
        r = u_ext[:, sl]
        s = 1
        while s < w:
            r = r + pltpu.roll(r, s, 0)
            s *= 2
        d = r[N_META:, :] * (1.0 / w) - u[:, sl]
        outs.append(jnp.dot(d.astype(jnp.bfloat16), wpool_ref[gi],
                            preferred_element_type=jnp.float32))
    y = jnp.concatenate(outs, axis=-1) * pscale
    return y * gate


def _meta_kernel(mt_ref, g_ref, b_ref, w_ref, bf_ref,
                 kmeta_ref, augmeta_ref, vTmeta_ref, umeta_ref):
    hn = _layer_norm(mt_ref[...], g_ref[...], b_ref[...])
    hb = hn.astype(jnp.bfloat16)
    flT = _proj_t(w_ref, hb, O_F, LANES)
    row = lax.broadcasted_iota(jnp.int32, flT.shape, 0)
    col = lax.broadcasted_iota(jnp.int32, flT.shape, 1)
    valid = (row < HEADS) & (col < N_META)
    logf = jnp.where(valid, _log_sigmoid(flT + bf_ref[...]), 0.0)
    cT = _lane_cumsum(logf)
    total = cT[:, META_PAD - 1:META_PAD]
    c_rows = ((cT - total) * LOG2E).T
    trow = lax.broadcasted_iota(jnp.int32, c_rows.shape, 0)
    tcol = lax.broadcasted_iota(jnp.int32, c_rows.shape, 1)
    c_rows = jnp.where((trow >= N_META) & (tcol < HEADS), MASKED_BIAS, c_rows)
    augmeta_ref[...] = _bias_columns(c_rows)
    kmeta_ref[...] = _proj(w_ref, hb, O_K, ATT_WIDTH).astype(jnp.bfloat16)
    vTmeta_ref[...] = _proj_t(w_ref, hb, O_V, ATT_WIDTH).astype(jnp.bfloat16)
    u = _proj(w_ref, hb, O_U, D_MODEL)
    umeta_ref[...] = u[:N_META, :]


def _in_proj_kernel(x_ref, g_ref, b_ref, w_ref, bf_ref, wpool_ref, pscale_ref, umeta_ref,
                    qT_ref, k_ref, aug_ref, vT_ref, sgT_ref, p_ref, carry_c, carry_u):
    t = pl.program_id(1)

    @pl.when(t == 0)
    def _():
        carry_c[...] = jnp.zeros_like(carry_c)
        carry_u[...] = umeta_ref[...]

    hn = _layer_norm(x_ref[0], g_ref[...], b_ref[...])
    hb = hn.astype(jnp.bfloat16)

    qT = _proj_t(w_ref, hb, O_Q, ATT_WIDTH) * Q_SCALE
    qT_ref[0, 0] = qT.astype(jnp.bfloat16)

    vT = _proj_t(w_ref, hb, O_V, ATT_WIDTH)
    vTb = vT.astype(jnp.bfloat16)
    for j in range(TOKEN_TILE // ATT_TILE):
        vT_ref[0, j] = vTb[:, j * ATT_TILE:(j + 1) * ATT_TILE]

    gT = _proj_t(w_ref, hb, O_G, ATT_WIDTH)
    sgT_ref[0, 0] = _silu(gT).astype(jnp.bfloat16)

    flT = _proj_t(w_ref, hb, O_F, LANES)
    logf = _log_sigmoid(flT[:HEADS, :] + bf_ref[:HEADS, :])
    cT = _lane_cumsum(logf) + carry_c[:, 0:1]
    carry_c[...] = jnp.broadcast_to(cT[:, TOKEN_TILE - 1:TOKEN_TILE], carry_c.shape)
    cT_pad = jnp.concatenate(
        [cT * LOG2E, jnp.zeros((LANES - HEADS, TOKEN_TILE), jnp.float32)], axis=0)
    c_rows = cT_pad.T
    aug_ref[0] = _bias_columns(c_rows)
    k_ref[0] = _proj(w_ref, hb, O_K, ATT_WIDTH).astype(jnp.bfloat16)

    u = _proj(w_ref, hb, O_U, D_MODEL)
    gp = _proj(w_ref, hb, O_GP, D_MODEL)
    u_ext = jnp.concatenate([carry_u[...], u], axis=0)
    carry_u[...] = u[TOKEN_TILE - N_META:, :]
    p = _pool_mix(u_ext, u, wpool_ref, pscale_ref[...], _silu(gp))
    p_ref[0] = p.astype(jnp.bfloat16)


def _attn_kernel(qT_ref, k_ref, aug_ref, vT_ref, sgT_ref, kmeta_ref, augmeta_ref, vTmeta_ref, o_ref,
                 qa_ref, s_ref, smeta_ref, acc_ref):
    n_qb = qT_ref.shape[1]
    heads = range(HEAD_GROUP)
    sel_row = lax.broadcasted_iota(jnp.int32, (LANES, Q_TILE), 0)
    for g in heads:
        h = pl.program_id(1) * HEAD_GROUP + g
        picks = (sel_row == h) | (sel_row == HEADS + h) | (sel_row == 2 * HEADS + h)
        for par in range(2):
            qa_ref[par, g, :PAIR_WIDTH, :] = jnp.zeros((PAIR_WIDTH, Q_TILE), jnp.bfloat16)
            qa_ref[par, g, PAIR_WIDTH:, :] = jnp.where(picks, -1.0, 0.0).astype(jnp.bfloat16)
    ones = jnp.ones((ONES_ROWS, ATT_TILE), jnp.bfloat16)
    ones_meta = jnp.ones((ONES_ROWS, META_PAD), jnp.bfloat16)
    krow = lax.broadcasted_iota(jnp.int32, (ATT_TILE, Q_TILE), 0)
    qcol = lax.broadcasted_iota(jnp.int32, (ATT_TILE, Q_TILE), 1)
    causal_a = krow <= qcol
    causal_b = causal_a[:, :ATT_TILE]
    meta_fill = jnp.zeros((META_PAD - N_META, Q_TILE), jnp.bfloat16)
    acc_zero = jnp.zeros(acc_ref.shape[2:], jnp.float32)

    def hrows(g, width):
        return slice(g * width, (g + 1) * width)

    def colmax(s):
        return jnp.max(s, axis=0, keepdims=True)

    def key_tile(g, kj):
        rows = pl.ds(pl.multiple_of(kj * ATT_TILE, ATT_TILE), ATT_TILE)
        return jnp.concatenate([k_ref[0, rows, hrows(g // 2, PAIR_WIDTH)], aug_ref[0, rows, :]], axis=1)

    def value_tile(g, kj):
        return jnp.concatenate([vT_ref[0, kj, hrows(g, HEAD_DIM), :], ones], axis=0)

    def head_scores(g, kj, qa, dst):
        s = jnp.dot(key_tile(g, kj), qa[g], preferred_element_type=jnp.float32)
        dst[g] = s
        return colmax(s)

    def head_consume(g, kj, src, acc, smax, m):
        m_new = jnp.maximum(m, smax)
        alpha = jnp.exp2(m - m_new)
        p = jnp.exp2(src[g] - m_new).astype(jnp.bfloat16)
        acc[g] = alpha * acc[g] + jnp.dot(value_tile(g, kj), p, preferred_element_type=jnp.float32)
        return m_new

    def start_block(g, qi, qa, smeta, dst):
        qa[g, hrows(g % 2, HEAD_DIM), :] = qT_ref[0, qi, hrows(g, HEAD_DIM), :]
        kmeta = jnp.concatenate([kmeta_ref[:N_META, hrows(g // 2, PAIR_WIDTH)],
                                 augmeta_ref[:N_META, :]], axis=1)
        smeta[g] = jnp.dot(kmeta, qa[g], preferred_element_type=jnp.float32)
        return head_scores(g, 0, qa, dst)

    def block(qi, par, smax_first):
        qa, qa_nx = qa_ref.at[par], qa_ref.at[1 - par]
        smeta, smeta_nx = smeta_ref.at[par], smeta_ref.at[1 - par]
        acc, acc_nx = acc_ref.at[par], acc_ref.at[1 - par]
        s_first, s_other, s_nx = s_ref.at[2 * par], s_ref.at[1], s_ref.at[2 - 2 * par]
        m0 = tuple(jnp.full((1, Q_TILE), -jnp.inf, jnp.float32) for _ in heads)

        def step(k_next, dst, k_cur, src, smax_cur, ms):
            smax_next, ms_new = [], []
            for g in heads:
                smax_next.append(head_scores(g, k_next, qa, dst))
                ms_new.append(head_consume(g, k_cur, src, acc, smax_cur[g], ms[g]))
            return tuple(ms_new), tuple(smax_next)

        def pair(t, ms, smax):
            ms, smax_o = step(t + 1, s_other, t, s_first, smax, ms)
            return step(t + 2, s_first, t + 1, s_other, smax_o, ms)

        def quad(k, carry):
            return pair(4 * k + 2, *pair(4 * k, *carry))

        ms, smax = lax.fori_loop(0, qi // 2, quad, (m0, smax_first))
        if par == 1:
            ms, _ = pair(2 * qi - 2, ms, smax)

        q_next = jnp.minimum(qi + 1, n_qb - 1)
        sbs = [jnp.dot(key_tile(g, 2 * qi + 1), qa[g][:, ATT_TILE:],
                       preferred_element_type=jnp.float32) for g in heads]
        smax_next = [start_block(g, q_next, qa_nx, smeta_nx, s_nx) for g in heads]
        for g in heads:
            sb = jnp.where(causal_b, sbs[g], -jnp.inf)
            s = jnp.where(causal_a, s_first[g], -jnp.inf)
            sm = smeta[g]
            m_new = jnp.maximum(jnp.maximum(ms[g], colmax(s)), colmax(sm))
            m_new = jnp.concatenate(
                [m_new[:, :ATT_TILE], jnp.maximum(m_new[:, ATT_TILE:], colmax(sb))], axis=1)
            alpha = jnp.exp2(ms[g] - m_new)
            p = jnp.exp2(s - m_new).astype(jnp.bfloat16)
            pb = jnp.exp2(sb - m_new[:, ATT_TILE:]).astype(jnp.bfloat16)
            pm = jnp.concatenate([jnp.exp2(sm - m_new).astype(jnp.bfloat16), meta_fill], axis=0)
            acc_nx[g] = acc_zero
            vmeta = jnp.concatenate([vTmeta_ref[hrows(g, HEAD_DIM), :], ones_meta], axis=0)
            a = (alpha * acc[g]
                 + jnp.dot(value_tile(g, 2 * qi), p, preferred_element_type=jnp.float32)
                 + jnp.dot(vmeta, pm, preferred_element_type=jnp.float32))
            a_late = a[:, ATT_TILE:] + jnp.dot(value_tile(g, 2 * qi + 1), pb,
                                               preferred_element_type=jnp.float32)
            a = jnp.concatenate([a[:, :ATT_TILE], a_late], axis=1)
            o = a[:HEAD_DIM, :] / a[HEAD_DIM:HEAD_DIM + 1, :]
            o = o * sgT_ref[0, qi, hrows(g, HEAD_DIM), :].astype(jnp.float32)
            o_ref[0, qi, hrows(g, HEAD_DIM), :] = o.astype(jnp.bfloat16)
        return tuple(smax_next)

    def two_blocks(j, smax_first):
        return block(2 * j + 1, 1, block(2 * j, 0, smax_first))

    smax_first = []
    for g in heads:
        acc_ref[0, g] = acc_zero
        smax_first.append(start_block(g, 0, qa_ref.at[0], smeta_ref.at[0], s_ref.at[0]))
    lax.fori_loop(0, n_qb // 2, two_blocks, tuple(smax_first))


def _out_proj_kernel(x_ref, gin_ref, bin_ref, aT_ref, p_ref, waT_ref, wp_ref, g_ref, b_ref, o_ref):
    for j in range(OUT_TILE // Q_TILE):
        rows = slice(j * Q_TILE, (j + 1) * Q_TILE)
        hn = _layer_norm(x_ref[0, rows, :], gin_ref[...], bin_ref[...])
        yT = jnp.dot(waT_ref[...], aT_ref[0, j], preferred_element_type=jnp.float32)
        y = yT.T + jnp.dot(p_ref[0, rows, :], wp_ref[...], preferred_element_type=jnp.float32)
        o_ref[0, rows, :] = _layer_norm(DEEPNORM_ALPHA * hn + y, g_ref[...], b_ref[...])


def _cast_kernel(w_ref, o_ref):
    o_ref[...] = w_ref[...].astype(jnp.bfloat16)


def _out_weight_kernel(wa_ref, wpin_ref, waT_ref, wp_ref):
    waT_ref[...] = wa_ref[0].T.astype(jnp.bfloat16)
    wp_ref[...] = wpin_ref[0].astype(jnp.bfloat16)


def _const_spec(shape):
    nd = len(shape)
    return pl.BlockSpec(shape, lambda *_: (0,) * nd, pipeline_mode=pl.Buffered(1))


def kernel(x, meta_tokens, ln_in_g, ln_in_b, w_in, b_forget, w_pool, pool_scale, w_out, ln_g, ln_b):
    B, L, D = x.shape
    assert D == D_MODEL and L % (2 * Q_TILE) == 0 and TOKEN_TILE == Q_TILE and w_in.shape[0] == 1
    bf16, f32 = jnp.bfloat16, jnp.float32
    assert w_in.shape[2] == IN_COLS and w_out.shape[1] == 2 * D_MODEL

    w_t = jnp.swapaxes(w_in, 1, 2)[0]
    w_bf = pl.pallas_call(
        _cast_kernel,
        grid=(IN_COLS // CAST_ROWS,),
        in_specs=[pl.BlockSpec((CAST_ROWS, D), lambda r: (r, 0))],
        out_specs=pl.BlockSpec((CAST_ROWS, D), lambda r: (r, 0)),
        out_shape=jax.ShapeDtypeStruct((IN_COLS, D), bf16),
        compiler_params=pltpu.CompilerParams(dimension_semantics=("arbitrary",)),
        name="in_weight_cast",
    )(w_t)
    square = jax.ShapeDtypeStruct((D, D), bf16)
    waT, wp = pl.pallas_call(
        _out_weight_kernel,
        grid=(D // PREP_ROWS,),
        in_specs=[pl.BlockSpec((1, PREP_ROWS, D), lambda r: (0, r, 0)),
                  pl.BlockSpec((1, PREP_ROWS, D), lambda r: (0, r + D // PREP_ROWS, 0))],
        out_specs=(pl.BlockSpec((D, PREP_ROWS), lambda r: (0, r)),
                   pl.BlockSpec((PREP_ROWS, D), lambda r: (r, 0))),
        out_shape=(square, square),
        compiler_params=pltpu.CompilerParams(dimension_semantics=("arbitrary",)),
        name="out_weight_prep",
    )(w_out, w_out)
    bf_col = jnp.pad(b_forget[0].astype(f32), (0, LANES - HEADS)).reshape(LANES, 1)
    wpool = w_pool[0].astype(bf16)
    pscale = pool_scale[0].reshape(1, D).astype(f32)
    g_in = ln_in_g.reshape(1, D).astype(f32)
    b_in = ln_in_b.reshape(1, D).astype(f32)
    g_out = ln_g[0].reshape(1, D).astype(f32)
    b_out = ln_b[0].reshape(1, D).astype(f32)
    mt_pad = jnp.pad(meta_tokens.astype(f32), ((0, META_PAD - N_META), (0, 0)))

    kmeta, augmeta, vTmeta, umeta = pl.pallas_call(
        _meta_kernel,
        out_shape=(jax.ShapeDtypeStruct((META_PAD, ATT_WIDTH), bf16),
                   jax.ShapeDtypeStruct((META_PAD, LANES), bf16),
                   jax.ShapeDtypeStruct((ATT_WIDTH, META_PAD), bf16),
                   jax.ShapeDtypeStruct((N_META, D), f32)),
        compiler_params=pltpu.CompilerParams(vmem_limit_bytes=VMEM_LIMIT),
        name="meta_proj",
    )(mt_pad, g_in, b_in, w_bf, bf_col)

    n_t = L // TOKEN_TILE
    n_sub = TOKEN_TILE // ATT_TILE
    n_ab = L // ATT_TILE
    ktiled = jax.ShapeDtypeStruct((B, n_ab, ATT_WIDTH, ATT_TILE), bf16)
    ktiled_spec = pl.BlockSpec((1, n_sub, ATT_WIDTH, ATT_TILE), lambda b, t: (b, t, 0, 0))
    qtiled = jax.ShapeDtypeStruct((B, n_t, ATT_WIDTH, Q_TILE), bf16)
    qtiled_spec = pl.BlockSpec((1, 1, ATT_WIDTH, Q_TILE), lambda b, t: (b, t, 0, 0))
    qT, kk, aug, vT, sgT, pmix = pl.pallas_call(
        _in_proj_kernel,
        grid=(B, n_t),
        in_specs=[
            pl.BlockSpec((1, TOKEN_TILE, D), lambda b, t: (b, t, 0)),
            _const_spec((1, D)), _const_spec((1, D)),
            _const_spec((IN_COLS, D)), _const_spec((LANES, 1)),
            _const_spec((len(POOL_WINDOWS), POOL_GROUP_WIDTH, POOL_GROUP_WIDTH)),
            _const_spec((1, D)), _const_spec((N_META, D)),
        ],
        out_specs=(qtiled_spec,
                   pl.BlockSpec((1, TOKEN_TILE, ATT_WIDTH), lambda b, t: (b, t, 0)),
                   pl.BlockSpec((1, TOKEN_TILE, LANES), lambda b, t: (b, t, 0)),
                   ktiled_spec, qtiled_spec,
                   pl.BlockSpec((1, TOKEN_TILE, D), lambda b, t: (b, t, 0))),
        out_shape=(qtiled, jax.ShapeDtypeStruct((B, L, ATT_WIDTH), bf16),
                   jax.ShapeDtypeStruct((B, L, LANES), bf16), ktiled, qtiled,
                   jax.ShapeDtypeStruct((B, L, D), bf16)),
        scratch_shapes=[pltpu.VMEM((HEADS, LANES), f32), pltpu.VMEM((N_META, D), f32)],
        compiler_params=pltpu.CompilerParams(
            dimension_semantics=("arbitrary", "arbitrary"), vmem_limit_bytes=VMEM_LIMIT),
        name="in_proj",
    )(x, g_in, b_in, w_bf, bf_col, wpool, pscale, umeta)

    gd = HEAD_GROUP * HEAD_DIM
    head_ktiled = pl.BlockSpec((1, n_ab, gd, ATT_TILE), lambda b, h: (b, 0, h, 0))
    head_qtiled = pl.BlockSpec((1, n_t, gd, Q_TILE), lambda b, h: (b, 0, h, 0))
    aT = pl.pallas_call(
        _attn_kernel,
        grid=(B, HEADS // HEAD_GROUP),
        in_specs=[
            head_qtiled,
            pl.BlockSpec((1, L, gd), lambda b, h: (b, 0, h)),
            pl.BlockSpec((1, L, LANES), lambda b, h: (b, 0, 0)),
            head_ktiled, head_qtiled,
            pl.BlockSpec((META_PAD, gd), lambda b, h: (0, h)),
            pl.BlockSpec((META_PAD, LANES), lambda b, h: (0, 0)),
            pl.BlockSpec((gd, META_PAD), lambda b, h: (h, 0)),
        ],
        out_specs=head_qtiled,
        out_shape=qtiled,
        scratch_shapes=[pltpu.VMEM((2, HEAD_GROUP, KEY_WIDTH, Q_TILE), bf16),
                        pltpu.VMEM((3, HEAD_GROUP, ATT_TILE, Q_TILE), f32),
                        pltpu.VMEM((2, HEAD_GROUP, N_META, Q_TILE), f32),
                        pltpu.VMEM((2, HEAD_GROUP, HEAD_DIM + ONES_ROWS, Q_TILE), f32)],
        compiler_params=pltpu.CompilerParams(
            dimension_semantics=("arbitrary", "arbitrary"), vmem_limit_bytes=VMEM_LIMIT),
        name="fox_attn",
    )(qT, kk, aug, vT, sgT, kmeta, augmeta, vTmeta)

    out = pl.pallas_call(
        _out_proj_kernel,
        grid=(B, L // OUT_TILE),
        in_specs=[
            pl.BlockSpec((1, OUT_TILE, D), lambda b, t: (b, t, 0)),
            _const_spec((1, D)), _const_spec((1, D)),
            pl.BlockSpec((1, OUT_TILE // Q_TILE, ATT_WIDTH, Q_TILE), lambda b, t: (b, t, 0, 0)),
            pl.BlockSpec((1, OUT_TILE, D), lambda b, t: (b, t, 0)),
            _const_spec((D, ATT_WIDTH)), _const_spec((D, D)),
            _const_spec((1, D)), _const_spec((1, D)),
        ],
        out_specs=pl.BlockSpec((1, OUT_TILE, D), lambda b, t: (b, t, 0)),
        out_shape=jax.ShapeDtypeStruct((B, L, D), x.dtype),
        compiler_params=pltpu.CompilerParams(
            dimension_semantics=("arbitrary", "arbitrary"), vmem_limit_bytes=VMEM_LIMIT),
        name="out_proj",
    )(x, g_in, b_in, aT, pmix, waT, wp, g_out, b_out)
    return out
```

```python
import math

import jax
import jax.numpy as jnp
from jax import lax
from jax.experimental import pallas as pl
from jax.experimental.pallas import tpu as pltpu

D_MODEL = 1024
N_META = 16
HEADS = 16
HEAD_DIM = 64
ATT_WIDTH = HEADS * HEAD_DIM
POOL_WINDOWS = (2, 4, 8, 16)
POOL_GROUP_WIDTH = D_MODEL // len(POOL_WINDOWS)
LN_EPS = 1e-5
DEEPNORM_ALPHA = 2.0 ** 0.25
LOG2E = math.log2(math.e)
Q_SCALE = HEAD_DIM ** -0.5 * LOG2E

LANES = 128
PAIR_WIDTH = 2 * HEAD_DIM
KEY_WIDTH = PAIR_WIDTH + LANES
META_PAD = 128
MASKED_BIAS = 1e30
TOKEN_TILE = 1024
OUT_TILE = 1024
PREP_ROWS = 256
CAST_ROWS = 560
O_Q, O_K, O_V, O_F = 0, ATT_WIDTH, 2 * ATT_WIDTH, 3 * ATT_WIDTH
O_G = O_F + HEADS
O_U = O_G + ATT_WIDTH
O_GP = O_U + D_MODEL
IN_COLS = O_GP + D_MODEL
ATT_TILE = 256
Q_TILE = 2 * ATT_TILE
ONES_ROWS = 16
HEAD_GROUP = 4
VMEM_LIMIT = 56 * 1024 * 1024

_NT = (((1,), (1,)), ((), ()))


def _layer_norm(x, g, b):
    mu = jnp.mean(x, axis=-1, keepdims=True)
    xc = x - mu
    var = jnp.mean(xc * xc, axis=-1, keepdims=True)
    return xc * lax.rsqrt(var + LN_EPS) * g + b


def _log_sigmoid(z):
    return jnp.minimum(z, 0.0) - jnp.log(1.0 + jnp.exp(-jnp.abs(z)))


def _silu(z):
    return z / (1.0 + jnp.exp(-z))


def _proj(w_ref, hb, lo, n):
    return lax.dot_general(hb, w_ref[lo:lo + n, :], _NT, preferred_element_type=jnp.float32)


def _proj_t(w_ref, hb, lo, n):
    return lax.dot_general(w_ref[lo:lo + n, :], hb, _NT, preferred_element_type=jnp.float32)


def _lane_cumsum(x):
    n = x.shape[-1]
    lane = lax.broadcasted_iota(jnp.int32, x.shape, x.ndim - 1)
    d = 1
    while d < n:
        x = x + jnp.where(lane >= d, pltpu.roll(x, d, x.ndim - 1), 0.0)
        d *= 2
    return x


def _bias_columns(c_rows):
    hi = c_rows.astype(jnp.bfloat16).astype(jnp.float32)
    r1 = c_rows - hi
    mid = r1.astype(jnp.bfloat16).astype(jnp.float32)
    lo = (r1 - mid).astype(jnp.bfloat16).astype(jnp.float32)
    packed = hi + pltpu.roll(mid, HEADS, 1) + pltpu.roll(lo, 2 * HEADS, 1)
    return packed.astype(jnp.bfloat16)


def _pool_mix(u_ext, u, wpool_ref, pscale, gate):
    outs = []
    for gi, w in enumerate(POOL_WINDOWS):
        sl = slice(gi * POOL_GROUP_WIDTH, (gi + 1) * POOL_GROUP_WIDTH)
        r = u_ext[:, sl]
        s = 1
        while s < w:
            r = r + pltpu.roll(r, s, 0)
            s *= 2
        d = r[N_META:, :] * (1.0 / w) - u[:, sl]
        outs.append(jnp.dot(d.astype(jnp.bfloat16), wpool_ref[gi],
                            preferred_element_type=jnp.float32))
    y = jnp.concatenate(outs, axis=-1) * pscale
    return y * gate


def _meta_kernel(mt_ref, g_ref, b_ref, w_ref, bf_ref,
                 kmeta_ref, augmeta_ref, vTmeta_ref, umeta_ref):
    hn = _layer_norm(mt_ref[...], g_ref[...], b_ref[...])
    hb = hn.astype(jnp.bfloat16)
    flT = _proj_t(w_ref, hb, O_F, LANES)
    row = lax.broadcasted_iota(jnp.int32, flT.shape, 0)
    col = lax.broadcasted_iota(jnp.int32, flT.shape, 1)
    valid = (row < HEADS) & (col < N_META)
    logf = jnp.where(valid, _log_sigmoid(flT + bf_ref[...]), 0.0)
    cT = _lane_cumsum(logf)
    total = cT[:, META_PAD - 1:META_PAD]
    c_rows = ((cT - total) * LOG2E).T
    trow = lax.broadcasted_iota(jnp.int32, c_rows.shape, 0)
    tcol = lax.broadcasted_iota(jnp.int32, c_rows.shape, 1)
    c_rows = jnp.where((trow >= N_META) & (tcol < HEADS), MASKED_BIAS, c_rows)
    augmeta_ref[...] = _bias_columns(c_rows)
    kmeta_ref[...] = _proj(w_ref, hb, O_K, ATT_WIDTH).astype(jnp.bfloat16)
    vTmeta_ref[...] = _proj_t(w_ref, hb, O_V, ATT_WIDTH).astype(jnp.bfloat16)
    u = _proj(w_ref, hb, O_U, D_MODEL)
    umeta_ref[...] = u[:N_META, :]


def _in_proj_kernel(x_ref, g_ref, b_ref, w_ref, bf_ref, wpool_ref, pscale_ref, umeta_ref,
                    qT_ref, k_ref, aug_ref, vT_ref, sgT_ref, p_ref, carry_c, carry_u):
    t = pl.program_id(1)

    @pl.when(t == 0)
    def _():
        carry_c[...] = jnp.zeros_like(carry_c)
        carry_u[...] = umeta_ref[...]

    c_carry = carry_c[:, 0:1]
    u_carry = carry_u[...]
    sub = ATT_TILE
    per_q = Q_TILE // sub
    for h in range(TOKEN_TILE // sub):
        rows = slice(h * sub, (h + 1) * sub)
        qcols = slice((h % per_q) * sub, (h % per_q + 1) * sub)
        hn = _layer_norm(x_ref[0, rows, :], g_ref[...], b_ref[...])
        hb = hn.astype(jnp.bfloat16)

        qT = _proj_t(w_ref, hb, O_Q, ATT_WIDTH) * Q_SCALE
        qT_ref[0, h // per_q, :, qcols] = qT.astype(jnp.bfloat16)
        vT_ref[0, h] = _proj_t(w_ref, hb, O_V, ATT_WIDTH).astype(jnp.bfloat16)
        gT = _proj_t(w_ref, hb, O_G, ATT_WIDTH)
        sgT_ref[0, h // per_q, :, qcols] = _silu(gT).astype(jnp.bfloat16)

        flT = _proj_t(w_ref, hb, O_F, LANES)
        logf = _log_sigmoid(flT[:HEADS, :] + bf_ref[:HEADS, :])
        cT = _lane_cumsum(logf) + c_carry
        c_carry = cT[:, sub - 1:sub]
        cT_pad = jnp.concatenate(
            [cT * LOG2E, jnp.zeros((LANES - HEADS, sub), jnp.float32)], axis=0)
        aug_ref[0, rows, :] = _bias_columns(cT_pad.T)
        k_ref[0, rows, :] = _proj(w_ref, hb, O_K, ATT_WIDTH).astype(jnp.bfloat16)

        u = _proj(w_ref, hb, O_U, D_MODEL)
        gp = _proj(w_ref, hb, O_GP, D_MODEL)
        u_ext = jnp.concatenate([u_carry, u], axis=0)
        u_carry = u[sub - N_META:, :]
        p = _pool_mix(u_ext, u, wpool_ref, pscale_ref[...], _silu(gp))
        p_ref[0, rows, :] = p.astype(jnp.bfloat16)
    carry_c[...] = jnp.broadcast_to(c_carry, carry_c.shape)
    carry_u[...] = u_carry


def _attn_kernel(qT_ref, k_ref, aug_ref, vT_ref, sgT_ref, kmeta_ref, augmeta_ref, vTmeta_ref, o_ref,
                 qa_ref, s_ref, smeta_ref, acc_ref):
    n_qb = qT_ref.shape[1]
    heads = range(HEAD_GROUP)
    sel_row = lax.broadcasted_iota(jnp.int32, (LANES, Q_TILE), 0)
    for g in heads:
        h = pl.program_id(1) * HEAD_GROUP + g
        picks = (sel_row == h) | (sel_row == HEADS + h) | (sel_row == 2 * HEADS + h)
        for par in range(2):
            qa_ref[par, g, :PAIR_WIDTH, :] = jnp.zeros((PAIR_WIDTH, Q_TILE), jnp.bfloat16)
            qa_ref[par, g, PAIR_WIDTH:, :] = jnp.where(picks, -1.0, 0.0).astype(jnp.bfloat16)
    ones = jnp.ones((ONES_ROWS, ATT_TILE), jnp.bfloat16)
    ones_meta = jnp.ones((ONES_ROWS, META_PAD), jnp.bfloat16)
    krow = lax.broadcasted_iota(jnp.int32, (ATT_TILE, Q_TILE), 0)
    qcol = lax.broadcasted_iota(jnp.int32, (ATT_TILE, Q_TILE), 1)
    causal_a = krow <= qcol
    causal_b = causal_a[:, :ATT_TILE]
    meta_fill = jnp.zeros((META_PAD - N_META, Q_TILE), jnp.bfloat16)
    acc_zero = jnp.zeros(acc_ref.shape[2:], jnp.float32)

    def hrows(g, width):
        return slice(g * width, (g + 1) * width)

    def colmax(s):
        return jnp.max(s, axis=0, keepdims=True)

    def key_tile(g, kj):
        rows = pl.ds(pl.multiple_of(kj * ATT_TILE, ATT_TILE), ATT_TILE)
        return jnp.concatenate([k_ref[0, rows, hrows(g // 2, PAIR_WIDTH)], aug_ref[0, rows, :]], axis=1)

    def value_tile(g, kj):
        return jnp.concatenate([vT_ref[0, kj, hrows(g, HEAD_DIM), :], ones], axis=0)

    def head_scores(g, kj, qa, dst):
        s = jnp.dot(key_tile(g, kj), qa[g], preferred_element_type=jnp.float32)
        dst[g] = s
        return colmax(s)

    def head_consume(g, kj, src, acc, smax, m):
        m_new = jnp.maximum(m, smax)
        alpha = jnp.exp2(m - m_new)
        p = jnp.exp2(src[g] - m_new).astype(jnp.bfloat16)
        acc[g] = alpha * acc[g] + jnp.dot(value_tile(g, kj), p, preferred_element_type=jnp.float32)
        return m_new

    def start_block(g, qi, qa, smeta, dst):
        qa[g, hrows(g % 2, HEAD_DIM), :] = qT_ref[0, qi, hrows(g, HEAD_DIM), :]
        kmeta = jnp.concatenate([kmeta_ref[:N_META, hrows(g // 2, PAIR_WIDTH)],
                                 augmeta_ref[:N_META, :]], axis=1)
        smeta[g] = jnp.dot(kmeta, qa[g], preferred_element_type=jnp.float32)
        return head_scores(g, 0, qa, dst)

    def block(qi, par, smax_first):
        qa, qa_nx = qa_ref.at[par], qa_ref.at[1 - par]
        smeta, smeta_nx = smeta_ref.at[par], smeta_ref.at[1 - par]
        acc, acc_nx = acc_ref.at[par], acc_ref.at[1 - par]
        s_first, s_other, s_nx = s_ref.at[2 * par], s_ref.at[1], s_ref.at[2 - 2 * par]
        m0 = tuple(jnp.full((1, Q_TILE), -jnp.inf, jnp.float32) for _ in heads)

        def step(k_next, dst, k_cur, src, smax_cur, ms):
            smax_next, ms_new = [], []
            for g in heads:
                smax_next.append(head_scores(g, k_next, qa, dst))
                ms_new.append(head_consume(g, k_cur, src, acc, smax_cur[g], ms[g]))
            return tuple(ms_new), tuple(smax_next)

        def pair(t, ms, smax):
            ms, smax_o = step(t + 1, s_other, t, s_first, smax, ms)
            return step(t + 2, s_first, t + 1, s_other, smax_o, ms)

        def quad(k, carry):
            return pair(4 * k + 2, *pair(4 * k, *carry))

        ms, smax = lax.fori_loop(0, qi // 2, quad, (m0, smax_first))
        if par == 1:
            ms, _ = pair(2 * qi - 2, ms, smax)

        q_next = jnp.minimum(qi + 1, n_qb - 1)
        sbs = [jnp.dot(key_tile(g, 2 * qi + 1), qa[g][:, ATT_TILE:],
                       preferred_element_type=jnp.float32) for g in heads]
        smax_next = [start_block(g, q_next, qa_nx, smeta_nx, s_nx) for g in heads]
        for g in heads:
            sb = jnp.where(causal_b, sbs[g], -jnp.inf)
            s = jnp.where(causal_a, s_first[g], -jnp.inf)
            sm = smeta[g]
            m_new = jnp.maximum(jnp.maximum(ms[g], colmax(s)), colmax(sm))
            m_new = jnp.concatenate(
                [m_new[:, :ATT_TILE], jnp.maximum(m_new[:, ATT_TILE:], colmax(sb))], axis=1)
            alpha = jnp.exp2(ms[g] - m_new)
            p = jnp.exp2(s - m_new).astype(jnp.bfloat16)
            pb = jnp.exp2(sb - m_new[:, ATT_TILE:]).astype(jnp.bfloat16)
            pm = jnp.concatenate([jnp.exp2(sm - m_new).astype(jnp.bfloat16), meta_fill], axis=0)
            acc_nx[g] = acc_zero
            vmeta = jnp.concatenate([vTmeta_ref[hrows(g, HEAD_DIM), :], ones_meta], axis=0)
            a = (alpha * acc[g]
                 + jnp.dot(value_tile(g, 2 * qi), p, preferred_element_type=jnp.float32)
                 + jnp.dot(vmeta, pm, preferred_element_type=jnp.float32))
            a_late = a[:, ATT_TILE:] + jnp.dot(value_tile(g, 2 * qi + 1), pb,
                                               preferred_element_type=jnp.float32)
            a = jnp.concatenate([a[:, :ATT_TILE], a_late], axis=1)
            o = a[:HEAD_DIM, :] / a[HEAD_DIM:HEAD_DIM + 1, :]
            o = o * sgT_ref[0, qi, hrows(g, HEAD_DIM), :].astype(jnp.float32)
            o_ref[0, qi, hrows(g, HEAD_DIM), :] = o.astype(jnp.bfloat16)
        return tuple(smax_next)

    def two_blocks(j, smax_first):
        return block(2 * j + 1, 1, block(2 * j, 0, smax_first))

    smax_first = []
    for g in heads:
        acc_ref[0, g] = acc_zero
        smax_first.append(start_block(g, 0, qa_ref.at[0], smeta_ref.at[0], s_ref.at[0]))
    lax.fori_loop(0, n_qb // 2, two_blocks, tuple(smax_first))


def _out_proj_kernel(x_ref, gin_ref, bin_ref, aT_ref, p_ref, waT_ref, wp_ref, g_ref, b_ref, o_ref):
    for j in range(OUT_TILE // Q_TILE):
        rows = slice(j * Q_TILE, (j + 1) * Q_TILE)
        hn = _layer_norm(x_ref[0, rows, :], gin_ref[...], bin_ref[...])
        yT = jnp.dot(waT_ref[...], aT_ref[0, j], preferred_element_type=jnp.float32)
        y = yT.T + jnp.dot(p_ref[0, rows, :], wp_ref[...], preferred_element_type=jnp.float32)
        o_ref[0, rows, :] = _layer_norm(DEEPNORM_ALPHA * hn + y, g_ref[...], b_ref[...])


def _cast_kernel(w_ref, o_ref):
    o_ref[...] = w_ref[...].astype(jnp.bfloat16)


def _out_weight_kernel(wa_ref, wpin_ref, waT_ref, wp_ref):
    waT_ref[...] = wa_ref[0].T.astype(jnp.bfloat16)
    wp_ref[...] = wpin_ref[0].astype(jnp.bfloat16)


def _const_spec(shape):
    nd = len(shape)
    return pl.BlockSpec(shape, lambda *_: (0,) * nd, pipeline_mode=pl.Buffered(1))


def kernel(x, meta_tokens, ln_in_g, ln_in_b, w_in, b_forget, w_pool, pool_scale, w_out, ln_g, ln_b):
    B, L, D = x.shape
    assert D == D_MODEL and L % (2 * Q_TILE) == 0 and TOKEN_TILE % Q_TILE == 0 and w_in.shape[0] == 1
    bf16, f32 = jnp.bfloat16, jnp.float32
    assert w_in.shape[2] == IN_COLS and w_out.shape[1] == 2 * D_MODEL

    w_t = jnp.swapaxes(w_in, 1, 2)[0]
    w_bf = pl.pallas_call(
        _cast_kernel,
        grid=(IN_COLS // CAST_ROWS,),
        in_specs=[pl.BlockSpec((CAST_ROWS, D), lambda r: (r, 0))],
        out_specs=pl.BlockSpec((CAST_ROWS, D), lambda r: (r, 0)),
        out_shape=jax.ShapeDtypeStruct((IN_COLS, D), bf16),
        compiler_params=pltpu.CompilerParams(dimension_semantics=("arbitrary",)),
        name="in_weight_cast",
    )(w_t)
    square = jax.ShapeDtypeStruct((D, D), bf16)
    waT, wp = pl.pallas_call(
        _out_weight_kernel,
        grid=(D // PREP_ROWS,),
        in_specs=[pl.BlockSpec((1, PREP_ROWS, D), lambda r: (0, r, 0)),
                  pl.BlockSpec((1, PREP_ROWS, D), lambda r: (0, r + D // PREP_ROWS, 0))],
        out_specs=(pl.BlockSpec((D, PREP_ROWS), lambda r: (0, r)),
                   pl.BlockSpec((PREP_ROWS, D), lambda r: (r, 0))),
        out_shape=(square, square),
        compiler_params=pltpu.CompilerParams(dimension_semantics=("arbitrary",)),
        name="out_weight_prep",
    )(w_out, w_out)
    bf_col = jnp.pad(b_forget[0].astype(f32), (0, LANES - HEADS)).reshape(LANES, 1)
    wpool = w_pool[0].astype(bf16)
    pscale = pool_scale[0].reshape(1, D).astype(f32)
    g_in = ln_in_g.reshape(1, D).astype(f32)
    b_in = ln_in_b.reshape(1, D).astype(f32)
    g_out = ln_g[0].reshape(1, D).astype(f32)
    b_out = ln_b[0].reshape(1, D).astype(f32)
    mt_pad = jnp.pad(meta_tokens.astype(f32), ((0, META_PAD - N_META), (0, 0)))

    kmeta, augmeta, vTmeta, umeta = pl.pallas_call(
        _meta_kernel,
        out_shape=(jax.ShapeDtypeStruct((META_PAD, ATT_WIDTH), bf16),
                   jax.ShapeDtypeStruct((META_PAD, LANES), bf16),
                   jax.ShapeDtypeStruct((ATT_WIDTH, META_PAD), bf16),
                   jax.ShapeDtypeStruct((N_META, D), f32)),
        compiler_params=pltpu.CompilerParams(vmem_limit_bytes=VMEM_LIMIT),
        name="meta_proj",
    )(mt_pad, g_in, b_in, w_bf, bf_col)

    n_t = L // TOKEN_TILE
    n_sub = TOKEN_TILE // ATT_TILE
    n_ab = L // ATT_TILE
    ktiled = jax.ShapeDtypeStruct((B, n_ab, ATT_WIDTH, ATT_TILE), bf16)
    ktiled_spec = pl.BlockSpec((1, n_sub, ATT_WIDTH, ATT_TILE), lambda b, t: (b, t, 0, 0))
    n_qb = L // Q_TILE
    qtiled = jax.ShapeDtypeStruct((B, n_qb, ATT_WIDTH, Q_TILE), bf16)
    qtiled_spec = pl.BlockSpec((1, TOKEN_TILE // Q_TILE, ATT_WIDTH, Q_TILE), lambda b, t: (b, t, 0, 0))
    qT, kk, aug, vT, sgT, pmix = pl.pallas_call(
        _in_proj_kernel,
        grid=(B, n_t),
        in_specs=[
            pl.BlockSpec((1, TOKEN_TILE, D), lambda b, t: (b, t, 0)),
            _const_spec((1, D)), _const_spec((1, D)),
            _const_spec((IN_COLS, D)), _const_spec((LANES, 1)),
            _const_spec((len(POOL_WINDOWS), POOL_GROUP_WIDTH, POOL_GROUP_WIDTH)),
            _const_spec((1, D)), _const_spec((N_META, D)),
        ],
        out_specs=(qtiled_spec,
                   pl.BlockSpec((1, TOKEN_TILE, ATT_WIDTH), lambda b, t: (b, t, 0)),
                   pl.BlockSpec((1, TOKEN_TILE, LANES), lambda b, t: (b, t, 0)),
                   ktiled_spec, qtiled_spec,
                   pl.BlockSpec((1, TOKEN_TILE, D), lambda b, t: (b, t, 0))),
        out_shape=(qtiled, jax.ShapeDtypeStruct((B, L, ATT_WIDTH), bf16),
                   jax.ShapeDtypeStruct((B, L, LANES), bf16), ktiled, qtiled,
                   jax.ShapeDtypeStruct((B, L, D), bf16)),
        scratch_shapes=[pltpu.VMEM((HEADS, LANES), f32), pltpu.VMEM((N_META, D), f32)],
        compiler_params=pltpu.CompilerParams(
            dimension_semantics=("arbitrary", "arbitrary"), vmem_limit_bytes=VMEM_LIMIT),
        name="in_proj",
    )(x, g_in, b_in, w_bf, bf_col, wpool, pscale, umeta)

    gd = HEAD_GROUP * HEAD_DIM
    head_ktiled = pl.BlockSpec((1, n_ab, gd, ATT_TILE), lambda b, h: (b, 0, h, 0))
    head_qtiled = pl.BlockSpec((1, n_qb, gd, Q_TILE), lambda b, h: (b, 0, h, 0))
    aT = pl.pallas_call(
        _attn_kernel,
        grid=(B, HEADS // HEAD_GROUP),
        in_specs=[
            head_qtiled,
            pl.BlockSpec((1, L, gd), lambda b, h: (b, 0, h)),
            pl.BlockSpec((1, L, LANES), lambda b, h: (b, 0, 0)),
            head_ktiled, head_qtiled,
            pl.BlockSpec((META_PAD, gd), lambda b, h: (0, h)),
            pl.BlockSpec((META_PAD, LANES), lambda b, h: (0, 0)),
            pl.BlockSpec((gd, META_PAD), lambda b, h: (h, 0)),
        ],
        out_specs=head_qtiled,
        out_shape=qtiled,
        scratch_shapes=[pltpu.VMEM((2, HEAD_GROUP, KEY_WIDTH, Q_TILE), bf16),
                        pltpu.VMEM((3, HEAD_GROUP, ATT_TILE, Q_TILE), f32),
                        pltpu.VMEM((2, HEAD_GROUP, N_META, Q_TILE), f32),
                        pltpu.VMEM((2, HEAD_GROUP, HEAD_DIM + ONES_ROWS, Q_TILE), f32)],
        compiler_params=pltpu.CompilerParams(
            dimension_semantics=("arbitrary", "arbitrary"), vmem_limit_bytes=VMEM_LIMIT),
        name="fox_attn",
    )(qT, kk, aug, vT, sgT, kmeta, augmeta, vTmeta)

    out = pl.pallas_call(
        _out_proj_kernel,
        grid=(B, L // OUT_TILE),
        in_specs=[
            pl.BlockSpec((1, OUT_TILE, D), lambda b, t: (b, t, 0)),
            _const_spec((1, D)), _const_spec((1, D)),
            pl.BlockSpec((1, OUT_TILE // Q_TILE, ATT_WIDTH, Q_TILE), lambda b, t: (b, t, 0, 0)),
            pl.BlockSpec((1, OUT_TILE, D), lambda b, t: (b, t, 0)),
            _const_spec((D, ATT_WIDTH)), _const_spec((D, D)),
            _const_spec((1, D)), _const_spec((1, D)),
        ],
        out_specs=pl.BlockSpec((1, OUT_TILE, D), lambda b, t: (b, t, 0)),
        out_shape=jax.ShapeDtypeStruct((B, L, D), x.dtype),
        compiler_params=pltpu.CompilerParams(
            dimension_semantics=("arbitrary", "arbitrary"), vmem_limit_bytes=VMEM_LIMIT),
        name="out_proj",
    )(x, g_in, b_in, aT, pmix, waT, wp, g_out, b_out)
    return out
```

```python
import math

import jax
import jax.numpy as jnp
from jax import lax
from jax.experimental import pallas as pl
from jax.experimental.pallas import tpu as pltpu

D_MODEL = 1024
N_META = 16
HEADS = 16
HEAD_DIM = 64
ATT_WIDTH = HEADS * HEAD_DIM
POOL_WINDOWS = (2, 4, 8, 16)
POOL_GROUP_WIDTH = D_MODEL // len(POOL_WINDOWS)
LN_EPS = 1e-5
DEEPNORM_ALPHA = 2.0 ** 0.25
LOG2E = math.log2(math.e)
Q_SCALE = HEAD_DIM ** -0.5 * LOG2E

LANES = 128
PAIR_WIDTH = 2 * HEAD_DIM
KEY_WIDTH = LANES
META_PAD = 128
MASKED_BIAS = 1e30
TOKEN_TILE = 1024
OUT_TILE = 1024
PREP_ROWS = 256
CAST_ROWS = 560
O_Q, O_K, O_V, O_F = 0, ATT_WIDTH, 2 * ATT_WIDTH, 3 * ATT_WIDTH
O_G = O_F + HEADS
O_U = O_G + ATT_WIDTH
O_GP = O_U + D_MODEL
IN_COLS = O_GP + D_MODEL
ATT_TILE = 256
Q_TILE = 2 * ATT_TILE
ONES_ROWS = 16
HEAD_GROUP = 4
VMEM_LIMIT = 56 * 1024 * 1024

_NT = (((1,), (1,)), ((), ()))


def _layer_norm(x, g, b):
    mu = jnp.mean(x, axis=-1, keepdims=True)
    xc = x - mu
    var = jnp.mean(xc * xc, axis=-1, keepdims=True)
    return xc * lax.rsqrt(var + LN_EPS) * g + b


def _log_sigmoid(z):
    return jnp.minimum(z, 0.0) - jnp.log(1.0 + jnp.exp(-jnp.abs(z)))


def _silu(z):
    return z / (1.0 + jnp.exp(-z))


def _proj(w_ref, hb, lo, n):
    return lax.dot_general(hb, w_ref[lo:lo + n, :], _NT, preferred_element_type=jnp.float32)


def _proj_t(w_ref, hb, lo, n):
    return lax.dot_general(w_ref[lo:lo + n, :], hb, _NT, preferred_element_type=jnp.float32)


def _lane_cumsum(x):
    n = x.shape[-1]
    lane = lax.broadcasted_iota(jnp.int32, x.shape, x.ndim - 1)
    d = 1
    while d < n:
        x = x + jnp.where(lane >= d, pltpu.roll(x, d, x.ndim - 1), 0.0)
        d *= 2
    return x


def _bias_columns(c_rows):
    hi = c_rows.astype(jnp.bfloat16).astype(jnp.float32)
    r1 = c_rows - hi
    mid = r1.astype(jnp.bfloat16).astype(jnp.float32)
    lo = (r1 - mid).astype(jnp.bfloat16).astype(jnp.float32)
    return hi + pltpu.roll(mid, HEADS, 1) + pltpu.roll(lo, 2 * HEADS, 1)


def _key_operand(kk, bias_lanes):
    low = lax.broadcasted_iota(jnp.int32, bias_lanes.shape, 1) < HEAD_DIM
    upper = pltpu.roll(bias_lanes, HEAD_DIM, 1)
    parts = []
    for j in range(HEADS // 2):
        pair = kk[:, j * PAIR_WIDTH:(j + 1) * PAIR_WIDTH]
        parts.append(jnp.where(low, pair, upper))
        parts.append(jnp.where(low, pltpu.roll(pair, HEAD_DIM, 1), upper))
    return jnp.concatenate(parts, axis=1).astype(jnp.bfloat16)


def _pool_mix(u_ext, u, wpool_ref, pscale, gate):
    outs = []
    for gi, w in enumerate(POOL_WINDOWS):
        sl = slice(gi * POOL_GROUP_WIDTH, (gi + 1) * POOL_GROUP_WIDTH)
        r = u_ext[:, sl]
        s = 1
        while s < w:
            r = r + pltpu.roll(r, s, 0)
            s *= 2
        d = r[N_META:, :] * (1.0 / w) - u[:, sl]
        outs.append(jnp.dot(d.astype(jnp.bfloat16), wpool_ref[gi],
                            preferred_element_type=jnp.float32))
    y = jnp.concatenate(outs, axis=-1) * pscale
    return y * gate


def _meta_kernel(mt_ref, g_ref, b_ref, w_ref, bf_ref,
                 kmeta_ref, vTmeta_ref, umeta_ref):
    hn = _layer_norm(mt_ref[...], g_ref[...], b_ref[...])
    hb = hn.astype(jnp.bfloat16)
    flT = _proj_t(w_ref, hb, O_F, LANES)
    row = lax.broadcasted_iota(jnp.int32, flT.shape, 0)
    col = lax.broadcasted_iota(jnp.int32, flT.shape, 1)
    valid = (row < HEADS) & (col < N_META)
    logf = jnp.where(valid, _log_sigmoid(flT + bf_ref[...]), 0.0)
    cT = _lane_cumsum(logf)
    total = cT[:, META_PAD - 1:META_PAD]
    c_rows = ((cT - total) * LOG2E).T
    trow = lax.broadcasted_iota(jnp.int32, c_rows.shape, 0)
    tcol = lax.broadcasted_iota(jnp.int32, c_rows.shape, 1)
    c_rows = jnp.where((trow >= N_META) & (tcol < HEADS), MASKED_BIAS, c_rows)
    kmeta_ref[...] = _key_operand(_proj(w_ref, hb, O_K, ATT_WIDTH), _bias_columns(c_rows))
    vTmeta_ref[...] = _proj_t(w_ref, hb, O_V, ATT_WIDTH).astype(jnp.bfloat16)
    u = _proj(w_ref, hb, O_U, D_MODEL)
    umeta_ref[...] = u[:N_META, :]


def _in_proj_kernel(x_ref, g_ref, b_ref, w_ref, bf_ref, wpool_ref, pscale_ref, umeta_ref,
                    qT_ref, k_ref, vT_ref, sgT_ref, p_ref, carry_c, carry_u):
    t = pl.program_id(1)

    @pl.when(t == 0)
    def _():
        carry_c[...] = jnp.zeros_like(carry_c)
        carry_u[...] = umeta_ref[...]

    c_carry = carry_c[:, 0:1]
    u_carry = carry_u[...]
    sub = ATT_TILE
    per_q = Q_TILE // sub
    for h in range(TOKEN_TILE // sub):
        rows = slice(h * sub, (h + 1) * sub)
        qcols = slice((h % per_q) * sub, (h % per_q + 1) * sub)
        hn = _layer_norm(x_ref[0, rows, :], g_ref[...], b_ref[...])
        hb = hn.astype(jnp.bfloat16)

        qT = _proj_t(w_ref, hb, O_Q, ATT_WIDTH) * Q_SCALE
        qT_ref[0, h // per_q, :, qcols] = qT.astype(jnp.bfloat16)
        vT_ref[0, h] = _proj_t(w_ref, hb, O_V, ATT_WIDTH).astype(jnp.bfloat16)
        gT = _proj_t(w_ref, hb, O_G, ATT_WIDTH)
        sgT_ref[0, h // per_q, :, qcols] = _silu(gT).astype(jnp.bfloat16)

        flT = _proj_t(w_ref, hb, O_F, LANES)
        logf = _log_sigmoid(flT[:HEADS, :] + bf_ref[:HEADS, :])
        cT = _lane_cumsum(logf) + c_carry
        c_carry = cT[:, sub - 1:sub]
        cT_pad = jnp.concatenate(
            [cT * LOG2E, jnp.zeros((LANES - HEADS, sub), jnp.float32)], axis=0)
        k_ref[0, rows, :] = _key_operand(_proj(w_ref, hb, O_K, ATT_WIDTH), _bias_columns(cT_pad.T))

        u = _proj(w_ref, hb, O_U, D_MODEL)
        gp = _proj(w_ref, hb, O_GP, D_MODEL)
        u_ext = jnp.concatenate([u_carry, u], axis=0)
        u_carry = u[sub - N_META:, :]
        p = _pool_mix(u_ext, u, wpool_ref, pscale_ref[...], _silu(gp))
        p_ref[0, rows, :] = p.astype(jnp.bfloat16)
    carry_c[...] = jnp.broadcast_to(c_carry, carry_c.shape)
    carry_u[...] = u_carry


def _attn_kernel(qT_ref, k_ref, vT_ref, sgT_ref, kmeta_ref, vTmeta_ref, o_ref,
                 qa_ref, s_ref, smeta_ref, acc_ref):
    n_qb = qT_ref.shape[1]
    heads = range(HEAD_GROUP)
    sel_row = lax.broadcasted_iota(jnp.int32, (KEY_WIDTH - HEAD_DIM, Q_TILE), 0)
    for g in heads:
        h = pl.program_id(1) * HEAD_GROUP + g
        picks = (sel_row == h) | (sel_row == HEADS + h) | (sel_row == 2 * HEADS + h)
        for par in range(2):
            qa_ref[par, g, HEAD_DIM:, :] = jnp.where(picks, -1.0, 0.0).astype(jnp.bfloat16)
    ones = jnp.ones((ONES_ROWS, ATT_TILE), jnp.bfloat16)
    ones_meta = jnp.ones((ONES_ROWS, META_PAD), jnp.bfloat16)
    krow = lax.broadcasted_iota(jnp.int32, (ATT_TILE, Q_TILE), 0)
    qcol = lax.broadcasted_iota(jnp.int32, (ATT_TILE, Q_TILE), 1)
    causal_a = krow <= qcol
    causal_b = causal_a[:, :ATT_TILE]
    meta_fill = jnp.zeros((META_PAD - N_META, Q_TILE), jnp.bfloat16)
    acc_zero = jnp.zeros(acc_ref.shape[2:], jnp.float32)

    def hrows(g, width):
        return slice(g * width, (g + 1) * width)

    def colmax(s):
        return jnp.max(s, axis=0, keepdims=True)

    def key_tile(g, kj):
        rows = pl.ds(pl.multiple_of(kj * ATT_TILE, ATT_TILE), ATT_TILE)
        return k_ref[0, rows, hrows(g, KEY_WIDTH)]

    def value_tile(g, kj):
        return jnp.concatenate([vT_ref[0, kj, hrows(g, HEAD_DIM), :], ones], axis=0)

    def head_scores(g, kj, qa, dst):
        s = jnp.dot(key_tile(g, kj), qa[g], preferred_element_type=jnp.float32)
        dst[g] = s
        return colmax(s)

    def head_consume(g, kj, src, acc, smax, m):
        m_new = jnp.maximum(m, smax)
        alpha = jnp.exp2(m - m_new)
        p = jnp.exp2(src[g] - m_new).astype(jnp.bfloat16)
        acc[g] = alpha * acc[g] + jnp.dot(value_tile(g, kj), p, preferred_element_type=jnp.float32)
        return m_new

    def start_block(g, qi, qa, smeta, dst):
        qa[g, :HEAD_DIM, :] = qT_ref[0, qi, hrows(g, HEAD_DIM), :]
        smeta[g] = jnp.dot(kmeta_ref[:N_META, hrows(g, KEY_WIDTH)], qa[g],
                           preferred_element_type=jnp.float32)
        return head_scores(g, 0, qa, dst)

    def block(qi, par, smax_first):
        qa, qa_nx = qa_ref.at[par], qa_ref.at[1 - par]
        smeta, smeta_nx = smeta_ref.at[par], smeta_ref.at[1 - par]
        acc, acc_nx = acc_ref.at[par], acc_ref.at[1 - par]
        s_first, s_other, s_nx = s_ref.at[2 * par], s_ref.at[1], s_ref.at[2 - 2 * par]
        m0 = tuple(jnp.full((1, Q_TILE), -jnp.inf, jnp.float32) for _ in heads)

        def step(k_next, dst, k_cur, src, smax_cur, ms):
            smax_next, ms_new = [], []
            for g in heads:
                smax_next.append(head_scores(g, k_next, qa, dst))
                ms_new.append(head_consume(g, k_cur, src, acc, smax_cur[g], ms[g]))
            return tuple(ms_new), tuple(smax_next)

        def pair(t, ms, smax):
            ms, smax_o = step(t + 1, s_other, t, s_first, smax, ms)
            return step(t + 2, s_first, t + 1, s_other, smax_o, ms)

        def quad(k, carry):
            return pair(4 * k + 2, *pair(4 * k, *carry))

        ms, smax = lax.fori_loop(0, qi // 2, quad, (m0, smax_first))
        if par == 1:
            ms, _ = pair(2 * qi - 2, ms, smax)

        q_next = jnp.minimum(qi + 1, n_qb - 1)
        sbs = [jnp.dot(key_tile(g, 2 * qi + 1), qa[g][:, ATT_TILE:],
                       preferred_element_type=jnp.float32) for g in heads]
        smax_next = [start_block(g, q_next, qa_nx, smeta_nx, s_nx) for g in heads]
        for g in heads:
            sb = jnp.where(causal_b, sbs[g], -jnp.inf)
            s = jnp.where(causal_a, s_first[g], -jnp.inf)
            sm = smeta[g]
            m_new = jnp.maximum(jnp.maximum(ms[g], colmax(s)), colmax(sm))
            m_new = jnp.concatenate(
                [m_new[:, :ATT_TILE], jnp.maximum(m_new[:, ATT_TILE:], colmax(sb))], axis=1)
            alpha = jnp.exp2(ms[g] - m_new)
            p = jnp.exp2(s - m_new).astype(jnp.bfloat16)
            pb = jnp.exp2(sb - m_new[:, ATT_TILE:]).astype(jnp.bfloat16)
            pm = jnp.concatenate([jnp.exp2(sm - m_new).astype(jnp.bfloat16), meta_fill], axis=0)
            acc_nx[g] = acc_zero
            vmeta = jnp.concatenate([vTmeta_ref[hrows(g, HEAD_DIM), :], ones_meta], axis=0)
            a = (alpha * acc[g]
                 + jnp.dot(value_tile(g, 2 * qi), p, preferred_element_type=jnp.float32)
                 + jnp.dot(vmeta, pm, preferred_element_type=jnp.float32))
            a_late = a[:, ATT_TILE:] + jnp.dot(value_tile(g, 2 * qi + 1), pb,
                                               preferred_element_type=jnp.float32)
            a = jnp.concatenate([a[:, :ATT_TILE], a_late], axis=1)
            o = a[:HEAD_DIM, :] / a[HEAD_DIM:HEAD_DIM + 1, :]
            o = o * sgT_ref[0, qi, hrows(g, HEAD_DIM), :].astype(jnp.float32)
            o_ref[0, qi, hrows(g, HEAD_DIM), :] = o.astype(jnp.bfloat16)
        return tuple(smax_next)

    def two_blocks(j, smax_first):
        return block(2 * j + 1, 1, block(2 * j, 0, smax_first))

    smax_first = []
    for g in heads:
        acc_ref[0, g] = acc_zero
        smax_first.append(start_block(g, 0, qa_ref.at[0], smeta_ref.at[0], s_ref.at[0]))
    lax.fori_loop(0, n_qb // 2, two_blocks, tuple(smax_first))


def _out_proj_kernel(x_ref, gin_ref, bin_ref, aT_ref, p_ref, waT_ref, wp_ref, g_ref, b_ref, o_ref):
    for j in range(OUT_TILE // Q_TILE):
        rows = slice(j * Q_TILE, (j + 1) * Q_TILE)
        hn = _layer_norm(x_ref[0, rows, :], gin_ref[...], bin_ref[...])
        yT = jnp.dot(waT_ref[...], aT_ref[0, j], preferred_element_type=jnp.float32)
        y = yT.T + jnp.dot(p_ref[0, rows, :], wp_ref[...], preferred_element_type=jnp.float32)
        o_ref[0, rows, :] = _layer_norm(DEEPNORM_ALPHA * hn + y, g_ref[...], b_ref[...])


def _cast_kernel(w_ref, o_ref):
    o_ref[...] = w_ref[...].astype(jnp.bfloat16)


def _out_weight_kernel(wa_ref, wpin_ref, waT_ref, wp_ref):
    waT_ref[...] = wa_ref[0].T.astype(jnp.bfloat16)
    wp_ref[...] = wpin_ref[0].astype(jnp.bfloat16)


def _const_spec(shape):
    nd = len(shape)
    return pl.BlockSpec(shape, lambda *_: (0,) * nd, pipeline_mode=pl.Buffered(1))


def kernel(x, meta_tokens, ln_in_g, ln_in_b, w_in, b_forget, w_pool, pool_scale, w_out, ln_g, ln_b):
    B, L, D = x.shape
    assert D == D_MODEL and L % (2 * Q_TILE) == 0 and TOKEN_TILE % Q_TILE == 0 and w_in.shape[0] == 1
    bf16, f32 = jnp.bfloat16, jnp.float32
    assert w_in.shape[2] == IN_COLS and w_out.shape[1] == 2 * D_MODEL

    w_t = jnp.swapaxes(w_in, 1, 2)[0]
    w_bf = pl.pallas_call(
        _cast_kernel,
        grid=(IN_COLS // CAST_ROWS,),
        in_specs=[pl.BlockSpec((CAST_ROWS, D), lambda r: (r, 0))],
        out_specs=pl.BlockSpec((CAST_ROWS, D), lambda r: (r, 0)),
        out_shape=jax.ShapeDtypeStruct((IN_COLS, D), bf16),
        compiler_params=pltpu.CompilerParams(dimension_semantics=("arbitrary",)),
        name="in_weight_cast",
    )(w_t)
    square = jax.ShapeDtypeStruct((D, D), bf16)
    waT, wp = pl.pallas_call(
        _out_weight_kernel,
        grid=(D // PREP_ROWS,),
        in_specs=[pl.BlockSpec((1, PREP_ROWS, D), lambda r: (0, r, 0)),
                  pl.BlockSpec((1, PREP_ROWS, D), lambda r: (0, r + D // PREP_ROWS, 0))],
        out_specs=(pl.BlockSpec((D, PREP_ROWS), lambda r: (0, r)),
                   pl.BlockSpec((PREP_ROWS, D), lambda r: (r, 0))),
        out_shape=(square, square),
        compiler_params=pltpu.CompilerParams(dimension_semantics=("arbitrary",)),
        name="out_weight_prep",
    )(w_out, w_out)
    bf_col = jnp.pad(b_forget[0].astype(f32), (0, LANES - HEADS)).reshape(LANES, 1)
    wpool = w_pool[0].astype(bf16)
    pscale = pool_scale[0].reshape(1, D).astype(f32)
    g_in = ln_in_g.reshape(1, D).astype(f32)
    b_in = ln_in_b.reshape(1, D).astype(f32)
    g_out = ln_g[0].reshape(1, D).astype(f32)
    b_out = ln_b[0].reshape(1, D).astype(f32)
    mt_pad = jnp.pad(meta_tokens.astype(f32), ((0, META_PAD - N_META), (0, 0)))

    kw = HEADS * KEY_WIDTH
    kmeta, vTmeta, umeta = pl.pallas_call(
        _meta_kernel,
        out_shape=(jax.ShapeDtypeStruct((META_PAD, kw), bf16),
                   jax.ShapeDtypeStruct((ATT_WIDTH, META_PAD), bf16),
                   jax.ShapeDtypeStruct((N_META, D), f32)),
        compiler_params=pltpu.CompilerParams(vmem_limit_bytes=VMEM_LIMIT),
        name="meta_proj",
    )(mt_pad, g_in, b_in, w_bf, bf_col)

    n_t = L // TOKEN_TILE
    n_sub = TOKEN_TILE // ATT_TILE
    n_ab = L // ATT_TILE
    ktiled = jax.ShapeDtypeStruct((B, n_ab, ATT_WIDTH, ATT_TILE), bf16)
    ktiled_spec = pl.BlockSpec((1, n_sub, ATT_WIDTH, ATT_TILE), lambda b, t: (b, t, 0, 0))
    n_qb = L // Q_TILE
    qtiled = jax.ShapeDtypeStruct((B, n_qb, ATT_WIDTH, Q_TILE), bf16)
    qtiled_spec = pl.BlockSpec((1, TOKEN_TILE // Q_TILE, ATT_WIDTH, Q_TILE), lambda b, t: (b, t, 0, 0))
    qT, kk, vT, sgT, pmix = pl.pallas_call(
        _in_proj_kernel,
        grid=(B, n_t),
        in_specs=[
            pl.BlockSpec((1, TOKEN_TILE, D), lambda b, t: (b, t, 0)),
            _const_spec((1, D)), _const_spec((1, D)),
            _const_spec((IN_COLS, D)), _const_spec((LANES, 1)),
            _const_spec((len(POOL_WINDOWS), POOL_GROUP_WIDTH, POOL_GROUP_WIDTH)),
            _const_spec((1, D)), _const_spec((N_META, D)),
        ],
        out_specs=(qtiled_spec,
                   pl.BlockSpec((1, TOKEN_TILE, kw), lambda b, t: (b, t, 0)),
                   ktiled_spec, qtiled_spec,
                   pl.BlockSpec((1, TOKEN_TILE, D), lambda b, t: (b, t, 0))),
        out_shape=(qtiled, jax.ShapeDtypeStruct((B, L, kw), bf16), ktiled, qtiled,
                   jax.ShapeDtypeStruct((B, L, D), bf16)),
        scratch_shapes=[pltpu.VMEM((HEADS, LANES), f32), pltpu.VMEM((N_META, D), f32)],
        compiler_params=pltpu.CompilerParams(
            dimension_semantics=("arbitrary", "arbitrary"), vmem_limit_bytes=VMEM_LIMIT),
        name="in_proj",
    )(x, g_in, b_in, w_bf, bf_col, wpool, pscale, umeta)

    gd = HEAD_GROUP * HEAD_DIM
    head_ktiled = pl.BlockSpec((1, n_ab, gd, ATT_TILE), lambda b, h: (b, 0, h, 0))
    head_qtiled = pl.BlockSpec((1, n_qb, gd, Q_TILE), lambda b, h: (b, 0, h, 0))
    aT = pl.pallas_call(
        _attn_kernel,
        grid=(B, HEADS // HEAD_GROUP),
        in_specs=[
            head_qtiled,
            pl.BlockSpec((1, L, HEAD_GROUP * KEY_WIDTH), lambda b, h: (b, 0, h)),
            head_ktiled, head_qtiled,
            pl.BlockSpec((META_PAD, HEAD_GROUP * KEY_WIDTH), lambda b, h: (0, h)),
            pl.BlockSpec((gd, META_PAD), lambda b, h: (h, 0)),
        ],
        out_specs=head_qtiled,
        out_shape=qtiled,
        scratch_shapes=[pltpu.VMEM((2, HEAD_GROUP, KEY_WIDTH, Q_TILE), bf16),
                        pltpu.VMEM((3, HEAD_GROUP, ATT_TILE, Q_TILE), f32),
                        pltpu.VMEM((2, HEAD_GROUP, N_META, Q_TILE), f32),
                        pltpu.VMEM((2, HEAD_GROUP, HEAD_DIM + ONES_ROWS, Q_TILE), f32)],
        compiler_params=pltpu.CompilerParams(
            dimension_semantics=("arbitrary", "arbitrary"), vmem_limit_bytes=VMEM_LIMIT),
        name="fox_attn",
    )(qT, kk, vT, sgT, kmeta, vTmeta)

    out = pl.pallas_call(
        _out_proj_kernel,
        grid=(B, L // OUT_TILE),
        in_specs=[
            pl.BlockSpec((1, OUT_TILE, D), lambda b, t: (b, t, 0)),
            _const_spec((1, D)), _const_spec((1, D)),
            pl.BlockSpec((1, OUT_TILE // Q_TILE, ATT_WIDTH, Q_TILE), lambda b, t: (b, t, 0, 0)),
            pl.BlockSpec((1, OUT_TILE, D), lambda b, t: (b, t, 0)),
            _const_spec((D, ATT_WIDTH)), _const_spec((D, D)),
            _const_spec((1, D)), _const_spec((1, D)),
        ],
        out_specs=pl.BlockSpec((1, OUT_TILE, D), lambda b, t: (b, t, 0)),
        out_shape=jax.ShapeDtypeStruct((B, L, D), x.dtype),
        compiler_params=pltpu.CompilerParams(
            dimension_semantics=("arbitrary", "arbitrary"), vmem_limit_bytes=VMEM_LIMIT),
        name="out_proj",
    )(x, g_in, b_in, aT, pmix, waT, wp, g_out, b_out)
    return out
```

```python
import math

import jax
import jax.numpy as jnp
from jax import lax
from jax.experimental import pallas as pl
from jax.experimental.pallas import tpu as pltpu

D_MODEL = 1024
N_META = 16
HEADS = 16
HEAD_DIM = 64
ATT_WIDTH = HEADS * HEAD_DIM
POOL_WINDOWS = (2, 4, 8, 16)
POOL_GROUP_WIDTH = D_MODEL // len(POOL_WINDOWS)
LN_EPS = 1e-5
DEEPNORM_ALPHA = 2.0 ** 0.25
LOG2E = math.log2(math.e)
Q_SCALE = HEAD_DIM ** -0.5 * LOG2E

LANES = 128
PAIR_WIDTH = 2 * HEAD_DIM
KEY_WIDTH = LANES
META_PAD = 128
TOKEN_TILE = 1024
OUT_TILE = 1024
PREP_ROWS = 256
CAST_ROWS = 560
O_Q, O_K, O_V, O_F = 0, ATT_WIDTH, 2 * ATT_WIDTH, 3 * ATT_WIDTH
O_G = O_F + HEADS
O_U = O_G + ATT_WIDTH
O_GP = O_U + D_MODEL
IN_COLS = O_GP + D_MODEL
ATT_TILE = 256
Q_TILE = 2 * ATT_TILE
ONES_ROWS = 16
HEAD_GROUP = 4
VMEM_LIMIT = 56 * 1024 * 1024

_NT = (((1,), (1,)), ((), ()))


def _layer_norm(x, g, b):
    mu = jnp.mean(x, axis=-1, keepdims=True)
    xc = x - mu
    var = jnp.mean(xc * xc, axis=-1, keepdims=True)
    return xc * lax.rsqrt(var + LN_EPS) * g + b


def _log_sigmoid(z):
    return jnp.minimum(z, 0.0) - jnp.log(1.0 + jnp.exp(-jnp.abs(z)))


def _silu(z):
    return z / (1.0 + jnp.exp(-z))


def _proj(w_ref, hb, lo, n):
    return lax.dot_general(hb, w_ref[lo:lo + n, :], _NT, preferred_element_type=jnp.float32)


def _proj_t(w_ref, hb, lo, n):
    return lax.dot_general(w_ref[lo:lo + n, :], hb, _NT, preferred_element_type=jnp.float32)


def _lane_cumsum(x):
    n = x.shape[-1]
    lane = lax.broadcasted_iota(jnp.int32, x.shape, x.ndim - 1)
    d = 1
    while d < n:
        x = x + jnp.where(lane >= d, pltpu.roll(x, d, x.ndim - 1), 0.0)
        d *= 2
    return x


def _bias_columns(c_rows):
    hi = c_rows.astype(jnp.bfloat16).astype(jnp.float32)
    r1 = c_rows - hi
    mid = r1.astype(jnp.bfloat16).astype(jnp.float32)
    lo = (r1 - mid).astype(jnp.bfloat16).astype(jnp.float32)
    return hi + pltpu.roll(mid, HEADS, 1) + pltpu.roll(lo, 2 * HEADS, 1)


def _key_operand(kk, bias_lanes):
    low = lax.broadcasted_iota(jnp.int32, bias_lanes.shape, 1) < HEAD_DIM
    upper = pltpu.roll(bias_lanes, HEAD_DIM, 1)
    parts = []
    for j in range(HEADS // 2):
        pair = kk[:, j * PAIR_WIDTH:(j + 1) * PAIR_WIDTH]
        parts.append(jnp.where(low, pair, upper))
        parts.append(jnp.where(low, pltpu.roll(pair, HEAD_DIM, 1), upper))
    return jnp.concatenate(parts, axis=1).astype(jnp.bfloat16)


def _pool_mix(u_ext, u, wpool_ref, pscale, gate):
    outs = []
    for gi, w in enumerate(POOL_WINDOWS):
        sl = slice(gi * POOL_GROUP_WIDTH, (gi + 1) * POOL_GROUP_WIDTH)
        r = u_ext[:, sl]
        s = 1
        while s < w:
            r = r + pltpu.roll(r, s, 0)
            s *= 2
        d = r[N_META:, :] * (1.0 / w) - u[:, sl]
        outs.append(jnp.dot(d.astype(jnp.bfloat16), wpool_ref[gi],
                            preferred_element_type=jnp.float32))
    y = jnp.concatenate(outs, axis=-1) * pscale
    return y * gate


def _meta_kernel(mt_ref, g_ref, b_ref, w_ref, bf_ref,
                 kmeta_ref, vTmeta_ref, umeta_ref):
    hn = _layer_norm(mt_ref[...], g_ref[...], b_ref[...])
    hb = hn.astype(jnp.bfloat16)
    flT = _proj_t(w_ref, hb, O_F, LANES)
    row = lax.broadcasted_iota(jnp.int32, flT.shape, 0)
    col = lax.broadcasted_iota(jnp.int32, flT.shape, 1)
    valid = (row < HEADS) & (col < N_META)
    logf = jnp.where(valid, _log_sigmoid(flT + bf_ref[...]), 0.0)
    cT = _lane_cumsum(logf)
    total = cT[:, META_PAD - 1:META_PAD]
    c_rows = ((cT - total) * LOG2E).T
    kmeta_ref[...] = _key_operand(_proj(w_ref, hb, O_K, ATT_WIDTH), _bias_columns(c_rows))
    vTmeta_ref[...] = _proj_t(w_ref, hb, O_V, ATT_WIDTH).astype(jnp.bfloat16)
    u = _proj(w_ref, hb, O_U, D_MODEL)
    umeta_ref[...] = u[:N_META, :]


def _in_proj_kernel(x_ref, g_ref, b_ref, w_ref, bf_ref, wpool_ref, pscale_ref, umeta_ref,
                    qT_ref, k_ref, vT_ref, sgT_ref, p_ref, carry_c, carry_u):
    t = pl.program_id(1)

    @pl.when(t == 0)
    def _():
        carry_c[...] = jnp.zeros_like(carry_c)
        carry_u[...] = umeta_ref[...]

    c_carry = carry_c[:, 0:1]
    u_carry = carry_u[...]
    sub = ATT_TILE
    per_q = Q_TILE // sub
    for h in range(TOKEN_TILE // sub):
        rows = slice(h * sub, (h + 1) * sub)
        qcols = slice((h % per_q) * sub, (h % per_q + 1) * sub)
        hn = _layer_norm(x_ref[0, rows, :], g_ref[...], b_ref[...])
        hb = hn.astype(jnp.bfloat16)

        qT = _proj_t(w_ref, hb, O_Q, ATT_WIDTH) * Q_SCALE
        qT_ref[0, h // per_q, :, qcols] = qT.astype(jnp.bfloat16)
        vT_ref[0, h] = _proj_t(w_ref, hb, O_V, ATT_WIDTH).astype(jnp.bfloat16)
        gT = _proj_t(w_ref, hb, O_G, ATT_WIDTH)
        sgT_ref[0, h // per_q, :, qcols] = _silu(gT).astype(jnp.bfloat16)

        flT = _proj_t(w_ref, hb, O_F, LANES)
        logf = _log_sigmoid(flT[:HEADS, :] + bf_ref[:HEADS, :])
        cT = _lane_cumsum(logf) + c_carry
        c_carry = cT[:, sub - 1:sub]
        cT_pad = jnp.concatenate(
            [cT * LOG2E, jnp.zeros((LANES - HEADS, sub), jnp.float32)], axis=0)
        k_ref[0, rows, :] = _key_operand(_proj(w_ref, hb, O_K, ATT_WIDTH), _bias_columns(cT_pad.T))

        u = _proj(w_ref, hb, O_U, D_MODEL)
        gp = _proj(w_ref, hb, O_GP, D_MODEL)
        u_ext = jnp.concatenate([u_carry, u], axis=0)
        u_carry = u[sub - N_META:, :]
        p = _pool_mix(u_ext, u, wpool_ref, pscale_ref[...], _silu(gp))
        p_ref[0, rows, :] = p.astype(jnp.bfloat16)
    carry_c[...] = jnp.broadcast_to(c_carry, carry_c.shape)
    carry_u[...] = u_carry


def _attn_kernel(qT_ref, k_ref, vT_ref, sgT_ref, kmeta_ref, vTmeta_ref, o_ref,
                 qa_ref, s_ref, smeta_ref, acc_ref):
    n_qb = qT_ref.shape[1]
    heads = range(HEAD_GROUP)
    sel_row = lax.broadcasted_iota(jnp.int32, (KEY_WIDTH - HEAD_DIM, Q_TILE), 0)
    for g in heads:
        h = pl.program_id(1) * HEAD_GROUP + g
        picks = (sel_row == h) | (sel_row == HEADS + h) | (sel_row == 2 * HEADS + h)
        for par in range(2):
            qa_ref[par, g, HEAD_DIM:, :] = jnp.where(picks, -1.0, 0.0).astype(jnp.bfloat16)
    ones = jnp.ones((ONES_ROWS, ATT_TILE), jnp.bfloat16)
    ones_meta = jnp.ones((ONES_ROWS, META_PAD), jnp.bfloat16)
    krow = lax.broadcasted_iota(jnp.int32, (ATT_TILE, Q_TILE), 0)
    qcol = lax.broadcasted_iota(jnp.int32, (ATT_TILE, Q_TILE), 1)
    causal_a = krow <= qcol
    causal_b = causal_a[:, :ATT_TILE]
    meta_fill = jnp.zeros((META_PAD - N_META, Q_TILE), jnp.bfloat16)
    acc_zero = jnp.zeros(acc_ref.shape[2:], jnp.float32)

    def hrows(g, width):
        return slice(g * width, (g + 1) * width)

    def colmax(s):
        return jnp.max(s, axis=0, keepdims=True)

    def key_tile(g, kj):
        rows = pl.ds(pl.multiple_of(kj * ATT_TILE, ATT_TILE), ATT_TILE)
        return k_ref[0, rows, hrows(g, KEY_WIDTH)]

    def value_tile(g, kj):
        return jnp.concatenate([vT_ref[0, kj, hrows(g, HEAD_DIM), :], ones], axis=0)

    def head_scores(g, kj, qa, dst):
        s = jnp.dot(key_tile(g, kj), qa[g], preferred_element_type=jnp.float32)
        dst[g] = s
        return colmax(s)

    def head_consume(g, kj, src, acc, smax, m):
        m_new = jnp.maximum(m, smax)
        alpha = jnp.exp2(m - m_new)
        p = jnp.exp2(src[g] - m_new).astype(jnp.bfloat16)
        acc[g] = alpha * acc[g] + jnp.dot(value_tile(g, kj), p, preferred_element_type=jnp.float32)
        return m_new

    def start_block(g, qi, qa, smeta, dst):
        qa[g, :HEAD_DIM, :] = qT_ref[0, qi, hrows(g, HEAD_DIM), :]
        smeta[g] = jnp.dot(kmeta_ref[:N_META, hrows(g, KEY_WIDTH)], qa[g],
                           preferred_element_type=jnp.float32)
        return head_scores(g, 0, qa, dst)

    def block(qi, par, smax_first):
        qa, qa_nx = qa_ref.at[par], qa_ref.at[1 - par]
        smeta, smeta_nx = smeta_ref.at[par], smeta_ref.at[1 - par]
        acc, acc_nx = acc_ref.at[par], acc_ref.at[1 - par]
        s_first, s_other, s_nx = s_ref.at[2 * par], s_ref.at[1], s_ref.at[2 - 2 * par]
        m0 = tuple(jnp.full((1, Q_TILE), -jnp.inf, jnp.float32) for _ in heads)

        def step(k_next, dst, k_cur, src, smax_cur, ms):
            smax_next, ms_new = [], []
            for g in heads:
                smax_next.append(head_scores(g, k_next, qa, dst))
                ms_new.append(head_consume(g, k_cur, src, acc, smax_cur[g], ms[g]))
            return tuple(ms_new), tuple(smax_next)

        def pair(t, ms, smax):
            ms, smax_o = step(t + 1, s_other, t, s_first, smax, ms)
            return step(t + 2, s_first, t + 1, s_other, smax_o, ms)

        def quad(k, carry):
            return pair(4 * k + 2, *pair(4 * k, *carry))

        ms, smax = lax.fori_loop(0, qi // 2, quad, (m0, smax_first))
        if par == 1:
            ms, _ = pair(2 * qi - 2, ms, smax)

        q_next = jnp.minimum(qi + 1, n_qb - 1)
        sbs = [jnp.dot(key_tile(g, 2 * qi + 1), qa[g][:, ATT_TILE:],
                       preferred_element_type=jnp.float32) for g in heads]
        smax_next = [start_block(g, q_next, qa_nx, smeta_nx, s_nx) for g in heads]
        for g in heads:
            sb = jnp.where(causal_b, sbs[g], -jnp.inf)
            s = jnp.where(causal_a, s_first[g], -jnp.inf)
            sm = smeta[g]
            m_new = jnp.maximum(jnp.maximum(ms[g], colmax(s)), colmax(sm))
            m_new = jnp.concatenate(
                [m_new[:, :ATT_TILE], jnp.maximum(m_new[:, ATT_TILE:], colmax(sb))], axis=1)
            alpha = jnp.exp2(ms[g] - m_new)
            p = jnp.exp2(s - m_new).astype(jnp.bfloat16)
            pb = jnp.exp2(sb - m_new[:, ATT_TILE:]).astype(jnp.bfloat16)
            pm = jnp.concatenate([jnp.exp2(sm - m_new).astype(jnp.bfloat16), meta_fill], axis=0)
            acc_nx[g] = acc_zero
            vmeta = jnp.concatenate([vTmeta_ref[hrows(g, HEAD_DIM), :], ones_meta], axis=0)
            a = (alpha * acc[g]
                 + jnp.dot(value_tile(g, 2 * qi), p, preferred_element_type=jnp.float32)
                 + jnp.dot(vmeta, pm, preferred_element_type=jnp.float32))
            a_late = a[:, ATT_TILE:] + jnp.dot(value_tile(g, 2 * qi + 1), pb,
                                               preferred_element_type=jnp.float32)
            a = jnp.concatenate([a[:, :ATT_TILE], a_late], axis=1)
            o = a[:HEAD_DIM, :] / a[HEAD_DIM:HEAD_DIM + 1, :]
            o = o * sgT_ref[0, qi, hrows(g, HEAD_DIM), :].astype(jnp.float32)
            o_ref[0, qi, hrows(g, HEAD_DIM), :] = o.astype(jnp.bfloat16)
        return tuple(smax_next)

    def two_blocks(j, smax_first):
        return block(2 * j + 1, 1, block(2 * j, 0, smax_first))

    smax_first = []
    for g in heads:
        acc_ref[0, g] = acc_zero
        smax_first.append(start_block(g, 0, qa_ref.at[0], smeta_ref.at[0], s_ref.at[0]))
    lax.fori_loop(0, n_qb // 2, two_blocks, tuple(smax_first))


def _out_proj_kernel(x_ref, gin_ref, bin_ref, aT_ref, p_ref, waT_ref, wp_ref, g_ref, b_ref, o_ref):
    for j in range(OUT_TILE // Q_TILE):
        rows = slice(j * Q_TILE, (j + 1) * Q_TILE)
        hn = _layer_norm(x_ref[0, rows, :], gin_ref[...], bin_ref[...])
        yT = jnp.dot(waT_ref[...], aT_ref[0, j], preferred_element_type=jnp.float32)
        y = yT.T + jnp.dot(p_ref[0, rows, :], wp_ref[...], preferred_element_type=jnp.float32)
        o_ref[0, rows, :] = _layer_norm(DEEPNORM_ALPHA * hn + y, g_ref[...], b_ref[...])


def _cast_kernel(w_ref, o_ref):
    o_ref[...] = w_ref[...].astype(jnp.bfloat16)


def _out_weight_kernel(wa_ref, wpin_ref, waT_ref, wp_ref):
    waT_ref[...] = wa_ref[0].T.astype(jnp.bfloat16)
    wp_ref[...] = wpin_ref[0].astype(jnp.bfloat16)


def _const_spec(shape):
    nd = len(shape)
    return pl.BlockSpec(shape, lambda *_: (0,) * nd, pipeline_mode=pl.Buffered(1))


def kernel(x, meta_tokens, ln_in_g, ln_in_b, w_in, b_forget, w_pool, pool_scale, w_out, ln_g, ln_b):
    B, L, D = x.shape
    assert D == D_MODEL and L % (2 * Q_TILE) == 0 and TOKEN_TILE % Q_TILE == 0 and w_in.shape[0] == 1
    bf16, f32 = jnp.bfloat16, jnp.float32
    assert w_in.shape[2] == IN_COLS and w_out.shape[1] == 2 * D_MODEL

    w_t = jnp.swapaxes(w_in, 1, 2)[0]
    w_bf = pl.pallas_call(
        _cast_kernel,
        grid=(IN_COLS // CAST_ROWS,),
        in_specs=[pl.BlockSpec((CAST_ROWS, D), lambda r: (r, 0))],
        out_specs=pl.BlockSpec((CAST_ROWS, D), lambda r: (r, 0)),
        out_shape=jax.ShapeDtypeStruct((IN_COLS, D), bf16),
        compiler_params=pltpu.CompilerParams(dimension_semantics=("arbitrary",)),
        name="in_weight_cast",
    )(w_t)
    square = jax.ShapeDtypeStruct((D, D), bf16)
    waT, wp = pl.pallas_call(
        _out_weight_kernel,
        grid=(D // PREP_ROWS,),
        in_specs=[pl.BlockSpec((1, PREP_ROWS, D), lambda r: (0, r, 0)),
                  pl.BlockSpec((1, PREP_ROWS, D), lambda r: (0, r + D // PREP_ROWS, 0))],
        out_specs=(pl.BlockSpec((D, PREP_ROWS), lambda r: (0, r)),
                   pl.BlockSpec((PREP_ROWS, D), lambda r: (r, 0))),
        out_shape=(square, square),
        compiler_params=pltpu.CompilerParams(dimension_semantics=("arbitrary",)),
        name="out_weight_prep",
    )(w_out, w_out)
    bf_col = jnp.pad(b_forget[0].astype(f32), (0, LANES - HEADS)).reshape(LANES, 1)
    wpool = w_pool[0].astype(bf16)
    pscale = pool_scale[0].reshape(1, D).astype(f32)
    g_in = ln_in_g.reshape(1, D).astype(f32)
    b_in = ln_in_b.reshape(1, D).astype(f32)
    g_out = ln_g[0].reshape(1, D).astype(f32)
    b_out = ln_b[0].reshape(1, D).astype(f32)
    mt_pad = jnp.pad(meta_tokens.astype(f32), ((0, META_PAD - N_META), (0, 0)))

    kw = HEADS * KEY_WIDTH
    kmeta, vTmeta, umeta = pl.pallas_call(
        _meta_kernel,
        out_shape=(jax.ShapeDtypeStruct((META_PAD, kw), bf16),
                   jax.ShapeDtypeStruct((ATT_WIDTH, META_PAD), bf16),
                   jax.ShapeDtypeStruct((N_META, D), f32)),
        compiler_params=pltpu.CompilerParams(vmem_limit_bytes=VMEM_LIMIT),
        name="meta_proj",
    )(mt_pad, g_in, b_in, w_bf, bf_col)

    n_t = L // TOKEN_TILE
    n_sub = TOKEN_TILE // ATT_TILE
    n_ab = L // ATT_TILE
    ktiled = jax.ShapeDtypeStruct((B, n_ab, ATT_WIDTH, ATT_TILE), bf16)
    ktiled_spec = pl.BlockSpec((1, n_sub, ATT_WIDTH, ATT_TILE), lambda b, t: (b, t, 0, 0))
    n_qb = L // Q_TILE
    qtiled = jax.ShapeDtypeStruct((B, n_qb, ATT_WIDTH, Q_TILE), bf16)
    qtiled_spec = pl.BlockSpec((1, TOKEN_TILE // Q_TILE, ATT_WIDTH, Q_TILE), lambda b, t: (b, t, 0, 0))
    qT, kk, vT, sgT, pmix = pl.pallas_call(
        _in_proj_kernel,
        grid=(B, n_t),
        in_specs=[
            pl.BlockSpec((1, TOKEN_TILE, D), lambda b, t: (b, t, 0)),
            _const_spec((1, D)), _const_spec((1, D)),
            _const_spec((IN_COLS, D)), _const_spec((LANES, 1)),
            _const_spec((len(POOL_WINDOWS), POOL_GROUP_WIDTH, POOL_GROUP_WIDTH)),
            _const_spec((1, D)), _const_spec((N_META, D)),
        ],
        out_specs=(qtiled_spec,
                   pl.BlockSpec((1, TOKEN_TILE, kw), lambda b, t: (b, t, 0)),
                   ktiled_spec, qtiled_spec,
                   pl.BlockSpec((1, TOKEN_TILE, D), lambda b, t: (b, t, 0))),
        out_shape=(qtiled, jax.ShapeDtypeStruct((B, L, kw), bf16), ktiled, qtiled,
                   jax.ShapeDtypeStruct((B, L, D), bf16)),
        scratch_shapes=[pltpu.VMEM((HEADS, LANES), f32), pltpu.VMEM((N_META, D), f32)],
        compiler_params=pltpu.CompilerParams(
            dimension_semantics=("arbitrary", "arbitrary"), vmem_limit_bytes=VMEM_LIMIT),
        name="in_proj",
    )(x, g_in, b_in, w_bf, bf_col, wpool, pscale, umeta)

    gd = HEAD_GROUP * HEAD_DIM
    head_ktiled = pl.BlockSpec((1, n_ab, gd, ATT_TILE), lambda b, h: (b, 0, h, 0))
    head_qtiled = pl.BlockSpec((1, n_qb, gd, Q_TILE), lambda b, h: (b, 0, h, 0))
    aT = pl.pallas_call(
        _attn_kernel,
        grid=(B, HEADS // HEAD_GROUP),
        in_specs=[
            head_qtiled,
            pl.BlockSpec((1, L, HEAD_GROUP * KEY_WIDTH), lambda b, h: (b, 0, h)),
            head_ktiled, head_qtiled,
            pl.BlockSpec((META_PAD, HEAD_GROUP * KEY_WIDTH), lambda b, h: (0, h)),
            pl.BlockSpec((gd, META_PAD), lambda b, h: (h, 0)),
        ],
        out_specs=head_qtiled,
        out_shape=qtiled,
        scratch_shapes=[pltpu.VMEM((2, HEAD_GROUP, KEY_WIDTH, Q_TILE), bf16),
                        pltpu.VMEM((3, HEAD_GROUP, ATT_TILE, Q_TILE), f32),
                        pltpu.VMEM((2, HEAD_GROUP, N_META, Q_TILE), f32),
                        pltpu.VMEM((2, HEAD_GROUP, HEAD_DIM + ONES_ROWS, Q_TILE), f32)],
        compiler_params=pltpu.CompilerParams(
            dimension_semantics=("arbitrary", "arbitrary"), vmem_limit_bytes=VMEM_LIMIT),
        name="fox_attn",
    )(qT, kk, vT, sgT, kmeta, vTmeta)

    out = pl.pallas_call(
        _out_proj_kernel,
        grid=(B, L // OUT_TILE),
        in_specs=[
            pl.BlockSpec((1, OUT_TILE, D), lambda b, t: (b, t, 0)),
            _const_spec((1, D)), _const_spec((1, D)),
            pl.BlockSpec((1, OUT_TILE // Q_TILE, ATT_WIDTH, Q_TILE), lambda b, t: (b, t, 0, 0)),
            pl.BlockSpec((1, OUT_TILE, D), lambda b, t: (b, t, 0)),
            _const_spec((D, ATT_WIDTH)), _const_spec((D, D)),
            _const_spec((1, D)), _const_spec((1, D)),
        ],
        out_specs=pl.BlockSpec((1, OUT_TILE, D), lambda b, t: (b, t, 0)),
        out_shape=jax.ShapeDtypeStruct((B, L, D), x.dtype),
        compiler_params=pltpu.CompilerParams(
            dimension_semantics=("arbitrary", "arbitrary"), vmem_limit_bytes=VMEM_LIMIT),
        name="out_proj",
    )(x, g_in, b_in, aT, pmix, waT, wp, g_out, b_out)
    return out
```

```python
import math

import jax
import jax.numpy as jnp
from jax import lax
from jax.experimental import pallas as pl
from jax.experimental.pallas import tpu as pltpu

D_MODEL = 1024
N_META = 16
HEADS = 16
HEAD_DIM = 64
ATT_WIDTH = HEADS * HEAD_DIM
POOL_WINDOWS = (2, 4, 8, 16)
POOL_GROUP_WIDTH = D_MODEL // len(POOL_WINDOWS)
LN_EPS = 1e-5
DEEPNORM_ALPHA = 2.0 ** 0.25
LOG2E = math.log2(math.e)
Q_SCALE = HEAD_DIM ** -0.5 * LOG2E

LANES = 128
PAIR_WIDTH = 2 * HEAD_DIM
KEY_WIDTH = LANES
META_PAD = 128
TOKEN_TILE = 1024
OUT_TILE = 1024
PREP_ROWS = 256
CAST_ROWS = 560
O_Q, O_K, O_V, O_F = 0, ATT_WIDTH, 2 * ATT_WIDTH, 3 * ATT_WIDTH
O_G = O_F + HEADS
O_U = O_G + ATT_WIDTH
O_GP = O_U + D_MODEL
IN_COLS = O_GP + D_MODEL
ATT_TILE = 256
Q_TILE = 2 * ATT_TILE
ONES_ROWS = 16
HEAD_GROUP = 4
VMEM_LIMIT = 56 * 1024 * 1024

_NT = (((1,), (1,)), ((), ()))


def _layer_norm(x, g, b):
    mu = jnp.mean(x, axis=-1, keepdims=True)
    xc = x - mu
    var = jnp.mean(xc * xc, axis=-1, keepdims=True)
    return xc * lax.rsqrt(var + LN_EPS) * g + b


def _log_sigmoid(z):
    return jnp.minimum(z, 0.0) - jnp.log(1.0 + jnp.exp(-jnp.abs(z)))


def _silu(z):
    return z / (1.0 + jnp.exp(-z))


def _proj(w_ref, hb, lo, n):
    return lax.dot_general(hb, w_ref[lo:lo + n, :], _NT, preferred_element_type=jnp.float32)


def _proj_t(w_ref, hb, lo, n):
    return lax.dot_general(w_ref[lo:lo + n, :], hb, _NT, preferred_element_type=jnp.float32)


def _lane_cumsum(x):
    n = x.shape[-1]
    lane = lax.broadcasted_iota(jnp.int32, x.shape, x.ndim - 1)
    d = 1
    while d < n:
        x = x + jnp.where(lane >= d, pltpu.roll(x, d, x.ndim - 1), 0.0)
        d *= 2
    return x


def _bias_columns(c_rows):
    hi = c_rows.astype(jnp.bfloat16).astype(jnp.float32)
    r1 = c_rows - hi
    mid = r1.astype(jnp.bfloat16).astype(jnp.float32)
    lo = (r1 - mid).astype(jnp.bfloat16).astype(jnp.float32)
    return hi + pltpu.roll(mid, HEADS, 1) + pltpu.roll(lo, 2 * HEADS, 1)


def _key_operand(kk, bias_lanes):
    low = lax.broadcasted_iota(jnp.int32, bias_lanes.shape, 1) < HEAD_DIM
    upper = pltpu.roll(bias_lanes, HEAD_DIM, 1)
    parts = []
    for j in range(HEADS // 2):
        pair = kk[:, j * PAIR_WIDTH:(j + 1) * PAIR_WIDTH]
        parts.append(jnp.where(low, pair, upper))
        parts.append(jnp.where(low, pltpu.roll(pair, HEAD_DIM, 1), upper))
    return jnp.concatenate(parts, axis=1).astype(jnp.bfloat16)


def _pool_mix(u_ext, u, wpool_ref, pscale, gate):
    outs = []
    for gi, w in enumerate(POOL_WINDOWS):
        sl = slice(gi * POOL_GROUP_WIDTH, (gi + 1) * POOL_GROUP_WIDTH)
        r = u_ext[:, sl]
        s = 1
        while s < w:
            r = r + pltpu.roll(r, s, 0)
            s *= 2
        d = r[N_META:, :] * (1.0 / w) - u[:, sl]
        outs.append(jnp.dot(d.astype(jnp.bfloat16), wpool_ref[gi],
                            preferred_element_type=jnp.float32))
    y = jnp.concatenate(outs, axis=-1) * pscale
    return y * gate


def _meta_kernel(mt_ref, g_ref, b_ref, w_ref, bf_ref,
                 kmeta_ref, vTmeta_ref, umeta_ref):
    hn = _layer_norm(mt_ref[...], g_ref[...], b_ref[...])
    hb = hn.astype(jnp.bfloat16)
    flT = _proj_t(w_ref, hb, O_F, LANES)
    row = lax.broadcasted_iota(jnp.int32, flT.shape, 0)
    col = lax.broadcasted_iota(jnp.int32, flT.shape, 1)
    valid = (row < HEADS) & (col < N_META)
    logf = jnp.where(valid, _log_sigmoid(flT + bf_ref[...]), 0.0)
    cT = _lane_cumsum(logf)
    total = cT[:, META_PAD - 1:META_PAD]
    c_rows = ((cT - total) * LOG2E).T
    kmeta_ref[...] = _key_operand(_proj(w_ref, hb, O_K, ATT_WIDTH), _bias_columns(c_rows))
    vTmeta_ref[...] = _proj_t(w_ref, hb, O_V, ATT_WIDTH).astype(jnp.bfloat16)
    u = _proj(w_ref, hb, O_U, D_MODEL)
    umeta_ref[...] = u[:N_META, :]


def _in_proj_kernel(x_ref, g_ref, b_ref, w_ref, bf_ref, wpool_ref, pscale_ref, umeta_ref,
                    qT_ref, k_ref, vT_ref, sgT_ref, p_ref, carry_c, carry_u):
    t = pl.program_id(1)

    @pl.when(t == 0)
    def _():
        carry_c[...] = jnp.zeros_like(carry_c)
        carry_u[...] = umeta_ref[...]

    c_carry = carry_c[:, 0:1]
    u_carry = carry_u[...]
    sub = ATT_TILE
    per_q = Q_TILE // sub
    for h in range(TOKEN_TILE // sub):
        rows = slice(h * sub, (h + 1) * sub)
        qcols = slice((h % per_q) * sub, (h % per_q + 1) * sub)
        hn = _layer_norm(x_ref[0, rows, :], g_ref[...], b_ref[...])
        hb = hn.astype(jnp.bfloat16)

        qT = _proj_t(w_ref, hb, O_Q, ATT_WIDTH) * Q_SCALE
        qT_ref[0, h // per_q, :, qcols] = qT.astype(jnp.bfloat16)
        vT_ref[0, h] = _proj_t(w_ref, hb, O_V, ATT_WIDTH).astype(jnp.bfloat16)
        gT = _proj_t(w_ref, hb, O_G, ATT_WIDTH)
        sgT_ref[0, h // per_q, :, qcols] = _silu(gT).astype(jnp.bfloat16)

        flT = _proj_t(w_ref, hb, O_F, LANES)
        logf = _log_sigmoid(flT[:HEADS, :] + bf_ref[:HEADS, :])
        cT = _lane_cumsum(logf) + c_carry
        c_carry = cT[:, sub - 1:sub]
        cT_pad = jnp.concatenate(
            [cT * LOG2E, jnp.zeros((LANES - HEADS, sub), jnp.float32)], axis=0)
        k_ref[0, rows, :] = _key_operand(_proj(w_ref, hb, O_K, ATT_WIDTH), _bias_columns(cT_pad.T))

        u = _proj(w_ref, hb, O_U, D_MODEL)
        gp = _proj(w_ref, hb, O_GP, D_MODEL)
        u_ext = jnp.concatenate([u_carry, u], axis=0)
        u_carry = u[sub - N_META:, :]
        p = _pool_mix(u_ext, u, wpool_ref, pscale_ref[...], _silu(gp))
        p_ref[0, rows, :] = p.astype(jnp.bfloat16)
    carry_c[...] = jnp.broadcast_to(c_carry, carry_c.shape)
    carry_u[...] = u_carry


def _attn_kernel(qT_ref, k_ref, vT_ref, sgT_ref, kmeta_ref, vTmeta_ref, o_ref,
                 qa_ref, s_ref, smeta_ref, acc_ref):
    n_qb = qT_ref.shape[1]
    heads = range(HEAD_GROUP)
    sel_row = lax.broadcasted_iota(jnp.int32, (KEY_WIDTH - HEAD_DIM, Q_TILE), 0)
    for g in heads:
        h = pl.program_id(1) * HEAD_GROUP + g
        picks = (sel_row == h) | (sel_row == HEADS + h) | (sel_row == 2 * HEADS + h)
        for par in range(2):
            qa_ref[par, g, HEAD_DIM:, :] = jnp.where(picks, -1.0, 0.0).astype(jnp.bfloat16)
    ones = jnp.ones((ONES_ROWS, ATT_TILE), jnp.bfloat16)
    ones_meta = jnp.ones((ONES_ROWS, META_PAD), jnp.bfloat16)
    krow = lax.broadcasted_iota(jnp.int32, (ATT_TILE, Q_TILE), 0)
    qcol = lax.broadcasted_iota(jnp.int32, (ATT_TILE, Q_TILE), 1)
    causal_a = krow <= qcol
    causal_b = causal_a[:, :ATT_TILE]
    meta_fill = jnp.zeros((META_PAD - N_META, Q_TILE), jnp.bfloat16)
    acc_zero = jnp.zeros(acc_ref.shape[2:], jnp.float32)

    def hrows(g, width):
        return slice(g * width, (g + 1) * width)

    def colmax(s):
        return jnp.max(s, axis=0, keepdims=True)

    def key_tile(g, kj):
        rows = pl.ds(pl.multiple_of(kj * ATT_TILE, ATT_TILE), ATT_TILE)
        return k_ref[0, rows, hrows(g, KEY_WIDTH)]

    def value_tile(g, kj):
        return jnp.concatenate([vT_ref[0, kj, hrows(g, HEAD_DIM), :], ones], axis=0)

    def head_scores(g, kj, qa, dst):
        s = jnp.dot(key_tile(g, kj), qa[g], preferred_element_type=jnp.float32)
        dst[g] = s
        return colmax(s)

    def head_consume(g, kj, src, acc, smax, m):
        m_new = jnp.maximum(m, smax)
        alpha = jnp.exp2(m - m_new)
        p = jnp.exp2(src[g] - m_new).astype(jnp.bfloat16)
        acc[g] = alpha * acc[g] + jnp.dot(value_tile(g, kj), p, preferred_element_type=jnp.float32)
        return m_new

    def start_block(g, qi, qa, smeta, dst):
        qa[g, :HEAD_DIM, :] = qT_ref[0, qi, hrows(g, HEAD_DIM), :]
        smeta[g] = jnp.dot(kmeta_ref[:N_META, hrows(g, KEY_WIDTH)], qa[g],
                           preferred_element_type=jnp.float32)
        return head_scores(g, 0, qa, dst)

    def block(qi, par, smax_first):
        qa, qa_nx = qa_ref.at[par], qa_ref.at[1 - par]
        smeta, smeta_nx = smeta_ref.at[par], smeta_ref.at[1 - par]
        acc, acc_nx = acc_ref.at[par], acc_ref.at[1 - par]
        s_first, s_other, s_nx = s_ref.at[2 * par], s_ref.at[1], s_ref.at[2 - 2 * par]
        m0 = tuple(jnp.full((1, Q_TILE), -jnp.inf, jnp.float32) for _ in heads)

        def step(k_next, dst, k_cur, src, smax_cur, ms):
            smax_next, ms_new = [], []
            for g in heads:
                smax_next.append(head_scores(g, k_next, qa, dst))
                ms_new.append(head_consume(g, k_cur, src, acc, smax_cur[g], ms[g]))
            return tuple(ms_new), tuple(smax_next)

        def pair(t, ms, smax):
            ms, smax_o = step(t + 1, s_other, t, s_first, smax, ms)
            return step(t + 2, s_first, t + 1, s_other, smax_o, ms)

        def quad(k, carry):
            return pair(4 * k + 2, *pair(4 * k, *carry))

        def octo(k, carry):
            return quad(2 * k + 1, quad(2 * k, carry))

        n_quads = qi // 2
        carry = lax.fori_loop(0, n_quads // 2, octo, (m0, smax_first))
        ms, smax = lax.fori_loop(n_quads - n_quads % 2, n_quads, quad, carry)
        if par == 1:
            ms, _ = pair(2 * qi - 2, ms, smax)

        q_next = jnp.minimum(qi + 1, n_qb - 1)
        sbs = [jnp.dot(key_tile(g, 2 * qi + 1), qa[g][:, ATT_TILE:],
                       preferred_element_type=jnp.float32) for g in heads]
        smax_next = [start_block(g, q_next, qa_nx, smeta_nx, s_nx) for g in heads]
        for g in heads:
            sb = jnp.where(causal_b, sbs[g], -jnp.inf)
            s = jnp.where(causal_a, s_first[g], -jnp.inf)
            sm = smeta[g]
            m_new = jnp.maximum(jnp.maximum(ms[g], colmax(s)), colmax(sm))
            m_new = jnp.concatenate(
                [m_new[:, :ATT_TILE], jnp.maximum(m_new[:, ATT_TILE:], colmax(sb))], axis=1)
            alpha = jnp.exp2(ms[g] - m_new)
            p = jnp.exp2(s - m_new).astype(jnp.bfloat16)
            pb = jnp.exp2(sb - m_new[:, ATT_TILE:]).astype(jnp.bfloat16)
            pm = jnp.concatenate([jnp.exp2(sm - m_new).astype(jnp.bfloat16), meta_fill], axis=0)
            acc_nx[g] = acc_zero
            vmeta = jnp.concatenate([vTmeta_ref[hrows(g, HEAD_DIM), :], ones_meta], axis=0)
            a = (alpha * acc[g]
                 + jnp.dot(value_tile(g, 2 * qi), p, preferred_element_type=jnp.float32)
                 + jnp.dot(vmeta, pm, preferred_element_type=jnp.float32))
            a_late = a[:, ATT_TILE:] + jnp.dot(value_tile(g, 2 * qi + 1), pb,
                                               preferred_element_type=jnp.float32)
            a = jnp.concatenate([a[:, :ATT_TILE], a_late], axis=1)
            o = a[:HEAD_DIM, :] / a[HEAD_DIM:HEAD_DIM + 1, :]
            o = o * sgT_ref[0, qi, hrows(g, HEAD_DIM), :].astype(jnp.float32)
            o_ref[0, qi, hrows(g, HEAD_DIM), :] = o.astype(jnp.bfloat16)
        return tuple(smax_next)

    def two_blocks(j, smax_first):
        return block(2 * j + 1, 1, block(2 * j, 0, smax_first))

    smax_first = []
    for g in heads:
        acc_ref[0, g] = acc_zero
        smax_first.append(start_block(g, 0, qa_ref.at[0], smeta_ref.at[0], s_ref.at[0]))
    lax.fori_loop(0, n_qb // 2, two_blocks, tuple(smax_first))


def _out_proj_kernel(x_ref, gin_ref, bin_ref, aT_ref, p_ref, waT_ref, wp_ref, g_ref, b_ref, o_ref):
    for j in range(OUT_TILE // Q_TILE):
        rows = slice(j * Q_TILE, (j + 1) * Q_TILE)
        hn = _layer_norm(x_ref[0, rows, :], gin_ref[...], bin_ref[...])
        yT = jnp.dot(waT_ref[...], aT_ref[0, j], preferred_element_type=jnp.float32)
        y = yT.T + jnp.dot(p_ref[0, rows, :], wp_ref[...], preferred_element_type=jnp.float32)
        o_ref[0, rows, :] = _layer_norm(DEEPNORM_ALPHA * hn + y, g_ref[...], b_ref[...])


def _cast_kernel(w_ref, o_ref):
    o_ref[...] = w_ref[...].astype(jnp.bfloat16)


def _out_weight_kernel(wa_ref, wpin_ref, waT_ref, wp_ref):
    waT_ref[...] = wa_ref[0].T.astype(jnp.bfloat16)
    wp_ref[...] = wpin_ref[0].astype(jnp.bfloat16)


def _const_spec(shape):
    nd = len(shape)
    return pl.BlockSpec(shape, lambda *_: (0,) * nd, pipeline_mode=pl.Buffered(1))


def kernel(x, meta_tokens, ln_in_g, ln_in_b, w_in, b_forget, w_pool, pool_scale, w_out, ln_g, ln_b):
    B, L, D = x.shape
    assert D == D_MODEL and L % (2 * Q_TILE) == 0 and TOKEN_TILE % Q_TILE == 0 and w_in.shape[0] == 1
    bf16, f32 = jnp.bfloat16, jnp.float32
    assert w_in.shape[2] == IN_COLS and w_out.shape[1] == 2 * D_MODEL

    w_t = jnp.swapaxes(w_in, 1, 2)[0]
    w_bf = pl.pallas_call(
        _cast_kernel,
        grid=(IN_COLS // CAST_ROWS,),
        in_specs=[pl.BlockSpec((CAST_ROWS, D), lambda r: (r, 0))],
        out_specs=pl.BlockSpec((CAST_ROWS, D), lambda r: (r, 0)),
        out_shape=jax.ShapeDtypeStruct((IN_COLS, D), bf16),
        compiler_params=pltpu.CompilerParams(dimension_semantics=("arbitrary",)),
        name="in_weight_cast",
    )(w_t)
    square = jax.ShapeDtypeStruct((D, D), bf16)
    waT, wp = pl.pallas_call(
        _out_weight_kernel,
        grid=(D // PREP_ROWS,),
        in_specs=[pl.BlockSpec((1, PREP_ROWS, D), lambda r: (0, r, 0)),
                  pl.BlockSpec((1, PREP_ROWS, D), lambda r: (0, r + D // PREP_ROWS, 0))],
        out_specs=(pl.BlockSpec((D, PREP_ROWS), lambda r: (0, r)),
                   pl.BlockSpec((PREP_ROWS, D), lambda r: (r, 0))),
        out_shape=(square, square),
        compiler_params=pltpu.CompilerParams(dimension_semantics=("arbitrary",)),
        name="out_weight_prep",
    )(w_out, w_out)
    bf_col = jnp.pad(b_forget[0].astype(f32), (0, LANES - HEADS)).reshape(LANES, 1)
    wpool = w_pool[0].astype(bf16)
    pscale = pool_scale[0].reshape(1, D).astype(f32)
    g_in = ln_in_g.reshape(1, D).astype(f32)
    b_in = ln_in_b.reshape(1, D).astype(f32)
    g_out = ln_g[0].reshape(1, D).astype(f32)
    b_out = ln_b[0].reshape(1, D).astype(f32)
    mt_pad = jnp.pad(meta_tokens.astype(f32), ((0, META_PAD - N_META), (0, 0)))

    kw = HEADS * KEY_WIDTH
    kmeta, vTmeta, umeta = pl.pallas_call(
        _meta_kernel,
        out_shape=(jax.ShapeDtypeStruct((META_PAD, kw), bf16),
                   jax.ShapeDtypeStruct((ATT_WIDTH, META_PAD), bf16),
                   jax.ShapeDtypeStruct((N_META, D), f32)),
        compiler_params=pltpu.CompilerParams(vmem_limit_bytes=VMEM_LIMIT),
        name="meta_proj",
    )(mt_pad, g_in, b_in, w_bf, bf_col)

    n_t = L // TOKEN_TILE
    n_sub = TOKEN_TILE // ATT_TILE
    n_ab = L // ATT_TILE
    ktiled = jax.ShapeDtypeStruct((B, n_ab, ATT_WIDTH, ATT_TILE), bf16)
    ktiled_spec = pl.BlockSpec((1, n_sub, ATT_WIDTH, ATT_TILE), lambda b, t: (b, t, 0, 0))
    n_qb = L // Q_TILE
    qtiled = jax.ShapeDtypeStruct((B, n_qb, ATT_WIDTH, Q_TILE), bf16)
    qtiled_spec = pl.BlockSpec((1, TOKEN_TILE // Q_TILE, ATT_WIDTH, Q_TILE), lambda b, t: (b, t, 0, 0))
    qT, kk, vT, sgT, pmix = pl.pallas_call(
        _in_proj_kernel,
        grid=(B, n_t),
        in_specs=[
            pl.BlockSpec((1, TOKEN_TILE, D), lambda b, t: (b, t, 0)),
            _const_spec((1, D)), _const_spec((1, D)),
            _const_spec((IN_COLS, D)), _const_spec((LANES, 1)),
            _const_spec((len(POOL_WINDOWS), POOL_GROUP_WIDTH, POOL_GROUP_WIDTH)),
            _const_spec((1, D)), _const_spec((N_META, D)),
        ],
        out_specs=(qtiled_spec,
                   pl.BlockSpec((1, TOKEN_TILE, kw), lambda b, t: (b, t, 0)),
                   ktiled_spec, qtiled_spec,
                   pl.BlockSpec((1, TOKEN_TILE, D), lambda b, t: (b, t, 0))),
        out_shape=(qtiled, jax.ShapeDtypeStruct((B, L, kw), bf16), ktiled, qtiled,
                   jax.ShapeDtypeStruct((B, L, D), bf16)),
        scratch_shapes=[pltpu.VMEM((HEADS, LANES), f32), pltpu.VMEM((N_META, D), f32)],
        compiler_params=pltpu.CompilerParams(
            dimension_semantics=("arbitrary", "arbitrary"), vmem_limit_bytes=VMEM_LIMIT),
        name="in_proj",
    )(x, g_in, b_in, w_bf, bf_col, wpool, pscale, umeta)

    gd = HEAD_GROUP * HEAD_DIM
    head_ktiled = pl.BlockSpec((1, n_ab, gd, ATT_TILE), lambda b, h: (b, 0, h, 0))
    head_qtiled = pl.BlockSpec((1, n_qb, gd, Q_TILE), lambda b, h: (b, 0, h, 0))
    aT = pl.pallas_call(
        _attn_kernel,
        grid=(B, HEADS // HEAD_GROUP),
        in_specs=[
            head_qtiled,
            pl.BlockSpec((1, L, HEAD_GROUP * KEY_WIDTH), lambda b, h: (b, 0, h)),
            head_ktiled, head_qtiled,
            pl.BlockSpec((META_PAD, HEAD_GROUP * KEY_WIDTH), lambda b, h: (0, h)),
            pl.BlockSpec((gd, META_PAD), lambda b, h: (h, 0)),
        ],
        out_specs=head_qtiled,
        out_shape=qtiled,
        scratch_shapes=[pltpu.VMEM((2, HEAD_GROUP, KEY_WIDTH, Q_TILE), bf16),
                        pltpu.VMEM((3, HEAD_GROUP, ATT_TILE, Q_TILE), f32),
                        pltpu.VMEM((2, HEAD_GROUP, N_META, Q_TILE), f32),
                        pltpu.VMEM((2, HEAD_GROUP, HEAD_DIM + ONES_ROWS, Q_TILE), f32)],
        compiler_params=pltpu.CompilerParams(
            dimension_semantics=("arbitrary", "arbitrary"), vmem_limit_bytes=VMEM_LIMIT),
        name="fox_attn",
    )(qT, kk, vT, sgT, kmeta, vTmeta)

    out = pl.pallas_call(
        _out_proj_kernel,
        grid=(B, L // OUT_TILE),
        in_specs=[
            pl.BlockSpec((1, OUT_TILE, D), lambda b, t: (b, t, 0)),
            _const_spec((1, D)), _const_spec((1, D)),
            pl.BlockSpec((1, OUT_TILE // Q_TILE, ATT_WIDTH, Q_TILE), lambda b, t: (b, t, 0, 0)),
            pl.BlockSpec((1, OUT_TILE, D), lambda b, t: (b, t, 0)),
            _const_spec((D, ATT_WIDTH)), _const_spec((D, D)),
            _const_spec((1, D)), _const_spec((1, D)),
        ],
        out_specs=pl.BlockSpec((1, OUT_TILE, D), lambda b, t: (b, t, 0)),
        out_shape=jax.ShapeDtypeStruct((B, L, D), x.dtype),
        compiler_params=pltpu.CompilerParams(
            dimension_semantics=("arbitrary", "arbitrary"), vmem_limit_bytes=VMEM_LIMIT),
        name="out_proj",
    )(x, g_in, b_in, aT, pmix, waT, wp, g_out, b_out)
    return out
```

```python
import math

import jax
import jax.numpy as jnp
from jax import lax
from jax.experimental import pallas as pl
from jax.experimental.pallas import tpu as pltpu

D_MODEL = 1024
N_META = 16
HEADS = 16
HEAD_DIM = 64
ATT_WIDTH = HEADS * HEAD_DIM
POOL_WINDOWS = (2, 4, 8, 16)
POOL_GROUP_WIDTH = D_MODEL // len(POOL_WINDOWS)
LN_EPS = 1e-5
DEEPNORM_ALPHA = 2.0 ** 0.25
LOG2E = math.log2(math.e)
Q_SCALE = HEAD_DIM ** -0.5 * LOG2E

LANES = 128
PAIR_WIDTH = 2 * HEAD_DIM
KEY_WIDTH = LANES
META_PAD = 128
TOKEN_TILE = 1024
OUT_TILE = 1024
PREP_ROWS = 256
CAST_ROWS = 560
O_Q, O_K, O_V, O_F = 0, ATT_WIDTH, 2 * ATT_WIDTH, 3 * ATT_WIDTH
O_G = O_F + HEADS
O_U = O_G + ATT_WIDTH
O_GP = O_U + D_MODEL
IN_COLS = O_GP + D_MODEL
ATT_TILE = 256
Q_TILE = 2 * ATT_TILE
ONES_ROWS = 16
HEAD_GROUP = 4
VMEM_LIMIT = 56 * 1024 * 1024

_NT = (((1,), (1,)), ((), ()))


def _layer_norm(x, g, b):
    mu = jnp.mean(x, axis=-1, keepdims=True)
    xc = x - mu
    var = jnp.mean(xc * xc, axis=-1, keepdims=True)
    return xc * lax.rsqrt(var + LN_EPS) * g + b


def _log_sigmoid(z):
    return jnp.minimum(z, 0.0) - jnp.log(1.0 + jnp.exp(-jnp.abs(z)))


def _silu(z):
    return z / (1.0 + jnp.exp(-z))


def _proj(w_ref, hb, lo, n):
    return lax.dot_general(hb, w_ref[lo:lo + n, :], _NT, preferred_element_type=jnp.float32)


def _proj_t(w_ref, hb, lo, n):
    return lax.dot_general(w_ref[lo:lo + n, :], hb, _NT, preferred_element_type=jnp.float32)


def _lane_cumsum(x):
    n = x.shape[-1]
    lane = lax.broadcasted_iota(jnp.int32, x.shape, x.ndim - 1)
    d = 1
    while d < n:
        x = x + jnp.where(lane >= d, pltpu.roll(x, d, x.ndim - 1), 0.0)
        d *= 2
    return x


def _bias_columns(c_rows):
    hi = c_rows.astype(jnp.bfloat16).astype(jnp.float32)
    r1 = c_rows - hi
    mid = r1.astype(jnp.bfloat16).astype(jnp.float32)
    lo = (r1 - mid).astype(jnp.bfloat16).astype(jnp.float32)
    return hi + pltpu.roll(mid, HEADS, 1) + pltpu.roll(lo, 2 * HEADS, 1)


def _key_operand(kk, bias_lanes):
    low = lax.broadcasted_iota(jnp.int32, bias_lanes.shape, 1) < HEAD_DIM
    upper = pltpu.roll(bias_lanes, HEAD_DIM, 1)
    parts = []
    for j in range(HEADS // 2):
        pair = kk[:, j * PAIR_WIDTH:(j + 1) * PAIR_WIDTH]
        parts.append(jnp.where(low, pair, upper))
        parts.append(jnp.where(low, pltpu.roll(pair, HEAD_DIM, 1), upper))
    return jnp.concatenate(parts, axis=1).astype(jnp.bfloat16)


def _pool_mix(u_ext, u, wpool_ref, pscale, gate):
    outs = []
    for gi, w in enumerate(POOL_WINDOWS):
        sl = slice(gi * POOL_GROUP_WIDTH, (gi + 1) * POOL_GROUP_WIDTH)
        r = u_ext[:, sl]
        s = 1
        while s < w:
            r = r + pltpu.roll(r, s, 0)
            s *= 2
        d = r[N_META:, :] * (1.0 / w) - u[:, sl]
        outs.append(jnp.dot(d.astype(jnp.bfloat16), wpool_ref[gi],
                            preferred_element_type=jnp.float32))
    y = jnp.concatenate(outs, axis=-1) * pscale
    return y * gate


def _meta_kernel(mt_ref, g_ref, b_ref, w_ref, bf_ref,
                 kmeta_ref, vTmeta_ref, umeta_ref):
    hn = _layer_norm(mt_ref[...], g_ref[...], b_ref[...])
    hb = hn.astype(jnp.bfloat16)
    flT = _proj_t(w_ref, hb, O_F, LANES)
    row = lax.broadcasted_iota(jnp.int32, flT.shape, 0)
    col = lax.broadcasted_iota(jnp.int32, flT.shape, 1)
    valid = (row < HEADS) & (col < N_META)
    logf = jnp.where(valid, _log_sigmoid(flT + bf_ref[...]), 0.0)
    cT = _lane_cumsum(logf)
    total = cT[:, META_PAD - 1:META_PAD]
    c_rows = ((cT - total) * LOG2E).T
    kmeta_ref[...] = _key_operand(_proj(w_ref, hb, O_K, ATT_WIDTH), _bias_columns(c_rows))
    vTmeta_ref[...] = _proj_t(w_ref, hb, O_V, ATT_WIDTH).astype(jnp.bfloat16)
    u = _proj(w_ref, hb, O_U, D_MODEL)
    umeta_ref[...] = u[:N_META, :]


def _in_proj_kernel(x_ref, g_ref, b_ref, w_ref, bf_ref, wpool_ref, pscale_ref, umeta_ref,
                    qT_ref, k_ref, vT_ref, sgT_ref, p_ref, carry_c, carry_u):
    t = pl.program_id(1)

    @pl.when(t == 0)
    def _():
        carry_c[...] = jnp.zeros_like(carry_c)
        carry_u[...] = umeta_ref[...]

    c_carry = carry_c[:, 0:1]
    u_carry = carry_u[...]
    sub = ATT_TILE
    per_q = Q_TILE // sub
    for h in range(TOKEN_TILE // sub):
        rows = slice(h * sub, (h + 1) * sub)
        qcols = slice((h % per_q) * sub, (h % per_q + 1) * sub)
        hn = _layer_norm(x_ref[0, rows, :], g_ref[...], b_ref[...])
        hb = hn.astype(jnp.bfloat16)

        qT = _proj_t(w_ref, hb, O_Q, ATT_WIDTH) * Q_SCALE
        qT_ref[0, h // per_q, :, qcols] = qT.astype(jnp.bfloat16)
        vT_ref[0, h] = _proj_t(w_ref, hb, O_V, ATT_WIDTH).astype(jnp.bfloat16)
        gT = _proj_t(w_ref, hb, O_G, ATT_WIDTH)
        sgT_ref[0, h // per_q, :, qcols] = _silu(gT).astype(jnp.bfloat16)

        flT = _proj_t(w_ref, hb, O_F, LANES)
        logf = _log_sigmoid(flT[:HEADS, :] + bf_ref[:HEADS, :])
        cT = _lane_cumsum(logf) + c_carry
        c_carry = cT[:, sub - 1:sub]
        cT_pad = jnp.concatenate(
            [cT * LOG2E, jnp.zeros((LANES - HEADS, sub), jnp.float32)], axis=0)
        k_ref[0, rows, :] = _key_operand(_proj(w_ref, hb, O_K, ATT_WIDTH), _bias_columns(cT_pad.T))

        u = _proj(w_ref, hb, O_U, D_MODEL)
        gp = _proj(w_ref, hb, O_GP, D_MODEL)
        u_ext = jnp.concatenate([u_carry, u], axis=0)
        u_carry = u[sub - N_META:, :]
        p = _pool_mix(u_ext, u, wpool_ref, pscale_ref[...], _silu(gp))
        p_ref[0, rows, :] = p.astype(jnp.bfloat16)
    carry_c[...] = jnp.broadcast_to(c_carry, carry_c.shape)
    carry_u[...] = u_carry


def _attn_kernel(qT_ref, k_ref, vT_ref, sgT_ref, kmeta_ref, vTmeta_ref, o_ref,
                 qa_ref, s_ref, smeta_ref, acc_ref):
    n_qb = qT_ref.shape[1]
    heads = range(HEAD_GROUP)
    sel_row = lax.broadcasted_iota(jnp.int32, (KEY_WIDTH - HEAD_DIM, Q_TILE), 0)
    for g in heads:
        h = pl.program_id(1) * HEAD_GROUP + g
        picks = (sel_row == h) | (sel_row == HEADS + h) | (sel_row == 2 * HEADS + h)
        for par in range(2):
            qa_ref[par, g, HEAD_DIM:, :] = jnp.where(picks, -1.0, 0.0).astype(jnp.bfloat16)
    ones = jnp.ones((ONES_ROWS, ATT_TILE), jnp.bfloat16)
    ones_meta = jnp.ones((ONES_ROWS, META_PAD), jnp.bfloat16)
    krow = lax.broadcasted_iota(jnp.int32, (ATT_TILE, Q_TILE), 0)
    qcol = lax.broadcasted_iota(jnp.int32, (ATT_TILE, Q_TILE), 1)
    causal_a = krow <= qcol
    causal_b = causal_a[:, :ATT_TILE]
    meta_fill = jnp.zeros((META_PAD - N_META, Q_TILE), jnp.bfloat16)
    acc_zero = jnp.zeros(acc_ref.shape[2:], jnp.float32)

    def hrows(g, width):
        return slice(g * width, (g + 1) * width)

    def colmax(s):
        return jnp.max(s, axis=0, keepdims=True)

    def key_tile(g, kj):
        rows = pl.ds(pl.multiple_of(kj * ATT_TILE, ATT_TILE), ATT_TILE)
        return k_ref[0, rows, hrows(g, KEY_WIDTH)]

    def value_tile(g, kj):
        return jnp.concatenate([vT_ref[0, kj, hrows(g, HEAD_DIM), :], ones], axis=0)

    def head_scores(g, kj, qa, dst):
        s = jnp.dot(key_tile(g, kj), qa[g], preferred_element_type=jnp.float32)
        dst[g] = s
        return colmax(s)

    def head_consume(g, kj, src, acc, smax, m):
        m_new = jnp.maximum(m, smax)
        alpha = jnp.exp2(m - m_new)
        p = jnp.exp2(src[g] - m_new).astype(jnp.bfloat16)
        acc[g] = alpha * acc[g] + jnp.dot(value_tile(g, kj), p, preferred_element_type=jnp.float32)
        return m_new

    def start_block(g, qi, qa, smeta, dst):
        qa[g, :HEAD_DIM, :] = qT_ref[0, qi, hrows(g, HEAD_DIM), :]
        smeta[g] = jnp.dot(kmeta_ref[:N_META, hrows(g, KEY_WIDTH)], qa[g],
                           preferred_element_type=jnp.float32)
        return head_scores(g, 0, qa, dst)

    def block(qi, pos, smax_first):
        par = pos % 2
        qa, qa_nx = qa_ref.at[par], qa_ref.at[1 - par]
        smeta, smeta_nx = smeta_ref.at[par], smeta_ref.at[1 - par]
        acc, acc_nx = acc_ref.at[par], acc_ref.at[1 - par]
        s_first, s_other, s_nx = s_ref.at[2 * par], s_ref.at[1], s_ref.at[2 - 2 * par]
        m0 = tuple(jnp.full((1, Q_TILE), -jnp.inf, jnp.float32) for _ in heads)

        def step(k_next, dst, k_cur, src, smax_cur, ms):
            smax_next, ms_new = [], []
            for g in heads:
                smax_next.append(head_scores(g, k_next, qa, dst))
                ms_new.append(head_consume(g, k_cur, src, acc, smax_cur[g], ms[g]))
            return tuple(ms_new), tuple(smax_next)

        def pair(t, ms, smax):
            ms, smax_o = step(t + 1, s_other, t, s_first, smax, ms)
            return step(t + 2, s_first, t + 1, s_other, smax_o, ms)

        def quad(k, carry):
            return pair(4 * k + 2, *pair(4 * k, *carry))

        def octo(k, carry):
            return quad(2 * k + 1, quad(2 * k, carry))

        ms, smax = lax.fori_loop(0, qi // 4, octo, (m0, smax_first))
        for t in range(pos):
            ms, smax = pair(2 * qi - 2 * pos + 2 * t, ms, smax)

        q_next = jnp.minimum(qi + 1, n_qb - 1)
        sbs = [jnp.dot(key_tile(g, 2 * qi + 1), qa[g][:, ATT_TILE:],
                       preferred_element_type=jnp.float32) for g in heads]
        smax_next = [start_block(g, q_next, qa_nx, smeta_nx, s_nx) for g in heads]
        for g in heads:
            sb = jnp.where(causal_b, sbs[g], -jnp.inf)
            s = jnp.where(causal_a, s_first[g], -jnp.inf)
            sm = smeta[g]
            m_new = jnp.maximum(jnp.maximum(ms[g], colmax(s)), colmax(sm))
            m_new = jnp.concatenate(
                [m_new[:, :ATT_TILE], jnp.maximum(m_new[:, ATT_TILE:], colmax(sb))], axis=1)
            alpha = jnp.exp2(ms[g] - m_new)
            p = jnp.exp2(s - m_new).astype(jnp.bfloat16)
            pb = jnp.exp2(sb - m_new[:, ATT_TILE:]).astype(jnp.bfloat16)
            pm = jnp.concatenate([jnp.exp2(sm - m_new).astype(jnp.bfloat16), meta_fill], axis=0)
            acc_nx[g] = acc_zero
            vmeta = jnp.concatenate([vTmeta_ref[hrows(g, HEAD_DIM), :], ones_meta], axis=0)
            a = (alpha * acc[g]
                 + jnp.dot(value_tile(g, 2 * qi), p, preferred_element_type=jnp.float32)
                 + jnp.dot(vmeta, pm, preferred_element_type=jnp.float32))
            a_late = a[:, ATT_TILE:] + jnp.dot(value_tile(g, 2 * qi + 1), pb,
                                               preferred_element_type=jnp.float32)
            a = jnp.concatenate([a[:, :ATT_TILE], a_late], axis=1)
            o = a[:HEAD_DIM, :] / a[HEAD_DIM:HEAD_DIM + 1, :]
            o = o * sgT_ref[0, qi, hrows(g, HEAD_DIM), :].astype(jnp.float32)
            o_ref[0, qi, hrows(g, HEAD_DIM), :] = o.astype(jnp.bfloat16)
        return tuple(smax_next)

    def four_blocks(j, smax_first):
        for pos in range(4):
            smax_first = block(4 * j + pos, pos, smax_first)
        return smax_first

    smax_first = []
    for g in heads:
        acc_ref[0, g] = acc_zero
        smax_first.append(start_block(g, 0, qa_ref.at[0], smeta_ref.at[0], s_ref.at[0]))
    lax.fori_loop(0, n_qb // 4, four_blocks, tuple(smax_first))


def _out_proj_kernel(x_ref, gin_ref, bin_ref, aT_ref, p_ref, waT_ref, wp_ref, g_ref, b_ref, o_ref):
    for j in range(OUT_TILE // Q_TILE):
        rows = slice(j * Q_TILE, (j + 1) * Q_TILE)
        hn = _layer_norm(x_ref[0, rows, :], gin_ref[...], bin_ref[...])
        yT = jnp.dot(waT_ref[...], aT_ref[0, j], preferred_element_type=jnp.float32)
        y = yT.T + jnp.dot(p_ref[0, rows, :], wp_ref[...], preferred_element_type=jnp.float32)
        o_ref[0, rows, :] = _layer_norm(DEEPNORM_ALPHA * hn + y, g_ref[...], b_ref[...])


def _cast_kernel(w_ref, o_ref):
    o_ref[...] = w_ref[...].astype(jnp.bfloat16)


def _out_weight_kernel(wa_ref, wpin_ref, waT_ref, wp_ref):
    waT_ref[...] = wa_ref[0].T.astype(jnp.bfloat16)
    wp_ref[...] = wpin_ref[0].astype(jnp.bfloat16)


def _const_spec(shape):
    nd = len(shape)
    return pl.BlockSpec(shape, lambda *_: (0,) * nd, pipeline_mode=pl.Buffered(1))


def kernel(x, meta_tokens, ln_in_g, ln_in_b, w_in, b_forget, w_pool, pool_scale, w_out, ln_g, ln_b):
    B, L, D = x.shape
    assert D == D_MODEL and L % (4 * Q_TILE) == 0 and TOKEN_TILE % Q_TILE == 0 and w_in.shape[0] == 1
    bf16, f32 = jnp.bfloat16, jnp.float32
    assert w_in.shape[2] == IN_COLS and w_out.shape[1] == 2 * D_MODEL

    w_t = jnp.swapaxes(w_in, 1, 2)[0]
    w_bf = pl.pallas_call(
        _cast_kernel,
        grid=(IN_COLS // CAST_ROWS,),
        in_specs=[pl.BlockSpec((CAST_ROWS, D), lambda r: (r, 0))],
        out_specs=pl.BlockSpec((CAST_ROWS, D), lambda r: (r, 0)),
        out_shape=jax.ShapeDtypeStruct((IN_COLS, D), bf16),
        compiler_params=pltpu.CompilerParams(dimension_semantics=("arbitrary",)),
        name="in_weight_cast",
    )(w_t)
    square = jax.ShapeDtypeStruct((D, D), bf16)
    waT, wp = pl.pallas_call(
        _out_weight_kernel,
        grid=(D // PREP_ROWS,),
        in_specs=[pl.BlockSpec((1, PREP_ROWS, D), lambda r: (0, r, 0)),
                  pl.BlockSpec((1, PREP_ROWS, D), lambda r: (0, r + D // PREP_ROWS, 0))],
        out_specs=(pl.BlockSpec((D, PREP_ROWS), lambda r: (0, r)),
                   pl.BlockSpec((PREP_ROWS, D), lambda r: (r, 0))),
        out_shape=(square, square),
        compiler_params=pltpu.CompilerParams(dimension_semantics=("arbitrary",)),
        name="out_weight_prep",
    )(w_out, w_out)
    bf_col = jnp.pad(b_forget[0].astype(f32), (0, LANES - HEADS)).reshape(LANES, 1)
    wpool = w_pool[0].astype(bf16)
    pscale = pool_scale[0].reshape(1, D).astype(f32)
    g_in = ln_in_g.reshape(1, D).astype(f32)
    b_in = ln_in_b.reshape(1, D).astype(f32)
    g_out = ln_g[0].reshape(1, D).astype(f32)
    b_out = ln_b[0].reshape(1, D).astype(f32)
    mt_pad = jnp.pad(meta_tokens.astype(f32), ((0, META_PAD - N_META), (0, 0)))

    kw = HEADS * KEY_WIDTH
    kmeta, vTmeta, umeta = pl.pallas_call(
        _meta_kernel,
        out_shape=(jax.ShapeDtypeStruct((META_PAD, kw), bf16),
                   jax.ShapeDtypeStruct((ATT_WIDTH, META_PAD), bf16),
                   jax.ShapeDtypeStruct((N_META, D), f32)),
        compiler_params=pltpu.CompilerParams(vmem_limit_bytes=VMEM_LIMIT),
        name="meta_proj",
    )(mt_pad, g_in, b_in, w_bf, bf_col)

    n_t = L // TOKEN_TILE
    n_sub = TOKEN_TILE // ATT_TILE
    n_ab = L // ATT_TILE
    ktiled = jax.ShapeDtypeStruct((B, n_ab, ATT_WIDTH, ATT_TILE), bf16)
    ktiled_spec = pl.BlockSpec((1, n_sub, ATT_WIDTH, ATT_TILE), lambda b, t: (b, t, 0, 0))
    n_qb = L // Q_TILE
    qtiled = jax.ShapeDtypeStruct((B, n_qb, ATT_WIDTH, Q_TILE), bf16)
    qtiled_spec = pl.BlockSpec((1, TOKEN_TILE // Q_TILE, ATT_WIDTH, Q_TILE), lambda b, t: (b, t, 0, 0))
    qT, kk, vT, sgT, pmix = pl.pallas_call(
        _in_proj_kernel,
        grid=(B, n_t),
        in_specs=[
            pl.BlockSpec((1, TOKEN_TILE, D), lambda b, t: (b, t, 0)),
            _const_spec((1, D)), _const_spec((1, D)),
            _const_spec((IN_COLS, D)), _const_spec((LANES, 1)),
            _const_spec((len(POOL_WINDOWS), POOL_GROUP_WIDTH, POOL_GROUP_WIDTH)),
            _const_spec((1, D)), _const_spec((N_META, D)),
        ],
        out_specs=(qtiled_spec,
                   pl.BlockSpec((1, TOKEN_TILE, kw), lambda b, t: (b, t, 0)),
                   ktiled_spec, qtiled_spec,
                   pl.BlockSpec((1, TOKEN_TILE, D), lambda b, t: (b, t, 0))),
        out_shape=(qtiled, jax.ShapeDtypeStruct((B, L, kw), bf16), ktiled, qtiled,
                   jax.ShapeDtypeStruct((B, L, D), bf16)),
        scratch_shapes=[pltpu.VMEM((HEADS, LANES), f32), pltpu.VMEM((N_META, D), f32)],
        compiler_params=pltpu.CompilerParams(
            dimension_semantics=("arbitrary", "arbitrary"), vmem_limit_bytes=VMEM_LIMIT),
        name="in_proj",
    )(x, g_in, b_in, w_bf, bf_col, wpool, pscale, umeta)

    gd = HEAD_GROUP * HEAD_DIM
    head_ktiled = pl.BlockSpec((1, n_ab, gd, ATT_TILE), lambda b, h: (b, 0, h, 0))
    head_qtiled = pl.BlockSpec((1, n_qb, gd, Q_TILE), lambda b, h: (b, 0, h, 0))
    aT = pl.pallas_call(
        _attn_kernel,
        grid=(B, HEADS // HEAD_GROUP),
        in_specs=[
            head_qtiled,
            pl.BlockSpec((1, L, HEAD_GROUP * KEY_WIDTH), lambda b, h: (b, 0, h)),
            head_ktiled, head_qtiled,
            pl.BlockSpec((META_PAD, HEAD_GROUP * KEY_WIDTH), lambda b, h: (0, h)),
            pl.BlockSpec((gd, META_PAD), lambda b, h: (h, 0)),
        ],
        out_specs=head_qtiled,
        out_shape=qtiled,
        scratch_shapes=[pltpu.VMEM((2, HEAD_GROUP, KEY_WIDTH, Q_TILE), bf16),
                        pltpu.VMEM((3, HEAD_GROUP, ATT_TILE, Q_TILE), f32),
                        pltpu.VMEM((2, HEAD_GROUP, N_META, Q_TILE), f32),
                        pltpu.VMEM((2, HEAD_GROUP, HEAD_DIM + ONES_ROWS, Q_TILE), f32)],
        compiler_params=pltpu.CompilerParams(
            dimension_semantics=("arbitrary", "arbitrary"), vmem_limit_bytes=VMEM_LIMIT),
        name="fox_attn",
    )(qT, kk, vT, sgT, kmeta, vTmeta)

    out = pl.pallas_call(
        _out_proj_kernel,
        grid=(B, L // OUT_TILE),
        in_specs=[
            pl.BlockSpec((1, OUT_TILE, D), lambda b, t: (b, t, 0)),
            _const_spec((1, D)), _const_spec((1, D)),
            pl.BlockSpec((1, OUT_TILE // Q_TILE, ATT_WIDTH, Q_TILE), lambda b, t: (b, t, 0, 0)),
            pl.BlockSpec((1, OUT_TILE, D), lambda b, t: (b, t, 0)),
            _const_spec((D, ATT_WIDTH)), _const_spec((D, D)),
            _const_spec((1, D)), _const_spec((1, D)),
        ],
        out_specs=pl.BlockSpec((1, OUT_TILE, D), lambda b, t: (b, t, 0)),
        out_shape=jax.ShapeDtypeStruct((B, L, D), x.dtype),
        compiler_params=pltpu.CompilerParams(
            dimension_semantics=("arbitrary", "arbitrary"), vmem_limit_bytes=VMEM_LIMIT),
        name="out_proj",
    )(x, g_in, b_in, aT, pmix, waT, wp, g_out, b_out)
    return out
```

```python
import math

import jax
import jax.numpy as jnp
from jax import lax
from jax.experimental import pallas as pl
from jax.experimental.pallas import tpu as pltpu

D_MODEL = 1024
N_META = 16
HEADS = 16
HEAD_DIM = 64
ATT_WIDTH = HEADS * HEAD_DIM
POOL_WINDOWS = (2, 4, 8, 16)
POOL_GROUP_WIDTH = D_MODEL // len(POOL_WINDOWS)
LN_EPS = 1e-5
DEEPNORM_ALPHA = 2.0 ** 0.25
LOG2E = math.log2(math.e)
Q_SCALE = HEAD_DIM ** -0.5 * LOG2E

LANES = 128
PAIR_WIDTH = 2 * HEAD_DIM
KEY_WIDTH = LANES
META_PAD = 128
TOKEN_TILE = 1024
OUT_TILE = 1024
PREP_ROWS = 256
CAST_ROWS = 560
O_Q, O_K, O_V, O_F = 0, ATT_WIDTH, 2 * ATT_WIDTH, 3 * ATT_WIDTH
O_G = O_F + HEADS
O_U = O_G + ATT_WIDTH
O_GP = O_U + D_MODEL
IN_COLS = O_GP + D_MODEL
ATT_TILE = 256
Q_TILE = 2 * ATT_TILE
ONES_ROWS = 16
HEAD_GROUP = 4
VMEM_LIMIT = 56 * 1024 * 1024

_NT = (((1,), (1,)), ((), ()))


def _layer_norm(x, g, b):
    mu = jnp.mean(x, axis=-1, keepdims=True)
    xc = x - mu
    var = jnp.mean(xc * xc, axis=-1, keepdims=True)
    return xc * lax.rsqrt(var + LN_EPS) * g + b


def _log_sigmoid(z):
    return jnp.minimum(z, 0.0) - jnp.log(1.0 + jnp.exp(-jnp.abs(z)))


def _silu(z):
    return z / (1.0 + jnp.exp(-z))


def _proj(w_ref, hb, lo, n):
    return lax.dot_general(hb, w_ref[lo:lo + n, :], _NT, preferred_element_type=jnp.float32)


def _proj_t(w_ref, hb, lo, n):
    return lax.dot_general(w_ref[lo:lo + n, :], hb, _NT, preferred_element_type=jnp.float32)


def _lane_cumsum(x):
    n = x.shape[-1]
    lane = lax.broadcasted_iota(jnp.int32, x.shape, x.ndim - 1)
    d = 1
    while d < n:
        x = x + jnp.where(lane >= d, pltpu.roll(x, d, x.ndim - 1), 0.0)
        d *= 2
    return x


def _bias_columns(c_rows):
    hi = c_rows.astype(jnp.bfloat16).astype(jnp.float32)
    r1 = c_rows - hi
    mid = r1.astype(jnp.bfloat16).astype(jnp.float32)
    lo = (r1 - mid).astype(jnp.bfloat16).astype(jnp.float32)
    return hi + pltpu.roll(mid, HEADS, 1) + pltpu.roll(lo, 2 * HEADS, 1)


def _key_operand(kk, bias_lanes):
    low = lax.broadcasted_iota(jnp.int32, bias_lanes.shape, 1) < HEAD_DIM
    upper = pltpu.roll(bias_lanes, HEAD_DIM, 1)
    parts = []
    for j in range(HEADS // 2):
        pair = kk[:, j * PAIR_WIDTH:(j + 1) * PAIR_WIDTH]
        parts.append(jnp.where(low, pair, upper))
        parts.append(jnp.where(low, pltpu.roll(pair, HEAD_DIM, 1), upper))
    return jnp.concatenate(parts, axis=1).astype(jnp.bfloat16)


def _pool_mix(u_ext, u, wpool_ref, pscale, gate):
    outs = []
    for gi, w in enumerate(POOL_WINDOWS):
        sl = slice(gi * POOL_GROUP_WIDTH, (gi + 1) * POOL_GROUP_WIDTH)
        r = u_ext[:, sl]
        s = 1
        while s < w:
            r = r + pltpu.roll(r, s, 0)
            s *= 2
        d = r[N_META:, :] * (1.0 / w) - u[:, sl]
        outs.append(jnp.dot(d.astype(jnp.bfloat16), wpool_ref[gi],
                            preferred_element_type=jnp.float32))
    y = jnp.concatenate(outs, axis=-1) * pscale
    return y * gate


def _meta_kernel(mt_ref, g_ref, b_ref, w_ref, bf_ref,
                 kmeta_ref, vTmeta_ref, umeta_ref):
    hn = _layer_norm(mt_ref[...], g_ref[...], b_ref[...])
    hb = hn.astype(jnp.bfloat16)
    flT = _proj_t(w_ref, hb, O_F, LANES)
    row = lax.broadcasted_iota(jnp.int32, flT.shape, 0)
    col = lax.broadcasted_iota(jnp.int32, flT.shape, 1)
    valid = (row < HEADS) & (col < N_META)
    logf = jnp.where(valid, _log_sigmoid(flT + bf_ref[...]), 0.0)
    cT = _lane_cumsum(logf)
    total = cT[:, META_PAD - 1:META_PAD]
    c_rows = ((cT - total) * LOG2E).T
    kmeta_ref[...] = _key_operand(_proj(w_ref, hb, O_K, ATT_WIDTH), _bias_columns(c_rows))
    vTmeta_ref[...] = _proj_t(w_ref, hb, O_V, ATT_WIDTH).astype(jnp.bfloat16)
    u = _proj(w_ref, hb, O_U, D_MODEL)
    umeta_ref[...] = u[:N_META, :]


def _cast_meta_kernel(wchunk_ref, mt_ref, g_ref, b_ref, bf_ref,
                      wbf_ref, kmeta_ref, vTmeta_ref, umeta_ref, wfull_ref):
    r = pl.program_id(0)
    chunk = wchunk_ref[...].astype(jnp.bfloat16)
    wbf_ref[...] = chunk
    wfull_ref[pl.ds(pl.multiple_of(r * CAST_ROWS, CAST_ROWS), CAST_ROWS), :] = chunk

    @pl.when(r == pl.num_programs(0) - 1)
    def _():
        _meta_kernel(mt_ref, g_ref, b_ref, wfull_ref, bf_ref, kmeta_ref, vTmeta_ref, umeta_ref)


def _in_proj_kernel(x_ref, g_ref, b_ref, w_ref, bf_ref, wpool_ref, pscale_ref, umeta_ref,
                    qT_ref, k_ref, vT_ref, sgT_ref, p_ref, carry_c, carry_u):
    t = pl.program_id(1)

    @pl.when(t == 0)
    def _():
        carry_c[...] = jnp.zeros_like(carry_c)
        carry_u[...] = umeta_ref[...]

    c_carry = carry_c[:, 0:1]
    u_carry = carry_u[...]
    sub = ATT_TILE
    per_q = Q_TILE // sub
    for h in range(TOKEN_TILE // sub):
        rows = slice(h * sub, (h + 1) * sub)
        qcols = slice((h % per_q) * sub, (h % per_q + 1) * sub)
        hn = _layer_norm(x_ref[0, rows, :], g_ref[...], b_ref[...])
        hb = hn.astype(jnp.bfloat16)

        qT = _proj_t(w_ref, hb, O_Q, ATT_WIDTH) * Q_SCALE
        qT_ref[0, h // per_q, :, qcols] = qT.astype(jnp.bfloat16)
        vT_ref[0, h] = _proj_t(w_ref, hb, O_V, ATT_WIDTH).astype(jnp.bfloat16)
        gT = _proj_t(w_ref, hb, O_G, ATT_WIDTH)
        sgT_ref[0, h // per_q, :, qcols] = _silu(gT).astype(jnp.bfloat16)

        flT = _proj_t(w_ref, hb, O_F, LANES)
        logf = _log_sigmoid(flT[:HEADS, :] + bf_ref[:HEADS, :])
        cT = _lane_cumsum(logf) + c_carry
        c_carry = cT[:, sub - 1:sub]
        cT_pad = jnp.concatenate(
            [cT * LOG2E, jnp.zeros((LANES - HEADS, sub), jnp.float32)], axis=0)
        k_ref[0, rows, :] = _key_operand(_proj(w_ref, hb, O_K, ATT_WIDTH), _bias_columns(cT_pad.T))

        u = _proj(w_ref, hb, O_U, D_MODEL)
        gp = _proj(w_ref, hb, O_GP, D_MODEL)
        u_ext = jnp.concatenate([u_carry, u], axis=0)
        u_carry = u[sub - N_META:, :]
        p = _pool_mix(u_ext, u, wpool_ref, pscale_ref[...], _silu(gp))
        p_ref[0, rows, :] = p.astype(jnp.bfloat16)
    carry_c[...] = jnp.broadcast_to(c_carry, carry_c.shape)
    carry_u[...] = u_carry


def _attn_kernel(qT_ref, k_ref, vT_ref, sgT_ref, kmeta_ref, vTmeta_ref, o_ref,
                 qa_ref, s_ref, smeta_ref, acc_ref):
    n_qb = qT_ref.shape[1]
    heads = range(HEAD_GROUP)
    sel_row = lax.broadcasted_iota(jnp.int32, (KEY_WIDTH - HEAD_DIM, Q_TILE), 0)
    for g in heads:
        h = pl.program_id(1) * HEAD_GROUP + g
        picks = (sel_row == h) | (sel_row == HEADS + h) | (sel_row == 2 * HEADS + h)
        for par in range(2):
            qa_ref[par, g, HEAD_DIM:, :] = jnp.where(picks, -1.0, 0.0).astype(jnp.bfloat16)
    ones = jnp.ones((ONES_ROWS, ATT_TILE), jnp.bfloat16)
    ones_meta = jnp.ones((ONES_ROWS, META_PAD), jnp.bfloat16)
    krow = lax.broadcasted_iota(jnp.int32, (ATT_TILE, Q_TILE), 0)
    qcol = lax.broadcasted_iota(jnp.int32, (ATT_TILE, Q_TILE), 1)
    causal_a = krow <= qcol
    causal_b = causal_a[:, :ATT_TILE]
    meta_fill = jnp.zeros((META_PAD - N_META, Q_TILE), jnp.bfloat16)
    acc_zero = jnp.zeros(acc_ref.shape[2:], jnp.float32)

    def hrows(g, width):
        return slice(g * width, (g + 1) * width)

    def colmax(s):
        return jnp.max(s, axis=0, keepdims=True)

    def key_tile(g, kj):
        rows = pl.ds(pl.multiple_of(kj * ATT_TILE, ATT_TILE), ATT_TILE)
        return k_ref[0, rows, hrows(g, KEY_WIDTH)]

    def value_tile(g, kj):
        return jnp.concatenate([vT_ref[0, kj, hrows(g, HEAD_DIM), :], ones], axis=0)

    def head_scores(g, kj, qa, dst):
        s = jnp.dot(key_tile(g, kj), qa[g], preferred_element_type=jnp.float32)
        dst[g] = s
        return colmax(s)

    def head_consume(g, kj, src, acc, smax, m):
        m_new = jnp.maximum(m, smax)
        alpha = jnp.exp2(m - m_new)
        p = jnp.exp2(src[g] - m_new).astype(jnp.bfloat16)
        acc[g] = alpha * acc[g] + jnp.dot(value_tile(g, kj), p, preferred_element_type=jnp.float32)
        return m_new

    def start_block(g, qi, qa, smeta, dst):
        qa[g, :HEAD_DIM, :] = qT_ref[0, qi, hrows(g, HEAD_DIM), :]
        smeta[g] = jnp.dot(kmeta_ref[:N_META, hrows(g, KEY_WIDTH)], qa[g],
                           preferred_element_type=jnp.float32)
        return head_scores(g, 0, qa, dst)

    def block(qi, pos, smax_first):
        par = pos % 2
        qa, qa_nx = qa_ref.at[par], qa_ref.at[1 - par]
        smeta, smeta_nx = smeta_ref.at[par], smeta_ref.at[1 - par]
        acc, acc_nx = acc_ref.at[par], acc_ref.at[1 - par]
        s_first, s_other, s_nx = s_ref.at[2 * par], s_ref.at[1], s_ref.at[2 - 2 * par]
        m0 = tuple(jnp.full((1, Q_TILE), -jnp.inf, jnp.float32) for _ in heads)

        def step(k_next, dst, k_cur, src, smax_cur, ms):
            smax_next, ms_new = [], []
            for g in heads:
                smax_next.append(head_scores(g, k_next, qa, dst))
                ms_new.append(head_consume(g, k_cur, src, acc, smax_cur[g], ms[g]))
            return tuple(ms_new), tuple(smax_next)

        def pair(t, ms, smax):
            ms, smax_o = step(t + 1, s_other, t, s_first, smax, ms)
            return step(t + 2, s_first, t + 1, s_other, smax_o, ms)

        def quad(k, carry):
            return pair(4 * k + 2, *pair(4 * k, *carry))

        def octo(k, carry):
            return quad(2 * k + 1, quad(2 * k, carry))

        ms, smax = lax.fori_loop(0, qi // 4, octo, (m0, smax_first))
        for t in range(pos):
            ms, smax = pair(2 * qi - 2 * pos + 2 * t, ms, smax)

        q_next = jnp.minimum(qi + 1, n_qb - 1)
        sbs = [jnp.dot(key_tile(g, 2 * qi + 1), qa[g][:, ATT_TILE:],
                       preferred_element_type=jnp.float32) for g in heads]
        smax_next = [start_block(g, q_next, qa_nx, smeta_nx, s_nx) for g in heads]
        for g in heads:
            sb = jnp.where(causal_b, sbs[g], -jnp.inf)
            s = jnp.where(causal_a, s_first[g], -jnp.inf)
            sm = smeta[g]
            m_new = jnp.maximum(jnp.maximum(ms[g], colmax(s)), colmax(sm))
            m_new = jnp.concatenate(
                [m_new[:, :ATT_TILE], jnp.maximum(m_new[:, ATT_TILE:], colmax(sb))], axis=1)
            alpha = jnp.exp2(ms[g] - m_new)
            p = jnp.exp2(s - m_new).astype(jnp.bfloat16)
            pb = jnp.exp2(sb - m_new[:, ATT_TILE:]).astype(jnp.bfloat16)
            pm = jnp.concatenate([jnp.exp2(sm - m_new).astype(jnp.bfloat16), meta_fill], axis=0)
            acc_nx[g] = acc_zero
            vmeta = jnp.concatenate([vTmeta_ref[hrows(g, HEAD_DIM), :], ones_meta], axis=0)
            a = (alpha * acc[g]
                 + jnp.dot(value_tile(g, 2 * qi), p, preferred_element_type=jnp.float32)
                 + jnp.dot(vmeta, pm, preferred_element_type=jnp.float32))
            a_late = a[:, ATT_TILE:] + jnp.dot(value_tile(g, 2 * qi + 1), pb,
                                               preferred_element_type=jnp.float32)
            a = jnp.concatenate([a[:, :ATT_TILE], a_late], axis=1)
            o = a[:HEAD_DIM, :] / a[HEAD_DIM:HEAD_DIM + 1, :]
            o = o * sgT_ref[0, qi, hrows(g, HEAD_DIM), :].astype(jnp.float32)
            o_ref[0, qi, hrows(g, HEAD_DIM), :] = o.astype(jnp.bfloat16)
        return tuple(smax_next)

    def four_blocks(j, smax_first):
        for pos in range(4):
            smax_first = block(4 * j + pos, pos, smax_first)
        return smax_first

    smax_first = []
    for g in heads:
        acc_ref[0, g] = acc_zero
        smax_first.append(start_block(g, 0, qa_ref.at[0], smeta_ref.at[0], s_ref.at[0]))
    lax.fori_loop(0, n_qb // 4, four_blocks, tuple(smax_first))


def _out_proj_kernel(x_ref, gin_ref, bin_ref, aT_ref, p_ref, waT_ref, wp_ref, g_ref, b_ref, o_ref):
    for j in range(OUT_TILE // Q_TILE):
        rows = slice(j * Q_TILE, (j + 1) * Q_TILE)
        hn = _layer_norm(x_ref[0, rows, :], gin_ref[...], bin_ref[...])
        yT = jnp.dot(waT_ref[...], aT_ref[0, j], preferred_element_type=jnp.float32)
        y = yT.T + jnp.dot(p_ref[0, rows, :], wp_ref[...], preferred_element_type=jnp.float32)
        o_ref[0, rows, :] = _layer_norm(DEEPNORM_ALPHA * hn + y, g_ref[...], b_ref[...])


def _out_weight_kernel(wa_ref, wpin_ref, waT_ref, wp_ref):
    waT_ref[...] = wa_ref[0].T.astype(jnp.bfloat16)
    wp_ref[...] = wpin_ref[0].astype(jnp.bfloat16)


def _const_spec(shape):
    nd = len(shape)
    return pl.BlockSpec(shape, lambda *_: (0,) * nd, pipeline_mode=pl.Buffered(1))


def kernel(x, meta_tokens, ln_in_g, ln_in_b, w_in, b_forget, w_pool, pool_scale, w_out, ln_g, ln_b):
    B, L, D = x.shape
    assert D == D_MODEL and L % (4 * Q_TILE) == 0 and TOKEN_TILE % Q_TILE == 0 and w_in.shape[0] == 1
    bf16, f32 = jnp.bfloat16, jnp.float32
    assert w_in.shape[2] == IN_COLS and w_out.shape[1] == 2 * D_MODEL

    w_t = jnp.swapaxes(w_in, 1, 2)[0]
    square = jax.ShapeDtypeStruct((D, D), bf16)
    waT, wp = pl.pallas_call(
        _out_weight_kernel,
        grid=(D // PREP_ROWS,),
        in_specs=[pl.BlockSpec((1, PREP_ROWS, D), lambda r: (0, r, 0)),
                  pl.BlockSpec((1, PREP_ROWS, D), lambda r: (0, r + D // PREP_ROWS, 0))],
        out_specs=(pl.BlockSpec((D, PREP_ROWS), lambda r: (0, r)),
                   pl.BlockSpec((PREP_ROWS, D), lambda r: (r, 0))),
        out_shape=(square, square),
        compiler_params=pltpu.CompilerParams(dimension_semantics=("arbitrary",)),
        name="out_weight_prep",
    )(w_out, w_out)
    bf_col = jnp.pad(b_forget[0].astype(f32), (0, LANES - HEADS)).reshape(LANES, 1)
    wpool = w_pool[0].astype(bf16)
    pscale = pool_scale[0].reshape(1, D).astype(f32)
    g_in = ln_in_g.reshape(1, D).astype(f32)
    b_in = ln_in_b.reshape(1, D).astype(f32)
    g_out = ln_g[0].reshape(1, D).astype(f32)
    b_out = ln_b[0].reshape(1, D).astype(f32)
    mt_pad = jnp.pad(meta_tokens.astype(f32), ((0, META_PAD - N_META), (0, 0)))

    kw = HEADS * KEY_WIDTH
    whole = lambda shape: pl.BlockSpec(shape, lambda r: (0,) * len(shape))
    w_bf, kmeta, vTmeta, umeta = pl.pallas_call(
        _cast_meta_kernel,
        grid=(IN_COLS // CAST_ROWS,),
        in_specs=[pl.BlockSpec((CAST_ROWS, D), lambda r: (r, 0)),
                  whole((META_PAD, D)), whole((1, D)), whole((1, D)), whole((LANES, 1))],
        out_specs=(pl.BlockSpec((CAST_ROWS, D), lambda r: (r, 0)),
                   whole((META_PAD, kw)), whole((ATT_WIDTH, META_PAD)), whole((N_META, D))),
        out_shape=(jax.ShapeDtypeStruct((IN_COLS, D), bf16),
                   jax.ShapeDtypeStruct((META_PAD, kw), bf16),
                   jax.ShapeDtypeStruct((ATT_WIDTH, META_PAD), bf16),
                   jax.ShapeDtypeStruct((N_META, D), f32)),
        scratch_shapes=[pltpu.VMEM((IN_COLS, D), bf16)],
        compiler_params=pltpu.CompilerParams(
            dimension_semantics=("arbitrary",), vmem_limit_bytes=VMEM_LIMIT),
        name="cast_meta_proj",
    )(w_t, mt_pad, g_in, b_in, bf_col)

    n_t = L // TOKEN_TILE
    n_sub = TOKEN_TILE // ATT_TILE
    n_ab = L // ATT_TILE
    ktiled = jax.ShapeDtypeStruct((B, n_ab, ATT_WIDTH, ATT_TILE), bf16)
    ktiled_spec = pl.BlockSpec((1, n_sub, ATT_WIDTH, ATT_TILE), lambda b, t: (b, t, 0, 0))
    n_qb = L // Q_TILE
    qtiled = jax.ShapeDtypeStruct((B, n_qb, ATT_WIDTH, Q_TILE), bf16)
    qtiled_spec = pl.BlockSpec((1, TOKEN_TILE // Q_TILE, ATT_WIDTH, Q_TILE), lambda b, t: (b, t, 0, 0))
    qT, kk, vT, sgT, pmix = pl.pallas_call(
        _in_proj_kernel,
        grid=(B, n_t),
        in_specs=[
            pl.BlockSpec((1, TOKEN_TILE, D), lambda b, t: (b, t, 0)),
            _const_spec((1, D)), _const_spec((1, D)),
            _const_spec((IN_COLS, D)), _const_spec((LANES, 1)),
            _const_spec((len(POOL_WINDOWS), POOL_GROUP_WIDTH, POOL_GROUP_WIDTH)),
            _const_spec((1, D)), _const_spec((N_META, D)),
        ],
        out_specs=(qtiled_spec,
                   pl.BlockSpec((1, TOKEN_TILE, kw), lambda b, t: (b, t, 0)),
                   ktiled_spec, qtiled_spec,
                   pl.BlockSpec((1, TOKEN_TILE, D), lambda b, t: (b, t, 0))),
        out_shape=(qtiled, jax.ShapeDtypeStruct((B, L, kw), bf16), ktiled, qtiled,
                   jax.ShapeDtypeStruct((B, L, D), bf16)),
        scratch_shapes=[pltpu.VMEM((HEADS, LANES), f32), pltpu.VMEM((N_META, D), f32)],
        compiler_params=pltpu.CompilerParams(
            dimension_semantics=("arbitrary", "arbitrary"), vmem_limit_bytes=VMEM_LIMIT),
        name="in_proj",
    )(x, g_in, b_in, w_bf, bf_col, wpool, pscale, umeta)

    gd = HEAD_GROUP * HEAD_DIM
    head_ktiled = pl.BlockSpec((1, n_ab, gd, ATT_TILE), lambda b, h: (b, 0, h, 0))
    head_qtiled = pl.BlockSpec((1, n_qb, gd, Q_TILE), lambda b, h: (b, 0, h, 0))
    aT = pl.pallas_call(
        _attn_kernel,
        grid=(B, HEADS // HEAD_GROUP),
        in_specs=[
            head_qtiled,
            pl.BlockSpec((1, L, HEAD_GROUP * KEY_WIDTH), lambda b, h: (b, 0, h)),
            head_ktiled, head_qtiled,
            pl.BlockSpec((META_PAD, HEAD_GROUP * KEY_WIDTH), lambda b, h: (0, h)),
            pl.BlockSpec((gd, META_PAD), lambda b, h: (h, 0)),
        ],
        out_specs=head_qtiled,
        out_shape=qtiled,
        scratch_shapes=[pltpu.VMEM((2, HEAD_GROUP, KEY_WIDTH, Q_TILE), bf16),
                        pltpu.VMEM((3, HEAD_GROUP, ATT_TILE, Q_TILE), f32),
                        pltpu.VMEM((2, HEAD_GROUP, N_META, Q_TILE), f32),
                        pltpu.VMEM((2, HEAD_GROUP, HEAD_DIM + ONES_ROWS, Q_TILE), f32)],
        compiler_params=pltpu.CompilerParams(
            dimension_semantics=("arbitrary", "arbitrary"), vmem_limit_bytes=VMEM_LIMIT),
        name="fox_attn",
    )(qT, kk, vT, sgT, kmeta, vTmeta)

    out = pl.pallas_call(
        _out_proj_kernel,
        grid=(B, L // OUT_TILE),
        in_specs=[
            pl.BlockSpec((1, OUT_TILE, D), lambda b, t: (b, t, 0)),
            _const_spec((1, D)), _const_spec((1, D)),
            pl.BlockSpec((1, OUT_TILE // Q_TILE, ATT_WIDTH, Q_TILE), lambda b, t: (b, t, 0, 0)),
            pl.BlockSpec((1, OUT_TILE, D), lambda b, t: (b, t, 0)),
            _const_spec((D, ATT_WIDTH)), _const_spec((D, D)),
            _const_spec((1, D)), _const_spec((1, D)),
        ],
        out_specs=pl.BlockSpec((1, OUT_TILE, D), lambda b, t: (b, t, 0)),
        out_shape=jax.ShapeDtypeStruct((B, L, D), x.dtype),
        compiler_params=pltpu.CompilerParams(
            dimension_semantics=("arbitrary", "arbitrary"), vmem_limit_bytes=VMEM_LIMIT),
        name="out_proj",
    )(x, g_in, b_in, aT, pmix, waT, wp, g_out, b_out)
    return out
```

```python
import math

import jax
import jax.numpy as jnp
from jax import lax
from jax.experimental import pallas as pl
from jax.experimental.pallas import tpu as pltpu

D_MODEL = 1024
N_META = 16
HEADS = 16
HEAD_DIM = 64
ATT_WIDTH = HEADS * HEAD_DIM
POOL_WINDOWS = (2, 4, 8, 16)
POOL_GROUP_WIDTH = D_MODEL // len(POOL_WINDOWS)
LN_EPS = 1e-5
DEEPNORM_ALPHA = 2.0 ** 0.25
LOG2E = math.log2(math.e)
Q_SCALE = HEAD_DIM ** -0.5 * LOG2E

LANES = 128
PAIR_WIDTH = 2 * HEAD_DIM
KEY_WIDTH = LANES
META_PAD = 128
TOKEN_TILE = 1024
OUT_TILE = 1024
PREP_ROWS = 256
OUT_CHUNKS = D_MODEL // PREP_ROWS
CAST_ROWS = 560
O_Q, O_K, O_V, O_F = 0, ATT_WIDTH, 2 * ATT_WIDTH, 3 * ATT_WIDTH
O_G = O_F + HEADS
O_U = O_G + ATT_WIDTH
O_GP = O_U + D_MODEL
IN_COLS = O_GP + D_MODEL
ATT_TILE = 256
Q_TILE = 2 * ATT_TILE
ONES_ROWS = 16
HEAD_GROUP = 4
VMEM_LIMIT = 56 * 1024 * 1024

_NT = (((1,), (1,)), ((), ()))


def _layer_norm(x, g, b):
    mu = jnp.mean(x, axis=-1, keepdims=True)
    xc = x - mu
    var = jnp.mean(xc * xc, axis=-1, keepdims=True)
    return xc * lax.rsqrt(var + LN_EPS) * g + b


def _log_sigmoid(z):
    return jnp.minimum(z, 0.0) - jnp.log(1.0 + jnp.exp(-jnp.abs(z)))


def _silu(z):
    return z / (1.0 + jnp.exp(-z))


def _proj(w_ref, hb, lo, n):
    return lax.dot_general(hb, w_ref[lo:lo + n, :], _NT, preferred_element_type=jnp.float32)


def _proj_t(w_ref, hb, lo, n):
    return lax.dot_general(w_ref[lo:lo + n, :], hb, _NT, preferred_element_type=jnp.float32)


def _lane_cumsum(x):
    n = x.shape[-1]
    lane = lax.broadcasted_iota(jnp.int32, x.shape, x.ndim - 1)
    d = 1
    while d < n:
        x = x + jnp.where(lane >= d, pltpu.roll(x, d, x.ndim - 1), 0.0)
        d *= 2
    return x


def _bias_columns(c_rows):
    hi = c_rows.astype(jnp.bfloat16).astype(jnp.float32)
    r1 = c_rows - hi
    mid = r1.astype(jnp.bfloat16).astype(jnp.float32)
    lo = (r1 - mid).astype(jnp.bfloat16).astype(jnp.float32)
    return hi + pltpu.roll(mid, HEADS, 1) + pltpu.roll(lo, 2 * HEADS, 1)


def _key_operand(kk, bias_lanes):
    low = lax.broadcasted_iota(jnp.int32, bias_lanes.shape, 1) < HEAD_DIM
    upper = pltpu.roll(bias_lanes, HEAD_DIM, 1)
    parts = []
    for j in range(HEADS // 2):
        pair = kk[:, j * PAIR_WIDTH:(j + 1) * PAIR_WIDTH]
        parts.append(jnp.where(low, pair, upper))
        parts.append(jnp.where(low, pltpu.roll(pair, HEAD_DIM, 1), upper))
    return jnp.concatenate(parts, axis=1).astype(jnp.bfloat16)


def _pool_mix(u_ext, u, wpool_ref, pscale, gate):
    outs = []
    for gi, w in enumerate(POOL_WINDOWS):
        sl = slice(gi * POOL_GROUP_WIDTH, (gi + 1) * POOL_GROUP_WIDTH)
        r = u_ext[:, sl]
        s = 1
        while s < w:
            r = r + pltpu.roll(r, s, 0)
            s *= 2
        d = r[N_META:, :] * (1.0 / w) - u[:, sl]
        outs.append(jnp.dot(d.astype(jnp.bfloat16), wpool_ref[gi],
                            preferred_element_type=jnp.float32))
    y = jnp.concatenate(outs, axis=-1) * pscale
    return y * gate


def _meta_kernel(mt_ref, g_ref, b_ref, w_ref, bf_ref,
                 kmeta_ref, vTmeta_ref, umeta_ref):
    hn = _layer_norm(mt_ref[...], g_ref[...], b_ref[...])
    hb = hn.astype(jnp.bfloat16)
    flT = _proj_t(w_ref, hb, O_F, LANES)
    row = lax.broadcasted_iota(jnp.int32, flT.shape, 0)
    col = lax.broadcasted_iota(jnp.int32, flT.shape, 1)
    valid = (row < HEADS) & (col < N_META)
    logf = jnp.where(valid, _log_sigmoid(flT + bf_ref[...]), 0.0)
    cT = _lane_cumsum(logf)
    total = cT[:, META_PAD - 1:META_PAD]
    c_rows = ((cT - total) * LOG2E).T
    kmeta_ref[...] = _key_operand(_proj(w_ref, hb, O_K, ATT_WIDTH), _bias_columns(c_rows))
    vTmeta_ref[...] = _proj_t(w_ref, hb, O_V, ATT_WIDTH).astype(jnp.bfloat16)
    u = _proj(w_ref, hb, O_U, D_MODEL)
    umeta_ref[...] = u[:N_META, :]


def _cast_meta_kernel(wchunk_ref, wa_ref, wpin_ref, mt_ref, g_ref, b_ref, bf_ref,
                      wbf_ref, waT_ref, wp_ref, kmeta_ref, vTmeta_ref, umeta_ref, wfull_ref):
    r = pl.program_id(0)

    @pl.when(r < OUT_CHUNKS)
    def _():
        waT_ref[...] = wa_ref[0].T.astype(jnp.bfloat16)
        wp_ref[...] = wpin_ref[0].astype(jnp.bfloat16)

    chunk = wchunk_ref[...].astype(jnp.bfloat16)
    wbf_ref[...] = chunk
    wfull_ref[pl.ds(pl.multiple_of(r * CAST_ROWS, CAST_ROWS), CAST_ROWS), :] = chunk

    @pl.when(r == pl.num_programs(0) - 1)
    def _():
        _meta_kernel(mt_ref, g_ref, b_ref, wfull_ref, bf_ref, kmeta_ref, vTmeta_ref, umeta_ref)


def _in_proj_kernel(x_ref, g_ref, b_ref, w_ref, bf_ref, wpool_ref, pscale_ref, umeta_ref,
                    qT_ref, k_ref, vT_ref, sgT_ref, p_ref, carry_c, carry_u):
    t = pl.program_id(1)

    @pl.when(t == 0)
    def _():
        carry_c[...] = jnp.zeros_like(carry_c)
        carry_u[...] = umeta_ref[...]

    c_carry = carry_c[:, 0:1]
    u_carry = carry_u[...]
    sub = ATT_TILE
    per_q = Q_TILE // sub
    for h in range(TOKEN_TILE // sub):
        rows = slice(h * sub, (h + 1) * sub)
        qcols = slice((h % per_q) * sub, (h % per_q + 1) * sub)
        hn = _layer_norm(x_ref[0, rows, :], g_ref[...], b_ref[...])
        hb = hn.astype(jnp.bfloat16)

        qT = _proj_t(w_ref, hb, O_Q, ATT_WIDTH) * Q_SCALE
        qT_ref[0, h // per_q, :, qcols] = qT.astype(jnp.bfloat16)
        vT_ref[0, h] = _proj_t(w_ref, hb, O_V, ATT_WIDTH).astype(jnp.bfloat16)
        gT = _proj_t(w_ref, hb, O_G, ATT_WIDTH)
        sgT_ref[0, h // per_q, :, qcols] = _silu(gT).astype(jnp.bfloat16)

        flT = _proj_t(w_ref, hb, O_F, LANES)
        logf = _log_sigmoid(flT[:HEADS, :] + bf_ref[:HEADS, :])
        cT = _lane_cumsum(logf) + c_carry
        c_carry = cT[:, sub - 1:sub]
        cT_pad = jnp.concatenate(
            [cT * LOG2E, jnp.zeros((LANES - HEADS, sub), jnp.float32)], axis=0)
        k_ref[0, rows, :] = _key_operand(_proj(w_ref, hb, O_K, ATT_WIDTH), _bias_columns(cT_pad.T))

        u = _proj(w_ref, hb, O_U, D_MODEL)
        gp = _proj(w_ref, hb, O_GP, D_MODEL)
        u_ext = jnp.concatenate([u_carry, u], axis=0)
        u_carry = u[sub - N_META:, :]
        p = _pool_mix(u_ext, u, wpool_ref, pscale_ref[...], _silu(gp))
        p_ref[0, rows, :] = p.astype(jnp.bfloat16)
    carry_c[...] = jnp.broadcast_to(c_carry, carry_c.shape)
    carry_u[...] = u_carry


def _attn_kernel(qT_ref, k_ref, vT_ref, sgT_ref, kmeta_ref, vTmeta_ref, o_ref,
                 qa_ref, s_ref, smeta_ref, acc_ref):
    n_qb = qT_ref.shape[1]
    heads = range(HEAD_GROUP)
    sel_row = lax.broadcasted_iota(jnp.int32, (KEY_WIDTH - HEAD_DIM, Q_TILE), 0)
    for g in heads:
        h = pl.program_id(1) * HEAD_GROUP + g
        picks = (sel_row == h) | (sel_row == HEADS + h) | (sel_row == 2 * HEADS + h)
        for par in range(2):
            qa_ref[par, g, HEAD_DIM:, :] = jnp.where(picks, -1.0, 0.0).astype(jnp.bfloat16)
    ones = jnp.ones((ONES_ROWS, ATT_TILE), jnp.bfloat16)
    ones_meta = jnp.ones((ONES_ROWS, META_PAD), jnp.bfloat16)
    krow = lax.broadcasted_iota(jnp.int32, (ATT_TILE, Q_TILE), 0)
    qcol = lax.broadcasted_iota(jnp.int32, (ATT_TILE, Q_TILE), 1)
    causal_a = krow <= qcol
    causal_b = causal_a[:, :ATT_TILE]
    meta_fill = jnp.zeros((META_PAD - N_META, Q_TILE), jnp.bfloat16)
    acc_zero = jnp.zeros(acc_ref.shape[2:], jnp.float32)

    def hrows(g, width):
        return slice(g * width, (g + 1) * width)

    def colmax(s):
        return jnp.max(s, axis=0, keepdims=True)

    def key_tile(g, kj):
        rows = pl.ds(pl.multiple_of(kj * ATT_TILE, ATT_TILE), ATT_TILE)
        return k_ref[0, rows, hrows(g, KEY_WIDTH)]

    def value_tile(g, kj):
        return jnp.concatenate([vT_ref[0, kj, hrows(g, HEAD_DIM), :], ones], axis=0)

    def head_scores(g, kj, qa, dst):
        s = jnp.dot(key_tile(g, kj), qa[g], preferred_element_type=jnp.float32)
        dst[g] = s
        return colmax(s)

    def head_consume(g, kj, src, acc, smax, m):
        m_new = jnp.maximum(m, smax)
        alpha = jnp.exp2(m - m_new)
        p = jnp.exp2(src[g] - m_new).astype(jnp.bfloat16)
        acc[g] = alpha * acc[g] + jnp.dot(value_tile(g, kj), p, preferred_element_type=jnp.float32)
        return m_new

    def start_block(g, qi, qa, smeta, dst):
        qa[g, :HEAD_DIM, :] = qT_ref[0, qi, hrows(g, HEAD_DIM), :]
        smeta[g] = jnp.dot(kmeta_ref[:N_META, hrows(g, KEY_WIDTH)], qa[g],
                           preferred_element_type=jnp.float32)
        return head_scores(g, 0, qa, dst)

    def block(qi, pos, smax_first):
        par = pos % 2
        qa, qa_nx = qa_ref.at[par], qa_ref.at[1 - par]
        smeta, smeta_nx = smeta_ref.at[par], smeta_ref.at[1 - par]
        acc, acc_nx = acc_ref.at[par], acc_ref.at[1 - par]
        s_first, s_other, s_nx = s_ref.at[2 * par], s_ref.at[1], s_ref.at[2 - 2 * par]
        m0 = tuple(jnp.full((1, Q_TILE), -jnp.inf, jnp.float32) for _ in heads)

        def step(k_next, dst, k_cur, src, smax_cur, ms):
            smax_next, ms_new = [], []
            for g in heads:
                smax_next.append(head_scores(g, k_next, qa, dst))
                ms_new.append(head_consume(g, k_cur, src, acc, smax_cur[g], ms[g]))
            return tuple(ms_new), tuple(smax_next)

        def pair(t, ms, smax):
            ms, smax_o = step(t + 1, s_other, t, s_first, smax, ms)
            return step(t + 2, s_first, t + 1, s_other, smax_o, ms)

        def quad(k, carry):
            return pair(4 * k + 2, *pair(4 * k, *carry))

        def octo(k, carry):
            return quad(2 * k + 1, quad(2 * k, carry))

        ms, smax = lax.fori_loop(0, qi // 4, octo, (m0, smax_first))
        for t in range(pos):
            ms, smax = pair(2 * qi - 2 * pos + 2 * t, ms, smax)

        q_next = jnp.minimum(qi + 1, n_qb - 1)
        sbs = [jnp.dot(key_tile(g, 2 * qi + 1), qa[g][:, ATT_TILE:],
                       preferred_element_type=jnp.float32) for g in heads]
        smax_next = [start_block(g, q_next, qa_nx, smeta_nx, s_nx) for g in heads]
        for g in heads:
            sb = jnp.where(causal_b, sbs[g], -jnp.inf)
            s = jnp.where(causal_a, s_first[g], -jnp.inf)
            sm = smeta[g]
            m_new = jnp.maximum(jnp.maximum(ms[g], colmax(s)), colmax(sm))
            m_new = jnp.concatenate(
                [m_new[:, :ATT_TILE], jnp.maximum(m_new[:, ATT_TILE:], colmax(sb))], axis=1)
            alpha = jnp.exp2(ms[g] - m_new)
            p = jnp.exp2(s - m_new).astype(jnp.bfloat16)
            pb = jnp.exp2(sb - m_new[:, ATT_TILE:]).astype(jnp.bfloat16)
            pm = jnp.concatenate([jnp.exp2(sm - m_new).astype(jnp.bfloat16), meta_fill], axis=0)
            acc_nx[g] = acc_zero
            vmeta = jnp.concatenate([vTmeta_ref[hrows(g, HEAD_DIM), :], ones_meta], axis=0)
            a = (alpha * acc[g]
                 + jnp.dot(value_tile(g, 2 * qi), p, preferred_element_type=jnp.float32)
                 + jnp.dot(vmeta, pm, preferred_element_type=jnp.float32))
            a_late = a[:, ATT_TILE:] + jnp.dot(value_tile(g, 2 * qi + 1), pb,
                                               preferred_element_type=jnp.float32)
            a = jnp.concatenate([a[:, :ATT_TILE], a_late], axis=1)
            o = a[:HEAD_DIM, :] / a[HEAD_DIM:HEAD_DIM + 1, :]
            o = o * sgT_ref[0, qi, hrows(g, HEAD_DIM), :].astype(jnp.float32)
            o_ref[0, qi, hrows(g, HEAD_DIM), :] = o.astype(jnp.bfloat16)
        return tuple(smax_next)

    def four_blocks(j, smax_first):
        for pos in range(4):
            smax_first = block(4 * j + pos, pos, smax_first)
        return smax_first

    smax_first = []
    for g in heads:
        acc_ref[0, g] = acc_zero
        smax_first.append(start_block(g, 0, qa_ref.at[0], smeta_ref.at[0], s_ref.at[0]))
    lax.fori_loop(0, n_qb // 4, four_blocks, tuple(smax_first))


def _out_proj_kernel(x_ref, gin_ref, bin_ref, aT_ref, p_ref, waT_ref, wp_ref, g_ref, b_ref, o_ref):
    for j in range(OUT_TILE // Q_TILE):
        rows = slice(j * Q_TILE, (j + 1) * Q_TILE)
        hn = _layer_norm(x_ref[0, rows, :], gin_ref[...], bin_ref[...])
        yT = jnp.dot(waT_ref[...], aT_ref[0, j], preferred_element_type=jnp.float32)
        y = yT.T + jnp.dot(p_ref[0, rows, :], wp_ref[...], preferred_element_type=jnp.float32)
        o_ref[0, rows, :] = _layer_norm(DEEPNORM_ALPHA * hn + y, g_ref[...], b_ref[...])


def _const_spec(shape):
    nd = len(shape)
    return pl.BlockSpec(shape, lambda *_: (0,) * nd, pipeline_mode=pl.Buffered(1))


def kernel(x, meta_tokens, ln_in_g, ln_in_b, w_in, b_forget, w_pool, pool_scale, w_out, ln_g, ln_b):
    B, L, D = x.shape
    assert D == D_MODEL and L % (4 * Q_TILE) == 0 and TOKEN_TILE % Q_TILE == 0 and w_in.shape[0] == 1
    bf16, f32 = jnp.bfloat16, jnp.float32
    assert w_in.shape[2] == IN_COLS and w_out.shape[1] == 2 * D_MODEL

    w_t = jnp.swapaxes(w_in, 1, 2)[0]
    square = jax.ShapeDtypeStruct((D, D), bf16)
    bf_col = jnp.pad(b_forget[0].astype(f32), (0, LANES - HEADS)).reshape(LANES, 1)
    wpool = w_pool[0].astype(bf16)
    pscale = pool_scale[0].reshape(1, D).astype(f32)
    g_in = ln_in_g.reshape(1, D).astype(f32)
    b_in = ln_in_b.reshape(1, D).astype(f32)
    g_out = ln_g[0].reshape(1, D).astype(f32)
    b_out = ln_b[0].reshape(1, D).astype(f32)
    mt_pad = jnp.pad(meta_tokens.astype(f32), ((0, META_PAD - N_META), (0, 0)))

    kw = HEADS * KEY_WIDTH
    whole = lambda shape: pl.BlockSpec(shape, lambda r: (0,) * len(shape))
    oc = lambda r: jnp.minimum(r, OUT_CHUNKS - 1)
    assert OUT_CHUNKS <= IN_COLS // CAST_ROWS
    w_bf, waT, wp, kmeta, vTmeta, umeta = pl.pallas_call(
        _cast_meta_kernel,
        grid=(IN_COLS // CAST_ROWS,),
        in_specs=[pl.BlockSpec((CAST_ROWS, D), lambda r: (r, 0)),
                  pl.BlockSpec((1, PREP_ROWS, D), lambda r: (0, oc(r), 0)),
                  pl.BlockSpec((1, PREP_ROWS, D), lambda r: (0, oc(r) + OUT_CHUNKS, 0)),
                  whole((META_PAD, D)), whole((1, D)), whole((1, D)), whole((LANES, 1))],
        out_specs=(pl.BlockSpec((CAST_ROWS, D), lambda r: (r, 0)),
                   pl.BlockSpec((D, PREP_ROWS), lambda r: (0, oc(r))),
                   pl.BlockSpec((PREP_ROWS, D), lambda r: (oc(r), 0)),
                   whole((META_PAD, kw)), whole((ATT_WIDTH, META_PAD)), whole((N_META, D))),
        out_shape=(jax.ShapeDtypeStruct((IN_COLS, D), bf16), square, square,
                   jax.ShapeDtypeStruct((META_PAD, kw), bf16),
                   jax.ShapeDtypeStruct((ATT_WIDTH, META_PAD), bf16),
                   jax.ShapeDtypeStruct((N_META, D), f32)),
        scratch_shapes=[pltpu.VMEM((IN_COLS, D), bf16)],
        compiler_params=pltpu.CompilerParams(
            dimension_semantics=("arbitrary",), vmem_limit_bytes=VMEM_LIMIT),
        name="cast_meta_proj",
    )(w_t, w_out, w_out, mt_pad, g_in, b_in, bf_col)

    n_t = L // TOKEN_TILE
    n_sub = TOKEN_TILE // ATT_TILE
    n_ab = L // ATT_TILE
    ktiled = jax.ShapeDtypeStruct((B, n_ab, ATT_WIDTH, ATT_TILE), bf16)
    ktiled_spec = pl.BlockSpec((1, n_sub, ATT_WIDTH, ATT_TILE), lambda b, t: (b, t, 0, 0))
    n_qb = L // Q_TILE
    qtiled = jax.ShapeDtypeStruct((B, n_qb, ATT_WIDTH, Q_TILE), bf16)
    qtiled_spec = pl.BlockSpec((1, TOKEN_TILE // Q_TILE, ATT_WIDTH, Q_TILE), lambda b, t: (b, t, 0, 0))
    qT, kk, vT, sgT, pmix = pl.pallas_call(
        _in_proj_kernel,
        grid=(B, n_t),
        in_specs=[
            pl.BlockSpec((1, TOKEN_TILE, D), lambda b, t: (b, t, 0)),
            _const_spec((1, D)), _const_spec((1, D)),
            _const_spec((IN_COLS, D)), _const_spec((LANES, 1)),
            _const_spec((len(POOL_WINDOWS), POOL_GROUP_WIDTH, POOL_GROUP_WIDTH)),
            _const_spec((1, D)), _const_spec((N_META, D)),
        ],
        out_specs=(qtiled_spec,
                   pl.BlockSpec((1, TOKEN_TILE, kw), lambda b, t: (b, t, 0)),
                   ktiled_spec, qtiled_spec,
                   pl.BlockSpec((1, TOKEN_TILE, D), lambda b, t: (b, t, 0))),
        out_shape=(qtiled, jax.ShapeDtypeStruct((B, L, kw), bf16), ktiled, qtiled,
                   jax.ShapeDtypeStruct((B, L, D), bf16)),
        scratch_shapes=[pltpu.VMEM((HEADS, LANES), f32), pltpu.VMEM((N_META, D), f32)],
        compiler_params=pltpu.CompilerParams(
            dimension_semantics=("arbitrary", "arbitrary"), vmem_limit_bytes=VMEM_LIMIT),
        name="in_proj",
    )(x, g_in, b_in, w_bf, bf_col, wpool, pscale, umeta)

    gd = HEAD_GROUP * HEAD_DIM
    head_ktiled = pl.BlockSpec((1, n_ab, gd, ATT_TILE), lambda b, h: (b, 0, h, 0))
    head_qtiled = pl.BlockSpec((1, n_qb, gd, Q_TILE), lambda b, h: (b, 0, h, 0))
    aT = pl.pallas_call(
        _attn_kernel,
        grid=(B, HEADS // HEAD_GROUP),
        in_specs=[
            head_qtiled,
            pl.BlockSpec((1, L, HEAD_GROUP * KEY_WIDTH), lambda b, h: (b, 0, h)),
            head_ktiled, head_qtiled,
            pl.BlockSpec((META_PAD, HEAD_GROUP * KEY_WIDTH), lambda b, h: (0, h)),
            pl.BlockSpec((gd, META_PAD), lambda b, h: (h, 0)),
        ],
        out_specs=head_qtiled,
        out_shape=qtiled,
        scratch_shapes=[pltpu.VMEM((2, HEAD_GROUP, KEY_WIDTH, Q_TILE), bf16),
                        pltpu.VMEM((3, HEAD_GROUP, ATT_TILE, Q_TILE), f32),
                        pltpu.VMEM((2, HEAD_GROUP, N_META, Q_TILE), f32),
                        pltpu.VMEM((2, HEAD_GROUP, HEAD_DIM + ONES_ROWS, Q_TILE), f32)],
        compiler_params=pltpu.CompilerParams(
            dimension_semantics=("arbitrary", "arbitrary"), vmem_limit_bytes=VMEM_LIMIT),
        name="fox_attn",
    )(qT, kk, vT, sgT, kmeta, vTmeta)

    out = pl.pallas_call(
        _out_proj_kernel,
        grid=(B, L // OUT_TILE),
        in_specs=[
            pl.BlockSpec((1, OUT_TILE, D), lambda b, t: (b, t, 0)),
            _const_spec((1, D)), _const_spec((1, D)),
            pl.BlockSpec((1, OUT_TILE // Q_TILE, ATT_WIDTH, Q_TILE), lambda b, t: (b, t, 0, 0)),
            pl.BlockSpec((1, OUT_TILE, D), lambda b, t: (b, t, 0)),
            _const_spec((D, ATT_WIDTH)), _const_spec((D, D)),
            _const_spec((1, D)), _const_spec((1, D)),
        ],
        out_specs=pl.BlockSpec((1, OUT_TILE, D), lambda b, t: (b, t, 0)),
        out_shape=jax.ShapeDtypeStruct((B, L, D), x.dtype),
        compiler_params=pltpu.CompilerParams(
            dimension_semantics=("arbitrary", "arbitrary"), vmem_limit_bytes=VMEM_LIMIT),
        name="out_proj",
    )(x, g_in, b_in, aT, pmix, waT, wp, g_out, b_out)
    return out
```

```python
import math

import jax
import jax.numpy as jnp
from jax import lax
from jax.experimental import pallas as pl
from jax.experimental.pallas import tpu as pltpu

D_MODEL = 1024
N_META = 16
HEADS = 16
HEAD_DIM = 64
ATT_WIDTH = HEADS * HEAD_DIM
POOL_WINDOWS = (2, 4, 8, 16)
POOL_GROUP_WIDTH = D_MODEL // len(POOL_WINDOWS)
LN_EPS = 1e-5
DEEPNORM_ALPHA = 2.0 ** 0.25
LOG2E = math.log2(math.e)
Q_SCALE = HEAD_DIM ** -0.5 * LOG2E

LANES = 128
PAIR_WIDTH = 2 * HEAD_DIM
KEY_WIDTH = LANES
META_PAD = 128
TOKEN_TILE = 1024
OUT_TILE = 1024
CAST_ROWS = 560
O_Q, O_K, O_V, O_F = 0, ATT_WIDTH, 2 * ATT_WIDTH, 3 * ATT_WIDTH
O_G = O_F + HEADS
O_U = O_G + ATT_WIDTH
O_GP = O_U + D_MODEL
IN_COLS = O_GP + D_MODEL
ATT_TILE = 256
Q_TILE = 2 * ATT_TILE
ONES_ROWS = 16
HEAD_GROUP = 4
VMEM_LIMIT = 56 * 1024 * 1024

_NT = (((1,), (1,)), ((), ()))


def _layer_norm(x, g, b):
    mu = jnp.mean(x, axis=-1, keepdims=True)
    xc = x - mu
    var = jnp.mean(xc * xc, axis=-1, keepdims=True)
    return xc * lax.rsqrt(var + LN_EPS) * g + b


def _log_sigmoid(z):
    return jnp.minimum(z, 0.0) - jnp.log(1.0 + jnp.exp(-jnp.abs(z)))


def _silu(z):
    return z / (1.0 + jnp.exp(-z))


def _proj(w_ref, hb, lo, n):
    return lax.dot_general(hb, w_ref[lo:lo + n, :], _NT, preferred_element_type=jnp.float32)


def _proj_t(w_ref, hb, lo, n):
    return lax.dot_general(w_ref[lo:lo + n, :], hb, _NT, preferred_element_type=jnp.float32)


def _lane_cumsum(x):
    n = x.shape[-1]
    lane = lax.broadcasted_iota(jnp.int32, x.shape, x.ndim - 1)
    d = 1
    while d < n:
        x = x + jnp.where(lane >= d, pltpu.roll(x, d, x.ndim - 1), 0.0)
        d *= 2
    return x


def _bias_columns(c_rows):
    hi = c_rows.astype(jnp.bfloat16).astype(jnp.float32)
    r1 = c_rows - hi
    mid = r1.astype(jnp.bfloat16).astype(jnp.float32)
    lo = (r1 - mid).astype(jnp.bfloat16).astype(jnp.float32)
    return hi + pltpu.roll(mid, HEADS, 1) + pltpu.roll(lo, 2 * HEADS, 1)


def _key_operand(kk, bias_lanes):
    low = lax.broadcasted_iota(jnp.int32, bias_lanes.shape, 1) < HEAD_DIM
    upper = pltpu.roll(bias_lanes, HEAD_DIM, 1)
    parts = []
    for j in range(HEADS // 2):
        pair = kk[:, j * PAIR_WIDTH:(j + 1) * PAIR_WIDTH]
        parts.append(jnp.where(low, pair, upper))
        parts.append(jnp.where(low, pltpu.roll(pair, HEAD_DIM, 1), upper))
    return jnp.concatenate(parts, axis=1).astype(jnp.bfloat16)


def _pool_mix(u_ext, u, wpool_ref, pscale, gate):
    outs = []
    for gi, w in enumerate(POOL_WINDOWS):
        sl = slice(gi * POOL_GROUP_WIDTH, (gi + 1) * POOL_GROUP_WIDTH)
        r = u_ext[:, sl]
        s = 1
        while s < w:
            r = r + pltpu.roll(r, s, 0)
            s *= 2
        d = r[N_META:, :] * (1.0 / w) - u[:, sl]
        outs.append(jnp.dot(d.astype(jnp.bfloat16), wpool_ref[gi],
                            preferred_element_type=jnp.float32))
    y = jnp.concatenate(outs, axis=-1) * pscale
    return y * gate


def _meta_kernel(mt_ref, g_ref, b_ref, w_ref, bf_ref,
                 kmeta_ref, vTmeta_ref, umeta_ref):
    hn = _layer_norm(mt_ref[...], g_ref[...], b_ref[...])
    hb = hn.astype(jnp.bfloat16)
    flT = _proj_t(w_ref, hb, O_F, LANES)
    row = lax.broadcasted_iota(jnp.int32, flT.shape, 0)
    col = lax.broadcasted_iota(jnp.int32, flT.shape, 1)
    valid = (row < HEADS) & (col < N_META)
    logf = jnp.where(valid, _log_sigmoid(flT + bf_ref[...]), 0.0)
    cT = _lane_cumsum(logf)
    total = cT[:, META_PAD - 1:META_PAD]
    c_rows = ((cT - total) * LOG2E).T
    kmeta_ref[...] = _key_operand(_proj(w_ref, hb, O_K, ATT_WIDTH), _bias_columns(c_rows))
    vTmeta_ref[...] = _proj_t(w_ref, hb, O_V, ATT_WIDTH).astype(jnp.bfloat16)
    u = _proj(w_ref, hb, O_U, D_MODEL)
    umeta_ref[...] = u[:N_META, :]


def _cast_meta_kernel(wchunk_ref, mt_ref, g_ref, b_ref, bf_ref,
                      wbf_ref, kmeta_ref, vTmeta_ref, umeta_ref, wfull_ref):
    r = pl.program_id(0)
    chunk = wchunk_ref[...].astype(jnp.bfloat16)
    wbf_ref[...] = chunk
    wfull_ref[pl.ds(pl.multiple_of(r * CAST_ROWS, CAST_ROWS), CAST_ROWS), :] = chunk

    @pl.when(r == pl.num_programs(0) - 1)
    def _():
        _meta_kernel(mt_ref, g_ref, b_ref, wfull_ref, bf_ref, kmeta_ref, vTmeta_ref, umeta_ref)


def _in_proj_kernel(x_ref, g_ref, b_ref, w_ref, bf_ref, wpool_ref, pscale_ref, umeta_ref,
                    qT_ref, k_ref, vT_ref, sgT_ref, p_ref, carry_c, carry_u):
    t = pl.program_id(1)

    @pl.when(t == 0)
    def _():
        carry_c[...] = jnp.zeros_like(carry_c)
        carry_u[...] = umeta_ref[...]

    c_carry = carry_c[:, 0:1]
    u_carry = carry_u[...]
    sub = ATT_TILE
    per_q = Q_TILE // sub
    for h in range(TOKEN_TILE // sub):
        rows = slice(h * sub, (h + 1) * sub)
        qcols = slice((h % per_q) * sub, (h % per_q + 1) * sub)
        hn = _layer_norm(x_ref[0, rows, :], g_ref[...], b_ref[...])
        hb = hn.astype(jnp.bfloat16)

        qT = _proj_t(w_ref, hb, O_Q, ATT_WIDTH) * Q_SCALE
        qT_ref[0, h // per_q, :, qcols] = qT.astype(jnp.bfloat16)
        vT_ref[0, h] = _proj_t(w_ref, hb, O_V, ATT_WIDTH).astype(jnp.bfloat16)
        gT = _proj_t(w_ref, hb, O_G, ATT_WIDTH)
        sgT_ref[0, h // per_q, :, qcols] = _silu(gT).astype(jnp.bfloat16)

        flT = _proj_t(w_ref, hb, O_F, LANES)
        logf = _log_sigmoid(flT[:HEADS, :] + bf_ref[:HEADS, :])
        cT = _lane_cumsum(logf) + c_carry
        c_carry = cT[:, sub - 1:sub]
        cT_pad = jnp.concatenate(
            [cT * LOG2E, jnp.zeros((LANES - HEADS, sub), jnp.float32)], axis=0)
        k_ref[0, rows, :] = _key_operand(_proj(w_ref, hb, O_K, ATT_WIDTH), _bias_columns(cT_pad.T))

        u = _proj(w_ref, hb, O_U, D_MODEL)
        gp = _proj(w_ref, hb, O_GP, D_MODEL)
        u_ext = jnp.concatenate([u_carry, u], axis=0)
        u_carry = u[sub - N_META:, :]
        p = _pool_mix(u_ext, u, wpool_ref, pscale_ref[...], _silu(gp))
        p_ref[0, rows, :] = p.astype(jnp.bfloat16)
    carry_c[...] = jnp.broadcast_to(c_carry, carry_c.shape)
    carry_u[...] = u_carry


def _attn_kernel(qT_ref, k_ref, vT_ref, sgT_ref, kmeta_ref, vTmeta_ref, o_ref,
                 qa_ref, s_ref, smeta_ref, acc_ref):
    n_qb = qT_ref.shape[1]
    heads = range(HEAD_GROUP)
    sel_row = lax.broadcasted_iota(jnp.int32, (KEY_WIDTH - HEAD_DIM, Q_TILE), 0)
    for g in heads:
        h = pl.program_id(1) * HEAD_GROUP + g
        picks = (sel_row == h) | (sel_row == HEADS + h) | (sel_row == 2 * HEADS + h)
        for par in range(2):
            qa_ref[par, g, HEAD_DIM:, :] = jnp.where(picks, -1.0, 0.0).astype(jnp.bfloat16)
    ones = jnp.ones((ONES_ROWS, ATT_TILE), jnp.bfloat16)
    ones_meta = jnp.ones((ONES_ROWS, META_PAD), jnp.bfloat16)
    krow = lax.broadcasted_iota(jnp.int32, (ATT_TILE, Q_TILE), 0)
    qcol = lax.broadcasted_iota(jnp.int32, (ATT_TILE, Q_TILE), 1)
    causal_a = krow <= qcol
    causal_b = causal_a[:, :ATT_TILE]
    meta_fill = jnp.zeros((META_PAD - N_META, Q_TILE), jnp.bfloat16)
    acc_zero = jnp.zeros(acc_ref.shape[2:], jnp.float32)

    def hrows(g, width):
        return slice(g * width, (g + 1) * width)

    def colmax(s):
        return jnp.max(s, axis=0, keepdims=True)

    def key_tile(g, kj):
        rows = pl.ds(pl.multiple_of(kj * ATT_TILE, ATT_TILE), ATT_TILE)
        return k_ref[0, rows, hrows(g, KEY_WIDTH)]

    def value_tile(g, kj):
        return jnp.concatenate([vT_ref[0, kj, hrows(g, HEAD_DIM), :], ones], axis=0)

    def head_scores(g, kj, qa, dst):
        s = jnp.dot(key_tile(g, kj), qa[g], preferred_element_type=jnp.float32)
        dst[g] = s
        return colmax(s)

    def head_consume(g, kj, src, acc, smax, m):
        m_new = jnp.maximum(m, smax)
        alpha = jnp.exp2(m - m_new)
        p = jnp.exp2(src[g] - m_new).astype(jnp.bfloat16)
        acc[g] = alpha * acc[g] + jnp.dot(value_tile(g, kj), p, preferred_element_type=jnp.float32)
        return m_new

    def start_block(g, qi, qa, smeta, dst):
        qa[g, :HEAD_DIM, :] = qT_ref[0, qi, hrows(g, HEAD_DIM), :]
        smeta[g] = jnp.dot(kmeta_ref[:N_META, hrows(g, KEY_WIDTH)], qa[g],
                           preferred_element_type=jnp.float32)
        return head_scores(g, 0, qa, dst)

    def block(qi, pos, smax_first):
        par = pos % 2
        qa, qa_nx = qa_ref.at[par], qa_ref.at[1 - par]
        smeta, smeta_nx = smeta_ref.at[par], smeta_ref.at[1 - par]
        acc, acc_nx = acc_ref.at[par], acc_ref.at[1 - par]
        s_first, s_other, s_nx = s_ref.at[2 * par], s_ref.at[1], s_ref.at[2 - 2 * par]
        m0 = tuple(jnp.full((1, Q_TILE), -jnp.inf, jnp.float32) for _ in heads)

        def step(k_next, dst, k_cur, src, smax_cur, ms):
            smax_next, ms_new = [], []
            for g in heads:
                smax_next.append(head_scores(g, k_next, qa, dst))
                ms_new.append(head_consume(g, k_cur, src, acc, smax_cur[g], ms[g]))
            return tuple(ms_new), tuple(smax_next)

        def pair(t, ms, smax):
            ms, smax_o = step(t + 1, s_other, t, s_first, smax, ms)
            return step(t + 2, s_first, t + 1, s_other, smax_o, ms)

        def quad(k, carry):
            return pair(4 * k + 2, *pair(4 * k, *carry))

        def octo(k, carry):
            return quad(2 * k + 1, quad(2 * k, carry))

        ms, smax = lax.fori_loop(0, qi // 4, octo, (m0, smax_first))
        for t in range(pos):
            ms, smax = pair(2 * qi - 2 * pos + 2 * t, ms, smax)

        q_next = jnp.minimum(qi + 1, n_qb - 1)
        sbs = [jnp.dot(key_tile(g, 2 * qi + 1), qa[g][:, ATT_TILE:],
                       preferred_element_type=jnp.float32) for g in heads]
        smax_next = [start_block(g, q_next, qa_nx, smeta_nx, s_nx) for g in heads]
        for g in heads:
            sb = jnp.where(causal_b, sbs[g], -jnp.inf)
            s = jnp.where(causal_a, s_first[g], -jnp.inf)
            sm = smeta[g]
            m_new = jnp.maximum(jnp.maximum(ms[g], colmax(s)), colmax(sm))
            m_new = jnp.concatenate(
                [m_new[:, :ATT_TILE], jnp.maximum(m_new[:, ATT_TILE:], colmax(sb))], axis=1)
            alpha = jnp.exp2(ms[g] - m_new)
            p = jnp.exp2(s - m_new).astype(jnp.bfloat16)
            pb = jnp.exp2(sb - m_new[:, ATT_TILE:]).astype(jnp.bfloat16)
            pm = jnp.concatenate([jnp.exp2(sm - m_new).astype(jnp.bfloat16), meta_fill], axis=0)
            acc_nx[g] = acc_zero
            vmeta = jnp.concatenate([vTmeta_ref[hrows(g, HEAD_DIM), :], ones_meta], axis=0)
            a = (alpha * acc[g]
                 + jnp.dot(value_tile(g, 2 * qi), p, preferred_element_type=jnp.float32)
                 + jnp.dot(vmeta, pm, preferred_element_type=jnp.float32))
            a_late = a[:, ATT_TILE:] + jnp.dot(value_tile(g, 2 * qi + 1), pb,
                                               preferred_element_type=jnp.float32)
            a = jnp.concatenate([a[:, :ATT_TILE], a_late], axis=1)
            o = a[:HEAD_DIM, :] / a[HEAD_DIM:HEAD_DIM + 1, :]
            o = o * sgT_ref[0, qi, hrows(g, HEAD_DIM), :].astype(jnp.float32)
            o_ref[0, qi, hrows(g, HEAD_DIM), :] = o.astype(jnp.bfloat16)
        return tuple(smax_next)

    def four_blocks(j, smax_first):
        for pos in range(4):
            smax_first = block(4 * j + pos, pos, smax_first)
        return smax_first

    smax_first = []
    for g in heads:
        acc_ref[0, g] = acc_zero
        smax_first.append(start_block(g, 0, qa_ref.at[0], smeta_ref.at[0], s_ref.at[0]))
    lax.fori_loop(0, n_qb // 4, four_blocks, tuple(smax_first))


def _out_proj_kernel(x_ref, gin_ref, bin_ref, aT_ref, p_ref, wo_ref, g_ref, b_ref, o_ref,
                     waT_ref, wp_ref):
    @pl.when((pl.program_id(0) == 0) & (pl.program_id(1) == 0))
    def _():
        waT_ref[...] = wo_ref[0, :ATT_WIDTH, :].T.astype(jnp.bfloat16)
        wp_ref[...] = wo_ref[0, ATT_WIDTH:, :].astype(jnp.bfloat16)

    for j in range(OUT_TILE // Q_TILE):
        rows = slice(j * Q_TILE, (j + 1) * Q_TILE)
        hn = _layer_norm(x_ref[0, rows, :], gin_ref[...], bin_ref[...])
        yT = jnp.dot(waT_ref[...], aT_ref[0, j], preferred_element_type=jnp.float32)
        y = yT.T + jnp.dot(p_ref[0, rows, :], wp_ref[...], preferred_element_type=jnp.float32)
        o_ref[0, rows, :] = _layer_norm(DEEPNORM_ALPHA * hn + y, g_ref[...], b_ref[...])


def _const_spec(shape):
    nd = len(shape)
    return pl.BlockSpec(shape, lambda *_: (0,) * nd, pipeline_mode=pl.Buffered(1))


def kernel(x, meta_tokens, ln_in_g, ln_in_b, w_in, b_forget, w_pool, pool_scale, w_out, ln_g, ln_b):
    B, L, D = x.shape
    assert D == D_MODEL and L % (4 * Q_TILE) == 0 and TOKEN_TILE % Q_TILE == 0 and w_in.shape[0] == 1
    bf16, f32 = jnp.bfloat16, jnp.float32
    assert w_in.shape[2] == IN_COLS and w_out.shape[1] == 2 * D_MODEL

    w_t = jnp.swapaxes(w_in, 1, 2)[0]
    bf_col = jnp.pad(b_forget[0].astype(f32), (0, LANES - HEADS)).reshape(LANES, 1)
    wpool = w_pool[0].astype(bf16)
    pscale = pool_scale[0].reshape(1, D).astype(f32)
    g_in = ln_in_g.reshape(1, D).astype(f32)
    b_in = ln_in_b.reshape(1, D).astype(f32)
    g_out = ln_g[0].reshape(1, D).astype(f32)
    b_out = ln_b[0].reshape(1, D).astype(f32)
    mt_pad = jnp.pad(meta_tokens.astype(f32), ((0, META_PAD - N_META), (0, 0)))

    kw = HEADS * KEY_WIDTH
    whole = lambda shape: pl.BlockSpec(shape, lambda r: (0,) * len(shape))
    w_bf, kmeta, vTmeta, umeta = pl.pallas_call(
        _cast_meta_kernel,
        grid=(IN_COLS // CAST_ROWS,),
        in_specs=[pl.BlockSpec((CAST_ROWS, D), lambda r: (r, 0)),
                  whole((META_PAD, D)), whole((1, D)), whole((1, D)), whole((LANES, 1))],
        out_specs=(pl.BlockSpec((CAST_ROWS, D), lambda r: (r, 0)),
                   whole((META_PAD, kw)), whole((ATT_WIDTH, META_PAD)), whole((N_META, D))),
        out_shape=(jax.ShapeDtypeStruct((IN_COLS, D), bf16),
                   jax.ShapeDtypeStruct((META_PAD, kw), bf16),
                   jax.ShapeDtypeStruct((ATT_WIDTH, META_PAD), bf16),
                   jax.ShapeDtypeStruct((N_META, D), f32)),
        scratch_shapes=[pltpu.VMEM((IN_COLS, D), bf16)],
        compiler_params=pltpu.CompilerParams(
            dimension_semantics=("arbitrary",), vmem_limit_bytes=VMEM_LIMIT),
        name="cast_meta_proj",
    )(w_t, mt_pad, g_in, b_in, bf_col)

    n_t = L // TOKEN_TILE
    n_sub = TOKEN_TILE // ATT_TILE
    n_ab = L // ATT_TILE
    ktiled = jax.ShapeDtypeStruct((B, n_ab, ATT_WIDTH, ATT_TILE), bf16)
    ktiled_spec = pl.BlockSpec((1, n_sub, ATT_WIDTH, ATT_TILE), lambda b, t: (b, t, 0, 0))
    n_qb = L // Q_TILE
    qtiled = jax.ShapeDtypeStruct((B, n_qb, ATT_WIDTH, Q_TILE), bf16)
    qtiled_spec = pl.BlockSpec((1, TOKEN_TILE // Q_TILE, ATT_WIDTH, Q_TILE), lambda b, t: (b, t, 0, 0))
    qT, kk, vT, sgT, pmix = pl.pallas_call(
        _in_proj_kernel,
        grid=(B, n_t),
        in_specs=[
            pl.BlockSpec((1, TOKEN_TILE, D), lambda b, t: (b, t, 0)),
            _const_spec((1, D)), _const_spec((1, D)),
            _const_spec((IN_COLS, D)), _const_spec((LANES, 1)),
            _const_spec((len(POOL_WINDOWS), POOL_GROUP_WIDTH, POOL_GROUP_WIDTH)),
            _const_spec((1, D)), _const_spec((N_META, D)),
        ],
        out_specs=(qtiled_spec,
                   pl.BlockSpec((1, TOKEN_TILE, kw), lambda b, t: (b, t, 0)),
                   ktiled_spec, qtiled_spec,
                   pl.BlockSpec((1, TOKEN_TILE, D), lambda b, t: (b, t, 0))),
        out_shape=(qtiled, jax.ShapeDtypeStruct((B, L, kw), bf16), ktiled, qtiled,
                   jax.ShapeDtypeStruct((B, L, D), bf16)),
        scratch_shapes=[pltpu.VMEM((HEADS, LANES), f32), pltpu.VMEM((N_META, D), f32)],
        compiler_params=pltpu.CompilerParams(
            dimension_semantics=("arbitrary", "arbitrary"), vmem_limit_bytes=VMEM_LIMIT),
        name="in_proj",
    )(x, g_in, b_in, w_bf, bf_col, wpool, pscale, umeta)

    gd = HEAD_GROUP * HEAD_DIM
    head_ktiled = pl.BlockSpec((1, n_ab, gd, ATT_TILE), lambda b, h: (b, 0, h, 0))
    head_qtiled = pl.BlockSpec((1, n_qb, gd, Q_TILE), lambda b, h: (b, 0, h, 0))
    aT = pl.pallas_call(
        _attn_kernel,
        grid=(B, HEADS // HEAD_GROUP),
        in_specs=[
            head_qtiled,
            pl.BlockSpec((1, L, HEAD_GROUP * KEY_WIDTH), lambda b, h: (b, 0, h)),
            head_ktiled, head_qtiled,
            pl.BlockSpec((META_PAD, HEAD_GROUP * KEY_WIDTH), lambda b, h: (0, h)),
            pl.BlockSpec((gd, META_PAD), lambda b, h: (h, 0)),
        ],
        out_specs=head_qtiled,
        out_shape=qtiled,
        scratch_shapes=[pltpu.VMEM((2, HEAD_GROUP, KEY_WIDTH, Q_TILE), bf16),
                        pltpu.VMEM((3, HEAD_GROUP, ATT_TILE, Q_TILE), f32),
                        pltpu.VMEM((2, HEAD_GROUP, N_META, Q_TILE), f32),
                        pltpu.VMEM((2, HEAD_GROUP, HEAD_DIM + ONES_ROWS, Q_TILE), f32)],
        compiler_params=pltpu.CompilerParams(
            dimension_semantics=("arbitrary", "arbitrary"), vmem_limit_bytes=VMEM_LIMIT),
        name="fox_attn",
    )(qT, kk, vT, sgT, kmeta, vTmeta)

    out = pl.pallas_call(
        _out_proj_kernel,
        grid=(B, L // OUT_TILE),
        in_specs=[
            pl.BlockSpec((1, OUT_TILE, D), lambda b, t: (b, t, 0)),
            _const_spec((1, D)), _const_spec((1, D)),
            pl.BlockSpec((1, OUT_TILE // Q_TILE, ATT_WIDTH, Q_TILE), lambda b, t: (b, t, 0, 0)),
            pl.BlockSpec((1, OUT_TILE, D), lambda b, t: (b, t, 0)),
            _const_spec((1, 2 * D, D)),
            _const_spec((1, D)), _const_spec((1, D)),
        ],
        out_specs=pl.BlockSpec((1, OUT_TILE, D), lambda b, t: (b, t, 0)),
        out_shape=jax.ShapeDtypeStruct((B, L, D), x.dtype),
        scratch_shapes=[pltpu.VMEM((D, ATT_WIDTH), bf16), pltpu.VMEM((D, D), bf16)],
        compiler_params=pltpu.CompilerParams(
            dimension_semantics=("arbitrary", "arbitrary"), vmem_limit_bytes=VMEM_LIMIT),
        name="out_proj",
    )(x, g_in, b_in, aT, pmix, w_out, g_out, b_out)
    return out
```
